```python
import math
import jax, jax.numpy as jnp
from jax import lax
import numpy as np

D_MODEL = 2048
BATCH = 1
SEQ = 8192
DEPTH = 1

D_MIX = D_MODEL
ATTN_WIDTH = D_MIX // 2
LRU_WIDTH = D_MIX - ATTN_WIDTH
HEAD_DIM = 64
N_HEADS = ATTN_WIDTH // HEAD_DIM
LRU_BLOCKS = 16
LRU_BLOCK_W = LRU_WIDTH // LRU_BLOCKS
CONV_WIDTH = 4
LRU_C = 8.0
DILATED_PATTERNS = ((128, 1), (512, 4), (2048, 16))
Q_BLOCK = 128
ATTN_SCALE = 1.0 / math.sqrt(HEAD_DIM)
NEG_INF = -1e30
EPS = 1e-6
PROJ_WIDTH = 4 * ATTN_WIDTH + 2 * LRU_WIDTH

kernel_name = "hymba_dilated_attn_rglru_adaln"


def rmsnorm(x, g):
    xf = x.astype(jnp.float32)
    var = jnp.mean(xf * xf, axis=-1, keepdims=True)
    return xf * lax.rsqrt(var + EPS) * g.astype(jnp.float32)


def alibi_slopes(n_heads):
    return 2.0 ** (-8.0 * jnp.arange(1, n_heads + 1, dtype=jnp.float32) / n_heads)


def dilated_attention(q, k, v):
    B, S, _ = q.shape
    q = q.reshape(B, S, N_HEADS, HEAD_DIM)
    k = k.reshape(B, S, N_HEADS, HEAD_DIM)
    v = v.reshape(B, S, N_HEADS, HEAD_DIM)
    slopes = alibi_slopes(N_HEADS)
    nb = S // Q_BLOCK
    q_blocks = q.reshape(B, nb, Q_BLOCK, N_HEADS, HEAD_DIM).transpose(1, 0, 2, 3, 4)
    starts = jnp.arange(nb, dtype=jnp.int32) * Q_BLOCK

    def one_block(args):
        qb, q0 = args
        qb = qb.astype(jnp.float32) * ATTN_SCALE
        t = q0 + jnp.arange(Q_BLOCK, dtype=jnp.int32)
        outs, lses = [], []
        for window, dilation in DILATED_PATTERNS:
            offs = jnp.arange(window // dilation + 1, dtype=jnp.int32) * dilation
            idx = t[:, None] - offs[None, :]
            valid = idx >= 0
            idx = jnp.maximum(idx, 0)
            kg = jnp.take(k, idx, axis=1).astype(jnp.float32)
            vg = jnp.take(v, idx, axis=1).astype(jnp.float32)
            s = jnp.einsum('bqhd,bqnhd->bhqn', qb, kg)
            s = s - slopes[:, None, None] * offs.astype(jnp.float32)[None, None, :]
            s = jnp.where(valid[None, None], s, NEG_INF)
            m = jnp.max(s, axis=-1, keepdims=True)
            p = jnp.exp(s - m)
            l = jnp.sum(p, axis=-1)
            o = jnp.einsum('bhqn,bqnhd->bqhd', p, vg) / jnp.transpose(l, (0, 2, 1))[..., None]
            outs.append(o)
            lses.append(m[..., 0] + jnp.log(l))
        w = jax.nn.softmax(jnp.stack(lses, axis=0), axis=0)
        return jnp.einsum('pbhq,pbqhd->bqhd', w, jnp.stack(outs, axis=0))

    out = lax.map(one_block, (q_blocks, starts))
    return out.transpose(1, 0, 2, 3, 4).reshape(B, S, ATTN_WIDTH)


def rglru_branch(u, conv_w, conv_b, w_rgate, b_rgate, w_igate, b_igate, lru_lambda):
    B, S, W = u.shape
    uf = u.astype(jnp.float32)
    up = jnp.pad(uf, ((0, 0), (CONV_WIDTH - 1, 0), (0, 0)))
    xc = conv_b.astype(jnp.float32)
    for j in range(CONV_WIDTH):
        xc = xc + up[:, j:j + S] * conv_w[j].astype(jnp.float32)
    xb = xc.reshape(B, S, LRU_BLOCKS, LRU_BLOCK_W)
    r = jax.nn.sigmoid(jnp.einsum('bsnk,nkj->bsnj', xb, w_rgate.astype(jnp.float32)) + b_rgate.astype(jnp.float32)).reshape(B, S, W)
    i = jax.nn.sigmoid(jnp.einsum('bsnk,nkj->bsnj', xb, w_igate.astype(jnp.float32)) + b_igate.astype(jnp.float32)).reshape(B, S, W)
    log_a = LRU_C * r * jax.nn.log_sigmoid(lru_lambda.astype(jnp.float32))
    a = jnp.exp(log_a)
    mult = jnp.sqrt(-jnp.expm1(2.0 * log_a))
    b = mult * (i * xc)

    def combine(e1, e2):
        a1, b1 = e1
        a2, b2 = e2
        return a1 * a2, a2 * b1 + b2

    _, h = lax.associative_scan(combine, (a, b), axis=1)
    return h


def setup_inputs(seed: int = 0) -> dict:
    key = jax.random.key(seed)
    ks = jax.random.split(key, 16)
    f32 = jnp.float32
    x = jax.random.normal(ks[0], (BATCH, SEQ, D_MODEL), f32)
    c = jax.random.normal(ks[1], (BATCH, D_MODEL), f32)
    norm_gain = jnp.ones((DEPTH, D_MODEL), f32) + 0.02 * jax.random.normal(ks[2], (DEPTH, D_MODEL), f32)
    w_ada = jax.random.normal(ks[3], (DEPTH, D_MODEL, 3 * D_MODEL), f32) * D_MODEL ** -0.5
    b_ada = 0.01 * jax.random.normal(ks[4], (DEPTH, 3 * D_MODEL), f32)
    w_in = jax.random.normal(ks[5], (DEPTH, D_MODEL, PROJ_WIDTH), f32) * D_MODEL ** -0.5
    conv_w = jax.random.normal(ks[6], (DEPTH, CONV_WIDTH, LRU_WIDTH), f32) * CONV_WIDTH ** -0.5
    conv_b = 0.01 * jax.random.normal(ks[7], (DEPTH, LRU_WIDTH), f32)
    w_rgate = jax.random.normal(ks[8], (DEPTH, LRU_BLOCKS, LRU_BLOCK_W, LRU_BLOCK_W), f32) * LRU_BLOCK_W ** -0.5
    b_rgate = 0.01 * jax.random.normal(ks[9], (DEPTH, LRU_BLOCKS, LRU_BLOCK_W), f32)
    w_igate = jax.random.normal(ks[10], (DEPTH, LRU_BLOCKS, LRU_BLOCK_W, LRU_BLOCK_W), f32) * LRU_BLOCK_W ** -0.5
    b_igate = 0.01 * jax.random.normal(ks[11], (DEPTH, LRU_BLOCKS, LRU_BLOCK_W), f32)
    a0 = jax.random.uniform(ks[12], (DEPTH, LRU_WIDTH), f32, 0.9, 0.999)
    s0 = a0 ** (1.0 / LRU_C)
    lru_lambda = jnp.log(s0) - jnp.log1p(-s0)
    w_out = jax.random.normal(ks[13], (DEPTH, D_MIX, D_MODEL), f32) * D_MIX ** -0.5
    final_gain = jnp.ones((D_MODEL,), f32) + 0.02 * jax.random.normal(ks[14], (D_MODEL,), f32)
    return {"x": x, "c": c, "norm_gain": norm_gain, "w_ada": w_ada, "b_ada": b_ada,
            "w_in": w_in, "conv_w": conv_w, "conv_b": conv_b, "w_rgate": w_rgate,
            "b_rgate": b_rgate, "w_igate": w_igate, "b_igate": b_igate,
            "lru_lambda": lru_lambda, "w_out": w_out, "final_gain": final_gain}


def reference(x, c, norm_gain, w_ada, b_ada, w_in, conv_w, conv_b, w_rgate, b_rgate,
              w_igate, b_igate, lru_lambda, w_out, final_gain):
    y = x.astype(jnp.float32)
    c_act = jax.nn.silu(c.astype(jnp.float32))
    for l in range(DEPTH):
        mod = c_act @ w_ada[l].astype(jnp.float32) + b_ada[l].astype(jnp.float32)
        shift, scale, gate = jnp.split(mod, 3, axis=-1)
        h = rmsnorm(y, norm_gain[l]) * (1.0 + scale[:, None, :]) + shift[:, None, :]
        proj = h @ w_in[l].astype(jnp.float32)
        A = ATTN_WIDTH
        q, k, v, g_attn, u, g_lru = jnp.split(
            proj, [A, 2 * A, 3 * A, 4 * A, 4 * A + LRU_WIDTH], axis=-1)
        attn = dilated_attention(q, k, v)
        lru = rglru_branch(u, conv_w[l], conv_b[l], w_rgate[l], b_rgate[l],
                           w_igate[l], b_igate[l], lru_lambda[l])
        mixed = jnp.concatenate([attn * jax.nn.silu(g_attn), lru * jax.nn.silu(g_lru)], axis=-1)
        y = y + gate[:, None, :] * (mixed @ w_out[l].astype(jnp.float32))
    return rmsnorm(y, final_gain)
```

```python
import functools
import math

import jax
import jax.numpy as jnp
from jax import lax
from jax.experimental import pallas as pl
from jax.experimental.pallas import tpu as pltpu

D_MODEL = 2048
SEQ = 8192
ATTN_WIDTH = D_MODEL // 2
LRU_WIDTH = D_MODEL - ATTN_WIDTH
HEAD_DIM = 64
N_HEADS = ATTN_WIDTH // HEAD_DIM
LRU_BLOCKS = 16
LRU_BLOCK_W = LRU_WIDTH // LRU_BLOCKS
CONV_WIDTH = 4
LRU_C = 8.0
DILATED_PATTERNS = ((128, 1), (512, 4), (2048, 16))
ATTN_SCALE = 1.0 / math.sqrt(HEAD_DIM)
NEG_INF = -1e30
EPS = 1e-6
PROJ_WIDTH = 4 * ATTN_WIDTH + 2 * LRU_WIDTH
MOD_SHIFT, MOD_SCALE, MOD_GATE = 0, 1, 2

LANES = 128
SUBLANES = 8
VMEM_LIMIT_BYTES = 56 * 1024 * 1024

ADA_TN = 512
PROJ_TM = 1024
PROJ_TN = 512
NORM_ROWS = 64
ATTN_QB = 128
ATTN_SB = 2048
HEADS_PER_BLOCK = LANES // HEAD_DIM
COMBINE_ROWS = 256
LRU_CHUNK = 256
OUT_TM = 512

F32 = jnp.float32
BF16 = jnp.bfloat16


def _sigmoid(x):
    return 1.0 / (1.0 + jnp.exp(-x))


def _silu(x):
    return x * _sigmoid(x)


def _ada_kernel(c_ref, w_ref, b_ref, o_ref):
    c_act = _silu(c_ref[...])
    acc = jnp.sum(w_ref[...] * c_act, axis=0, keepdims=True)
    o_ref[...] = acc + b_ref[...]


def _ada_mod(c_col, w_ada, b_ada):
    n = w_ada.shape[1]
    return pl.pallas_call(
        _ada_kernel,
        grid=(n // ADA_TN,),
        in_specs=[
            pl.BlockSpec((D_MODEL, 1), lambda j: (0, 0)),
            pl.BlockSpec((D_MODEL, ADA_TN), lambda j: (0, j)),
            pl.BlockSpec((1, ADA_TN), lambda j: (0, j)),
        ],
        out_specs=pl.BlockSpec((1, ADA_TN), lambda j: (0, j)),
        out_shape=jax.ShapeDtypeStruct((1, n), F32),
        compiler_params=pltpu.CompilerParams(
            dimension_semantics=("arbitrary",), vmem_limit_bytes=VMEM_LIMIT_BYTES),
        name="ada_mod",
    )(c_col, w_ada, b_ada)


def _proj_kernel(x_ref, gain_ref, shift_ref, scale_ref, w_ref, o_ref, h_ref):
    @pl.when(pl.program_id(1) == 0)
    def _():
        mul = gain_ref[...] * (1.0 + scale_ref[...])
        shift = shift_ref[...]

        def body(i, carry):
            r0 = pl.multiple_of(i * NORM_ROWS, NORM_ROWS)
            xv = x_ref[pl.ds(r0, NORM_ROWS), :]
            var = jnp.mean(xv * xv, axis=-1, keepdims=True)
            h = xv * lax.rsqrt(var + EPS) * mul + shift
            h_ref[pl.ds(r0, NORM_ROWS), :] = h.astype(BF16)
            return carry

        lax.fori_loop(0, PROJ_TM // NORM_ROWS, body, 0)

    o_ref[...] = jnp.dot(h_ref[...], w_ref[...], preferred_element_type=F32)


def _in_proj(x2d, gain, mod, w_in_bf16):
    return pl.pallas_call(
        _proj_kernel,
        grid=(SEQ // PROJ_TM, PROJ_WIDTH // PROJ_TN),
        in_specs=[
            pl.BlockSpec((PROJ_TM, D_MODEL), lambda i, j: (i, 0)),
            pl.BlockSpec((1, D_MODEL), lambda i, j: (0, 0)),
            pl.BlockSpec((1, D_MODEL), lambda i, j: (0, MOD_SHIFT)),
            pl.BlockSpec((1, D_MODEL), lambda i, j: (0, MOD_SCALE)),
            pl.BlockSpec((D_MODEL, PROJ_TN), lambda i, j: (0, j)),
        ],
        out_specs=pl.BlockSpec((PROJ_TM, PROJ_TN), lambda i, j: (i, j)),
        out_shape=jax.ShapeDtypeStruct((SEQ, PROJ_WIDTH), F32),
        scratch_shapes=[pltpu.VMEM((PROJ_TM, D_MODEL), BF16)],
        compiler_params=pltpu.CompilerParams(
            dimension_semantics=("arbitrary", "arbitrary"),
            vmem_limit_bytes=VMEM_LIMIT_BYTES),
        name="in_proj",
    )(x2d, gain, mod, mod, w_in_bf16)


def _rows(start, size, stride):
    return pl.ds(start, size) if stride == 1 else pl.ds(start, size, stride=stride)


def _attn_kernel(q_ref, k_ref, v_ref, g_ref, bias_ref, o_ref, m_s, l_s, acc_s):
    t0 = pl.program_id(1) * ATTN_SB
    lane = lax.broadcasted_iota(jnp.int32, (ATTN_QB, LANES), 1)
    is_h0 = lane < HEAD_DIM

    for p, (_, d) in enumerate(DILATED_PATTERNS):
        log2d = d.bit_length() - 1
        span = ATTN_QB * d

        def tile(ti, carry, p=p, d=d, log2d=log2d, span=span):
            r = jnp.bitwise_and(ti, d - 1)
            n = lax.shift_right_logical(ti, log2d)
            loc = r + n * span
            cur = t0 + loc
            first = cur < span
            prev = jnp.where(first, cur, cur - span)

            q = q_ref[_rows(loc, ATTN_QB, d), :] * ATTN_SCALE
            zero = jnp.zeros_like(q)
            qs = jnp.concatenate(
                [jnp.where(is_h0, q, zero), jnp.where(is_h0, zero, q)], axis=0).astype(BF16)
            kc = jnp.concatenate(
                [k_ref[_rows(prev, ATTN_QB, d), :], k_ref[_rows(cur, ATTN_QB, d), :]],
                axis=0).astype(BF16)
            vc = jnp.concatenate(
                [v_ref[_rows(prev, ATTN_QB, d), :], v_ref[_rows(cur, ATTN_QB, d), :]],
                axis=0).astype(BF16)

            s = lax.dot_general(qs, kc, (((1,), (1,)), ((), ())), preferred_element_type=F32)
            s = s + bias_ref[p, first.astype(jnp.int32)]
            m = jnp.max(s, axis=1, keepdims=True)
            e = jnp.exp(s - m)
            l = jnp.sum(e, axis=1, keepdims=True)
            pv = jnp.dot(e.astype(BF16), vc, preferred_element_type=F32)

            rows = _rows(loc, ATTN_QB, d)
            acc_s[p, rows, :] = jnp.where(is_h0, pv[:ATTN_QB], pv[ATTN_QB:])
            m_s[p, rows, :] = jnp.where(is_h0, m[:ATTN_QB], m[ATTN_QB:])
            l_s[p, rows, :] = jnp.where(is_h0, l[:ATTN_QB], l[ATTN_QB:])
            return carry

        lax.fori_loop(0, ATTN_SB // ATTN_QB, tile, 0)

    def combine(i, carry):
        rows = pl.ds(pl.multiple_of(i * COMBINE_ROWS, COMBINE_ROWS), COMBINE_ROWS)
        m0, m1, m2 = m_s[0, rows, :], m_s[1, rows, :], m_s[2, rows, :]
        mm = jnp.maximum(jnp.maximum(m0, m1), m2)
        e0, e1, e2 = jnp.exp(m0 - mm), jnp.exp(m1 - mm), jnp.exp(m2 - mm)
        num = e0 * acc_s[0, rows, :] + e1 * acc_s[1, rows, :] + e2 * acc_s[2, rows, :]
        den = e0 * l_s[0, rows, :] + e1 * l_s[1, rows, :] + e2 * l_s[2, rows, :]
        o_ref[rows, :] = (num / den * _silu(g_ref[rows, :])).astype(BF16)
        return carry

    lax.fori_loop(0, ATTN_SB // COMBINE_ROWS, combine, 0)


def _attn_bias():
    slopes = 2.0 ** (-8.0 * jnp.arange(1, N_HEADS + 1, dtype=F32) / N_HEADS)
    a = jnp.arange(ATTN_QB, dtype=jnp.int32)[:, None]
    b = jnp.arange(2 * ATTN_QB, dtype=jnp.int32)[None, :]
    dist = a + ATTN_QB - b
    valid = (dist >= 0) & (dist <= ATTN_QB)
    per_pattern = []
    for _, d in DILATED_PATTERNS:
        offs = (dist * d).astype(F32)
        pen = -(slopes[:, None, None] * offs[None])
        normal = jnp.where(valid[None], pen, NEG_INF)
        no_prev = jnp.where((b >= ATTN_QB)[None], normal, NEG_INF)
        per_pattern.append(jnp.stack([normal, no_prev], axis=1))
    t = jnp.stack(per_pattern, axis=1)
    n_p = len(DILATED_PATTERNS)
    t = t.reshape(N_HEADS // HEADS_PER_BLOCK, HEADS_PER_BLOCK, n_p, 2, ATTN_QB, 2 * ATTN_QB)
    t = t.transpose(0, 2, 3, 1, 4, 5)
    return t.reshape(N_HEADS // HEADS_PER_BLOCK, n_p, 2, HEADS_PER_BLOCK * ATTN_QB, 2 * ATTN_QB)


def _attention(proj, bias):
    n_hb = ATTN_WIDTH // LANES
    n_p = len(DILATED_PATTERNS)
    return pl.pallas_call(
        _attn_kernel,
        grid=(n_hb, SEQ // ATTN_SB),
        in_specs=[
            pl.BlockSpec((ATTN_SB, LANES), lambda h, s: (s, h)),
            pl.BlockSpec((SEQ, LANES), lambda h, s: (0, n_hb + h)),
            pl.BlockSpec((SEQ, LANES), lambda h, s: (0, 2 * n_hb + h)),
            pl.BlockSpec((ATTN_SB, LANES), lambda h, s: (s, 3 * n_hb + h)),
            pl.BlockSpec((None, n_p, 2, HEADS_PER_BLOCK * ATTN_QB, 2 * ATTN_QB),
                         lambda h, s: (h, 0, 0, 0, 0)),
        ],
        out_specs=pl.BlockSpec((ATTN_SB, LANES), lambda h, s: (s, h)),
        out_shape=jax.ShapeDtypeStruct((SEQ, ATTN_WIDTH), BF16),
        scratch_shapes=[pltpu.VMEM((n_p, ATTN_SB, LANES), F32)] * 3,
        compiler_params=pltpu.CompilerParams(
            dimension_semantics=("arbitrary", "arbitrary"),
            vmem_limit_bytes=VMEM_LIMIT_BYTES),
        name="dilated_attn",
    )(proj, proj, proj, proj, bias)


def _lru_kernel(u_ref, g_ref, cw_ref, cb_ref, wg_ref, bg_ref, lam_ref, o_ref,
                ubuf, a_s, b_s):
    lam = lam_ref[...]
    log_sig = jnp.minimum(lam, 0.0) - jnp.log1p(jnp.exp(-jnp.abs(lam)))
    sub = lax.broadcasted_iota(jnp.int32, (LRU_CHUNK // SUBLANES, SUBLANES, LANES), 1)

    ubuf[pl.ds(0, SUBLANES), :] = jnp.zeros((SUBLANES, LANES), F32)

    def chunk(c, carry):
        rows = pl.ds(pl.multiple_of(c * LRU_CHUNK, LRU_CHUNK), LRU_CHUNK)
        ubuf[pl.ds(SUBLANES, LRU_CHUNK), :] = u_ref[rows, :]
        xc = cb_ref[...]
        for j in range(CONV_WIDTH):
            off = SUBLANES - (CONV_WIDTH - 1) + j
            xc = xc + ubuf[pl.ds(off, LRU_CHUNK), :] * cw_ref[pl.ds(j, 1), :]
        ubuf[pl.ds(0, SUBLANES), :] = ubuf[pl.ds(LRU_CHUNK, SUBLANES), :]

        z = jnp.dot(xc.astype(BF16), wg_ref[...], preferred_element_type=F32) + bg_ref[...]
        r = _sigmoid(z[:, :LANES])
        i = _sigmoid(z[:, LANES:])
        log_a = LRU_C * r * log_sig
        a = jnp.exp(log_a)
        th = jnp.tanh(log_a)
        mult = jnp.sqrt(-2.0 * th / (1.0 - th))
        b = mult * (i * xc)

        a3 = a.reshape(LRU_CHUNK // SUBLANES, SUBLANES, LANES)
        b3 = b.reshape(LRU_CHUNK // SUBLANES, SUBLANES, LANES)
        for k in (1, 2, 4):
            keep = sub >= k
            a_prev = jnp.where(keep, pltpu.roll(a3, k, 1), 1.0)
            b_prev = jnp.where(keep, pltpu.roll(b3, k, 1), 0.0)
            b3 = a3 * b_prev + b3
            a3 = a3 * a_prev
        a_s[rows, :] = a3.reshape(LRU_CHUNK, LANES)
        b_s[rows, :] = b3.reshape(LRU_CHUNK, LANES)
        return carry

    lax.fori_loop(0, SEQ // LRU_CHUNK, chunk, 0)

    def group(gi, c):
        r0 = pl.multiple_of(gi * SUBLANES, SUBLANES)
        rows = pl.ds(r0, SUBLANES)
        ag = a_s[rows, :]
        bg = b_s[rows, :]
        a_last = jnp.broadcast_to(a_s[pl.ds(r0 + SUBLANES - 1, 1), :], (SUBLANES, LANES))
        b_last = jnp.broadcast_to(b_s[pl.ds(r0 + SUBLANES - 1, 1), :], (SUBLANES, LANES))
        b_s[rows, :] = bg + ag * c
        return a_last * c + b_last

    lax.fori_loop(0, SEQ // SUBLANES, group, jnp.zeros((SUBLANES, LANES), F32), unroll=8)

    def gate(c, carry):
        rows = pl.ds(pl.multiple_of(c * LRU_CHUNK, LRU_CHUNK), LRU_CHUNK)
        o_ref[rows, :] = (b_s[rows, :] * _silu(g_ref[rows, :])).astype(BF16)
        return carry

    lax.fori_loop(0, SEQ // LRU_CHUNK, gate, 0)


def _lru(proj, conv_w, conv_b, w_gates, b_gates, lam):
    n_cb = LRU_WIDTH // LANES
    u_col0 = 4 * ATTN_WIDTH // LANES
    g_col0 = u_col0 + n_cb
    return pl.pallas_call(
        _lru_kernel,
        grid=(n_cb,),
        in_specs=[
            pl.BlockSpec((SEQ, LANES), lambda j: (0, u_col0 + j)),
            pl.BlockSpec((SEQ, LANES), lambda j: (0, g_col0 + j)),
            pl.BlockSpec((CONV_WIDTH, LANES), lambda j: (0, j)),
            pl.BlockSpec((1, LANES), lambda j: (0, j)),
            pl.BlockSpec((None, LANES, 2 * LANES), lambda j: (j, 0, 0)),
            pl.BlockSpec((None, 1, 2 * LANES), lambda j: (j, 0, 0)),
            pl.BlockSpec((1, LANES), lambda j: (0, j)),
        ],
        out_specs=pl.BlockSpec((SEQ, LANES), lambda j: (0, j)),
        out_shape=jax.ShapeDtypeStruct((SEQ, LRU_WIDTH), BF16),
        scratch_shapes=[
            pltpu.VMEM((LRU_CHUNK + SUBLANES, LANES), F32),
            pltpu.VMEM((SEQ, LANES), F32),
            pltpu.VMEM((SEQ, LANES), F32),
        ],
        compiler_params=pltpu.CompilerParams(
            dimension_semantics=("arbitrary",), vmem_limit_bytes=VMEM_LIMIT_BYTES),
        name="rglru",
    )(proj, proj, conv_w, conv_b, w_gates, b_gates, lam)


def _block_diag_gates(w_rgate, b_rgate, w_igate, b_igate):
    per = LANES // LRU_BLOCK_W
    n_cb = LRU_WIDTH // LANES

    def bd(w):
        w = w.reshape(n_cb, per, LRU_BLOCK_W, LRU_BLOCK_W)
        eye = jnp.eye(per, dtype=w.dtype)
        return jnp.einsum('cpkj,pq->cpkqj', w, eye).reshape(n_cb, LANES, LANES)

    w = jnp.concatenate([bd(w_rgate), bd(w_igate)], axis=-1).astype(BF16)
    b = jnp.concatenate([b_rgate.reshape(n_cb, 1, LANES), b_igate.reshape(n_cb, 1, LANES)], axis=-1)
    return w, b


def _out_kernel(ma_ref, ml_ref, wa_ref, wl_ref, x_ref, gate_ref, fg_ref, o_ref):
    mix = jnp.dot(ma_ref[...], wa_ref[...], preferred_element_type=F32)
    mix = mix + jnp.dot(ml_ref[...], wl_ref[...], preferred_element_type=F32)
    y = x_ref[...] + gate_ref[...] * mix
    var = jnp.mean(y * y, axis=-1, keepdims=True)
    o_ref[...] = y * lax.rsqrt(var + EPS) * fg_ref[...]


def _out_proj(mixed_attn, mixed_lru, w_out_bf16, x2d, mod, final_gain):
    return pl.pallas_call(
        _out_kernel,
        grid=(SEQ // OUT_TM,),
        in_specs=[
            pl.BlockSpec((OUT_TM, ATTN_WIDTH), lambda i: (i, 0)),
            pl.BlockSpec((OUT_TM, LRU_WIDTH), lambda i: (i, 0)),
            pl.BlockSpec((ATTN_WIDTH, D_MODEL), lambda i: (0, 0)),
            pl.BlockSpec((LRU_WIDTH, D_MODEL), lambda i: (1, 0)),
            pl.BlockSpec((OUT_TM, D_MODEL), lambda i: (i, 0)),
            pl.BlockSpec((1, D_MODEL), lambda i: (0, MOD_GATE)),
            pl.BlockSpec((1, D_MODEL), lambda i: (0, 0)),
        ],
        out_specs=pl.BlockSpec((OUT_TM, D_MODEL), lambda i: (i, 0)),
        out_shape=jax.ShapeDtypeStruct((SEQ, D_MODEL), F32),
        compiler_params=pltpu.CompilerParams(
            dimension_semantics=("arbitrary",), vmem_limit_bytes=VMEM_LIMIT_BYTES),
        name="out_proj",
    )(mixed_attn, mixed_lru, w_out_bf16, w_out_bf16, x2d, mod, final_gain)


@jax.jit
def kernel(x, c, norm_gain, w_ada, b_ada, w_in, conv_w, conv_b, w_rgate, b_rgate,
           w_igate, b_igate, lru_lambda, w_out, final_gain):
    assert x.shape == (1, SEQ, D_MODEL) and norm_gain.shape[0] == 1
    x2d = x.reshape(SEQ, D_MODEL)
    mod = _ada_mod(c.reshape(D_MODEL, 1), w_ada[0], b_ada)
    proj = _in_proj(x2d, norm_gain, mod, w_in[0].astype(BF16))
    mixed_attn = _attention(proj, _attn_bias())
    w_gates, b_gates = _block_diag_gates(w_rgate[0], b_rgate[0], w_igate[0], b_igate[0])
    mixed_lru = _lru(proj, conv_w[0], conv_b, w_gates, b_gates, lru_lambda)
    y = _out_proj(mixed_attn, mixed_lru, w_out[0].astype(BF16), x2d, mod,
                  final_gain.reshape(1, D_MODEL))
    return y.reshape(1, SEQ, D_MODEL)
```

```python
import functools
import math

import jax
import jax.numpy as jnp
from jax import lax
from jax.experimental import pallas as pl
from jax.experimental.pallas import tpu as pltpu

D_MODEL = 2048
SEQ = 8192
ATTN_WIDTH = D_MODEL // 2
LRU_WIDTH = D_MODEL - ATTN_WIDTH
HEAD_DIM = 64
N_HEADS = ATTN_WIDTH // HEAD_DIM
LRU_BLOCKS = 16
LRU_BLOCK_W = LRU_WIDTH // LRU_BLOCKS
CONV_WIDTH = 4
LRU_C = 8.0
DILATED_PATTERNS = ((128, 1), (512, 4), (2048, 16))
ATTN_SCALE = 1.0 / math.sqrt(HEAD_DIM)
NEG_INF = -1e30
EPS = 1e-6
PROJ_WIDTH = 4 * ATTN_WIDTH + 2 * LRU_WIDTH
MOD_SHIFT, MOD_SCALE, MOD_GATE = 0, 1, 2

LANES = 128
SUBLANES = 8
VMEM_LIMIT_BYTES = 56 * 1024 * 1024

ADA_TN = 512
PROJ_TM = 1024
PROJ_TN = 512
NORM_ROWS = 64
ATTN_QB = 128
ATTN_SB = 2048
ATTN_UNROLL = 8
ATTN_GROUP = 8
DEINT_ROWS = 256
ATTN_VMEM_LIMIT_BYTES = 60 * 1024 * 1024
HEADS_PER_BLOCK = LANES // HEAD_DIM
COMBINE_ROWS = 256
LRU_CHUNK = 256
OUT_TM = 512

F32 = jnp.float32
BF16 = jnp.bfloat16


def _sigmoid(x):
    return 1.0 / (1.0 + jnp.exp(-x))


def _silu(x):
    return x * _sigmoid(x)


def _ada_kernel(c_ref, w_ref, b_ref, o_ref):
    c_act = _silu(c_ref[...])
    acc = jnp.sum(w_ref[...] * c_act, axis=0, keepdims=True)
    o_ref[...] = acc + b_ref[...]


def _ada_mod(c_col, w_ada, b_ada):
    n = w_ada.shape[1]
    return pl.pallas_call(
        _ada_kernel,
        grid=(n // ADA_TN,),
        in_specs=[
            pl.BlockSpec((D_MODEL, 1), lambda j: (0, 0)),
            pl.BlockSpec((D_MODEL, ADA_TN), lambda j: (0, j)),
            pl.BlockSpec((1, ADA_TN), lambda j: (0, j)),
        ],
        out_specs=pl.BlockSpec((1, ADA_TN), lambda j: (0, j)),
        out_shape=jax.ShapeDtypeStruct((1, n), F32),
        compiler_params=pltpu.CompilerParams(
            dimension_semantics=("arbitrary",), vmem_limit_bytes=VMEM_LIMIT_BYTES),
        name="ada_mod",
    )(c_col, w_ada, b_ada)


def _proj_kernel(x_ref, gain_ref, shift_ref, scale_ref, w_ref, o_ref, h_ref):
    @pl.when(pl.program_id(1) == 0)
    def _():
        mul = gain_ref[...] * (1.0 + scale_ref[...])
        shift = shift_ref[...]

        def body(i, carry):
            r0 = pl.multiple_of(i * NORM_ROWS, NORM_ROWS)
            xv = x_ref[pl.ds(r0, NORM_ROWS), :]
            var = jnp.mean(xv * xv, axis=-1, keepdims=True)
            h = xv * lax.rsqrt(var + EPS) * mul + shift
            h_ref[pl.ds(r0, NORM_ROWS), :] = h.astype(BF16)
            return carry

        lax.fori_loop(0, PROJ_TM // NORM_ROWS, body, 0)

    o_ref[...] = jnp.dot(h_ref[...], w_ref[...], preferred_element_type=F32)


def _in_proj(x2d, gain, mod, w_in_bf16):
    return pl.pallas_call(
        _proj_kernel,
        grid=(SEQ // PROJ_TM, PROJ_WIDTH // PROJ_TN),
        in_specs=[
            pl.BlockSpec((PROJ_TM, D_MODEL), lambda i, j: (i, 0)),
            pl.BlockSpec((1, D_MODEL), lambda i, j: (0, 0)),
            pl.BlockSpec((1, D_MODEL), lambda i, j: (0, MOD_SHIFT)),
            pl.BlockSpec((1, D_MODEL), lambda i, j: (0, MOD_SCALE)),
            pl.BlockSpec((D_MODEL, PROJ_TN), lambda i, j: (0, j)),
        ],
        out_specs=pl.BlockSpec((PROJ_TM, PROJ_TN), lambda i, j: (i, j)),
        out_shape=jax.ShapeDtypeStruct((SEQ, PROJ_WIDTH), F32),
        scratch_shapes=[pltpu.VMEM((PROJ_TM, D_MODEL), BF16)],
        compiler_params=pltpu.CompilerParams(
            dimension_semantics=("arbitrary", "arbitrary"),
            vmem_limit_bytes=VMEM_LIMIT_BYTES),
        name="in_proj",
    )(x2d, gain, mod, mod, w_in_bf16)


def _rows(start, size, stride):
    return pl.ds(start, size) if stride == 1 else pl.ds(start, size, stride=stride)


def _attn_kernel(q_ref, k_ref, v_ref, g_ref, bias_ref, o_ref,
                 kd, vd, s_s, mx_s, m_s, l_s, acc_s):
    sb = pl.program_id(1)
    lane = lax.broadcasted_iota(jnp.int32, (ATTN_QB, LANES), 1)
    is_h0 = lane < HEAD_DIM

    @pl.when(sb == 0)
    def _():
        for p, (_, d) in enumerate(DILATED_PATTERNS):
            per_residue = SEQ // d // DEINT_ROWS
            shift = per_residue.bit_length() - 1
            kd[p, pl.ds(0, ATTN_QB), :] = jnp.zeros((ATTN_QB, LANES), BF16)
            vd[p, pl.ds(0, ATTN_QB), :] = jnp.zeros((ATTN_QB, LANES), BF16)

            def deint(c, carry, p=p, d=d, per_residue=per_residue, shift=shift):
                r = lax.shift_right_logical(c, shift)
                j = jnp.bitwise_and(c, per_residue - 1)
                src = _rows(r + j * (DEINT_ROWS * d), DEINT_ROWS, d)
                dst = pl.ds(pl.multiple_of(ATTN_QB + c * DEINT_ROWS, ATTN_QB), DEINT_ROWS)
                kd[p, dst, :] = k_ref[src, :].astype(BF16)
                vd[p, dst, :] = v_ref[src, :].astype(BF16)
                return carry

            lax.fori_loop(0, SEQ // DEINT_ROWS, deint, 0, unroll=2)

    ones = jnp.ones((2 * ATTN_QB, LANES), BF16)

    for p, (_, d) in enumerate(DILATED_PATTERNS):
        log2d = d.bit_length() - 1
        blocks_per_step = ATTN_SB // (ATTN_QB * d)

        def place(ti, d=d, log2d=log2d, blocks_per_step=blocks_per_step):
            r = jnp.bitwise_and(ti, d - 1)
            n = lax.shift_right_logical(ti, log2d)
            blk = sb * blocks_per_step + n
            q_rows = _rows(r + n * (ATTN_QB * d), ATTN_QB, d)
            window = pl.ds(pl.multiple_of(r * (SEQ // d) + blk * ATTN_QB, ATTN_QB), 2 * ATTN_QB)
            return q_rows, window, blk == 0

        for g0 in range(0, ATTN_SB // ATTN_QB, ATTN_GROUP):

            def scores(i, carry, p=p, g0=g0, place=place):
                q_rows, window, first = place(g0 + i)
                q = q_ref[q_rows, :] * ATTN_SCALE
                zero = jnp.zeros_like(q)
                qs = jnp.concatenate(
                    [jnp.where(is_h0, q, zero), jnp.where(is_h0, zero, q)], axis=0).astype(BF16)
                s = lax.dot_general(qs, kd[p, window, :], (((1,), (1,)), ((), ())),
                                    preferred_element_type=F32)
                s = s + bias_ref[p, first.astype(jnp.int32)]
                s_s[i] = s
                mx_s[i] = jnp.broadcast_to(jnp.max(s, axis=1, keepdims=True), (2 * ATTN_QB, LANES))
                return carry

            lax.fori_loop(0, ATTN_GROUP, scores, 0, unroll=ATTN_UNROLL)

            def weighted(i, carry, p=p, g0=g0, place=place):
                q_rows, window, _ = place(g0 + i)
                mx = mx_s[i]
                e = jnp.concatenate(
                    [jnp.exp(s_s[i, :, :LANES] - mx), jnp.exp(s_s[i, :, LANES:] - mx)],
                    axis=1).astype(BF16)
                vc = jnp.concatenate([vd[p, window, :], ones], axis=1)
                pv = jnp.dot(e, vc, preferred_element_type=F32)
                acc_s[p, q_rows, :] = jnp.where(is_h0, pv[:ATTN_QB, :LANES], pv[ATTN_QB:, :LANES])
                l_s[p, q_rows, :] = jnp.where(is_h0, pv[:ATTN_QB, LANES:], pv[ATTN_QB:, LANES:])
                m_s[p, q_rows, :] = jnp.where(is_h0, mx[:ATTN_QB], mx[ATTN_QB:])
                return carry

            lax.fori_loop(0, ATTN_GROUP, weighted, 0, unroll=ATTN_UNROLL)

    def combine(i, carry):
        rows = pl.ds(pl.multiple_of(i * COMBINE_ROWS, COMBINE_ROWS), COMBINE_ROWS)
        m0, m1, m2 = m_s[0, rows, :], m_s[1, rows, :], m_s[2, rows, :]
        mm = jnp.maximum(jnp.maximum(m0, m1), m2)
        e0, e1, e2 = jnp.exp(m0 - mm), jnp.exp(m1 - mm), jnp.exp(m2 - mm)
        num = e0 * acc_s[0, rows, :] + e1 * acc_s[1, rows, :] + e2 * acc_s[2, rows, :]
        den = e0 * l_s[0, rows, :] + e1 * l_s[1, rows, :] + e2 * l_s[2, rows, :]
        o_ref[rows, :] = (num / den * _silu(g_ref[rows, :])).astype(BF16)
        return carry

    lax.fori_loop(0, ATTN_SB // COMBINE_ROWS, combine, 0)


def _attn_bias():
    slopes = 2.0 ** (-8.0 * jnp.arange(1, N_HEADS + 1, dtype=F32) / N_HEADS)
    a = jnp.arange(ATTN_QB, dtype=jnp.int32)[:, None]
    b = jnp.arange(2 * ATTN_QB, dtype=jnp.int32)[None, :]
    dist = a + ATTN_QB - b
    valid = (dist >= 0) & (dist <= ATTN_QB)
    per_pattern = []
    for _, d in DILATED_PATTERNS:
        offs = (dist * d).astype(F32)
        pen = -(slopes[:, None, None] * offs[None])
        normal = jnp.where(valid[None], pen, NEG_INF)
        no_prev = jnp.where((b >= ATTN_QB)[None], normal, NEG_INF)
        per_pattern.append(jnp.stack([normal, no_prev], axis=1))
    t = jnp.stack(per_pattern, axis=1)
    n_p = len(DILATED_PATTERNS)
    t = t.reshape(N_HEADS // HEADS_PER_BLOCK, HEADS_PER_BLOCK, n_p, 2, ATTN_QB, 2 * ATTN_QB)
    t = t.transpose(0, 2, 3, 1, 4, 5)
    return t.reshape(N_HEADS // HEADS_PER_BLOCK, n_p, 2, HEADS_PER_BLOCK * ATTN_QB, 2 * ATTN_QB)


def _attention(proj, bias):
    n_hb = ATTN_WIDTH // LANES
    n_p = len(DILATED_PATTERNS)
    return pl.pallas_call(
        _attn_kernel,
        grid=(n_hb, SEQ // ATTN_SB),
        in_specs=[
            pl.BlockSpec((ATTN_SB, LANES), lambda h, s: (s, h)),
            pl.BlockSpec((SEQ, LANES), lambda h, s: (0, n_hb + h)),
            pl.BlockSpec((SEQ, LANES), lambda h, s: (0, 2 * n_hb + h)),
            pl.BlockSpec((ATTN_SB, LANES), lambda h, s: (s, 3 * n_hb + h)),
            pl.BlockSpec((None, n_p, 2, HEADS_PER_BLOCK * ATTN_QB, 2 * ATTN_QB),
                         lambda h, s: (h, 0, 0, 0, 0)),
        ],
        out_specs=pl.BlockSpec((ATTN_SB, LANES), lambda h, s: (s, h)),
        out_shape=jax.ShapeDtypeStruct((SEQ, ATTN_WIDTH), BF16),
        scratch_shapes=[
            pltpu.VMEM((n_p, ATTN_QB + SEQ, LANES), BF16),
            pltpu.VMEM((n_p, ATTN_QB + SEQ, LANES), BF16),
            pltpu.VMEM((ATTN_GROUP, 2 * ATTN_QB, 2 * ATTN_QB), F32),
            pltpu.VMEM((ATTN_GROUP, 2 * ATTN_QB, LANES), F32),
        ] + [pltpu.VMEM((n_p, ATTN_SB, LANES), F32)] * 3,
        compiler_params=pltpu.CompilerParams(
            dimension_semantics=("arbitrary", "arbitrary"),
            vmem_limit_bytes=ATTN_VMEM_LIMIT_BYTES),
        name="dilated_attn",
    )(proj, proj, proj, proj, bias)


def _lru_kernel(u_ref, g_ref, cw_ref, cb_ref, wg_ref, bg_ref, lam_ref, o_ref,
                ubuf, a_s, b_s):
    lam = lam_ref[...]
    log_sig = jnp.minimum(lam, 0.0) - jnp.log1p(jnp.exp(-jnp.abs(lam)))
    sub = lax.broadcasted_iota(jnp.int32, (LRU_CHUNK // SUBLANES, SUBLANES, LANES), 1)

    ubuf[pl.ds(0, SUBLANES), :] = jnp.zeros((SUBLANES, LANES), F32)

    def chunk(c, carry):
        rows = pl.ds(pl.multiple_of(c * LRU_CHUNK, LRU_CHUNK), LRU_CHUNK)
        ubuf[pl.ds(SUBLANES, LRU_CHUNK), :] = u_ref[rows, :]
        xc = cb_ref[...]
        for j in range(CONV_WIDTH):
            off = SUBLANES - (CONV_WIDTH - 1) + j
            xc = xc + ubuf[pl.ds(off, LRU_CHUNK), :] * cw_ref[pl.ds(j, 1), :]
        ubuf[pl.ds(0, SUBLANES), :] = ubuf[pl.ds(LRU_CHUNK, SUBLANES), :]

        z = jnp.dot(xc.astype(BF16), wg_ref[...], preferred_element_type=F32) + bg_ref[...]
        r = _sigmoid(z[:, :LANES])
        i = _sigmoid(z[:, LANES:])
        log_a = LRU_C * r * log_sig
        a = jnp.exp(log_a)
        th = jnp.tanh(log_a)
        mult = jnp.sqrt(-2.0 * th / (1.0 - th))
        b = mult * (i * xc)

        a3 = a.reshape(LRU_CHUNK // SUBLANES, SUBLANES, LANES)
        b3 = b.reshape(LRU_CHUNK // SUBLANES, SUBLANES, LANES)
        for k in (1, 2, 4):
            keep = sub >= k
            a_prev = jnp.where(keep, pltpu.roll(a3, k, 1), 1.0)
            b_prev = jnp.where(keep, pltpu.roll(b3, k, 1), 0.0)
            b3 = a3 * b_prev + b3
            a3 = a3 * a_prev
        a_s[rows, :] = a3.reshape(LRU_CHUNK, LANES)
        b_s[rows, :] = b3.reshape(LRU_CHUNK, LANES)
        return carry

    lax.fori_loop(0, SEQ // LRU_CHUNK, chunk, 0)

    def group(gi, c):
        r0 = pl.multiple_of(gi * SUBLANES, SUBLANES)
        rows = pl.ds(r0, SUBLANES)
        ag = a_s[rows, :]
        bg = b_s[rows, :]
        a_last = jnp.broadcast_to(a_s[pl.ds(r0 + SUBLANES - 1, 1), :], (SUBLANES, LANES))
        b_last = jnp.broadcast_to(b_s[pl.ds(r0 + SUBLANES - 1, 1), :], (SUBLANES, LANES))
        b_s[rows, :] = bg + ag * c
        return a_last * c + b_last

    lax.fori_loop(0, SEQ // SUBLANES, group, jnp.zeros((SUBLANES, LANES), F32), unroll=8)

    def gate(c, carry):
        rows = pl.ds(pl.multiple_of(c * LRU_CHUNK, LRU_CHUNK), LRU_CHUNK)
        o_ref[rows, :] = (b_s[rows, :] * _silu(g_ref[rows, :])).astype(BF16)
        return carry

    lax.fori_loop(0, SEQ // LRU_CHUNK, gate, 0)


def _lru(proj, conv_w, conv_b, w_gates, b_gates, lam):
    n_cb = LRU_WIDTH // LANES
    u_col0 = 4 * ATTN_WIDTH // LANES
    g_col0 = u_col0 + n_cb
    return pl.pallas_call(
        _lru_kernel,
        grid=(n_cb,),
        in_specs=[
            pl.BlockSpec((SEQ, LANES), lambda j: (0, u_col0 + j)),
            pl.BlockSpec((SEQ, LANES), lambda j: (0, g_col0 + j)),
            pl.BlockSpec((CONV_WIDTH, LANES), lambda j: (0, j)),
            pl.BlockSpec((1, LANES), lambda j: (0, j)),
            pl.BlockSpec((None, LANES, 2 * LANES), lambda j: (j, 0, 0)),
            pl.BlockSpec((None, 1, 2 * LANES), lambda j: (j, 0, 0)),
            pl.BlockSpec((1, LANES), lambda j: (0, j)),
        ],
        out_specs=pl.BlockSpec((SEQ, LANES), lambda j: (0, j)),
        out_shape=jax.ShapeDtypeStruct((SEQ, LRU_WIDTH), BF16),
        scratch_shapes=[
            pltpu.VMEM((LRU_CHUNK + SUBLANES, LANES), F32),
            pltpu.VMEM((SEQ, LANES), F32),
            pltpu.VMEM((SEQ, LANES), F32),
        ],
        compiler_params=pltpu.CompilerParams(
            dimension_semantics=("arbitrary",), vmem_limit_bytes=VMEM_LIMIT_BYTES),
        name="rglru",
    )(proj, proj, conv_w, conv_b, w_gates, b_gates, lam)


def _block_diag_gates(w_rgate, b_rgate, w_igate, b_igate):
    per = LANES // LRU_BLOCK_W
    n_cb = LRU_WIDTH // LANES

    def bd(w):
        w = w.reshape(n_cb, per, LRU_BLOCK_W, LRU_BLOCK_W)
        eye = jnp.eye(per, dtype=w.dtype)
        return jnp.einsum('cpkj,pq->cpkqj', w, eye).reshape(n_cb, LANES, LANES)

    w = jnp.concatenate([bd(w_rgate), bd(w_igate)], axis=-1).astype(BF16)
    b = jnp.concatenate([b_rgate.reshape(n_cb, 1, LANES), b_igate.reshape(n_cb, 1, LANES)], axis=-1)
    return w, b


def _out_kernel(ma_ref, ml_ref, wa_ref, wl_ref, x_ref, gate_ref, fg_ref, o_ref):
    mix = jnp.dot(ma_ref[...], wa_ref[...], preferred_element_type=F32)
    mix = mix + jnp.dot(ml_ref[...], wl_ref[...], preferred_element_type=F32)
    y = x_ref[...] + gate_ref[...] * mix
    var = jnp.mean(y * y, axis=-1, keepdims=True)
    o_ref[...] = y * lax.rsqrt(var + EPS) * fg_ref[...]


def _out_proj(mixed_attn, mixed_lru, w_out_bf16, x2d, mod, final_gain):
    return pl.pallas_call(
        _out_kernel,
        grid=(SEQ // OUT_TM,),
        in_specs=[
            pl.BlockSpec((OUT_TM, ATTN_WIDTH), lambda i: (i, 0)),
            pl.BlockSpec((OUT_TM, LRU_WIDTH), lambda i: (i, 0)),
            pl.BlockSpec((ATTN_WIDTH, D_MODEL), lambda i: (0, 0)),
            pl.BlockSpec((LRU_WIDTH, D_MODEL), lambda i: (1, 0)),
            pl.BlockSpec((OUT_TM, D_MODEL), lambda i: (i, 0)),
            pl.BlockSpec((1, D_MODEL), lambda i: (0, MOD_GATE)),
            pl.BlockSpec((1, D_MODEL), lambda i: (0, 0)),
        ],
        out_specs=pl.BlockSpec((OUT_TM, D_MODEL), lambda i: (i, 0)),
        out_shape=jax.ShapeDtypeStruct((SEQ, D_MODEL), F32),
        compiler_params=pltpu.CompilerParams(
            dimension_semantics=("arbitrary",), vmem_limit_bytes=VMEM_LIMIT_BYTES),
        name="out_proj",
    )(mixed_attn, mixed_lru, w_out_bf16, w_out_bf16, x2d, mod, final_gain)


@jax.jit
def kernel(x, c, norm_gain, w_ada, b_ada, w_in, conv_w, conv_b, w_rgate, b_rgate,
           w_igate, b_igate, lru_lambda, w_out, final_gain):
    assert x.shape == (1, SEQ, D_MODEL) and norm_gain.shape[0] == 1
    x2d = x.reshape(SEQ, D_MODEL)
    mod = _ada_mod(c.reshape(D_MODEL, 1), w_ada[0], b_ada)
    proj = _in_proj(x2d, norm_gain, mod, w_in[0].astype(BF16))
    mixed_attn = _attention(proj, _attn_bias())
    w_gates, b_gates = _block_diag_gates(w_rgate[0], b_rgate[0], w_igate[0], b_igate[0])
    mixed_lru = _lru(proj, conv_w[0], conv_b, w_gates, b_gates, lru_lambda)
    y = _out_proj(mixed_attn, mixed_lru, w_out[0].astype(BF16), x2d, mod,
                  final_gain.reshape(1, D_MODEL))
    return y.reshape(1, SEQ, D_MODEL)
```

```python
import functools
import math

import jax
import jax.numpy as jnp
from jax import lax
from jax.experimental import pallas as pl
from jax.experimental.pallas import tpu as pltpu

D_MODEL = 2048
SEQ = 8192
ATTN_WIDTH = D_MODEL // 2
LRU_WIDTH = D_MODEL - ATTN_WIDTH
HEAD_DIM = 64
N_HEADS = ATTN_WIDTH // HEAD_DIM
LRU_BLOCKS = 16
LRU_BLOCK_W = LRU_WIDTH // LRU_BLOCKS
CONV_WIDTH = 4
LRU_C = 8.0
DILATED_PATTERNS = ((128, 1), (512, 4), (2048, 16))
ATTN_SCALE = 1.0 / math.sqrt(HEAD_DIM)
NEG_INF = -1e30
EPS = 1e-6
PROJ_WIDTH = 4 * ATTN_WIDTH + 2 * LRU_WIDTH
MOD_SHIFT, MOD_SCALE, MOD_GATE = 0, 1, 2

LANES = 128
SUBLANES = 8
VMEM_LIMIT_BYTES = 56 * 1024 * 1024

ADA_TN = 512
PROJ_TM = 1024
PROJ_TN = 1024
NORM_ROWS = 64
NORM_CHUNK = 256
ATTN_QB = 128
ATTN_SB = 2048
ATTN_UNROLL = 4
ATTN_GROUP = 4
DEINT_ROWS = 256
ATTN_VMEM_LIMIT_BYTES = 60 * 1024 * 1024
HEADS_PER_BLOCK = LANES // HEAD_DIM
COMBINE_ROWS = 256
LRU_CHUNK = 256
OUT_TM = 512

F32 = jnp.float32
BF16 = jnp.bfloat16


def _sigmoid(x):
    return 0.5 * jnp.tanh(0.5 * x) + 0.5


def _silu(x):
    return x * _sigmoid(x)


def _ada_kernel(c_ref, w_ref, b_ref, o_ref):
    c_act = _silu(c_ref[...])
    acc = jnp.sum(w_ref[...] * c_act, axis=0, keepdims=True)
    o_ref[...] = acc + b_ref[...]


def _ada_mod(c_col, w_ada, b_ada):
    n = w_ada.shape[1]
    return pl.pallas_call(
        _ada_kernel,
        grid=(n // ADA_TN,),
        in_specs=[
            pl.BlockSpec((D_MODEL, 1), lambda j: (0, 0)),
            pl.BlockSpec((D_MODEL, ADA_TN), lambda j: (0, j)),
            pl.BlockSpec((1, ADA_TN), lambda j: (0, j)),
        ],
        out_specs=pl.BlockSpec((1, ADA_TN), lambda j: (0, j)),
        out_shape=jax.ShapeDtypeStruct((1, n), F32),
        compiler_params=pltpu.CompilerParams(
            dimension_semantics=("arbitrary",), vmem_limit_bytes=VMEM_LIMIT_BYTES),
        name="ada_mod",
    )(c_col, w_ada, b_ada)


def _proj_kernel(x_ref, gain_ref, shift_ref, scale_ref, w_ref, o_ref, h_ref):
    first_col = pl.program_id(1) == 0

    @pl.when(first_col)
    def _():
        mul = gain_ref[...] * (1.0 + scale_ref[...])
        shift = shift_ref[...]
        for c in range(PROJ_TM // NORM_CHUNK):
            for r0 in range(c * NORM_CHUNK, (c + 1) * NORM_CHUNK, NORM_ROWS):
                xv = x_ref[pl.ds(r0, NORM_ROWS), :]
                var = jnp.mean(xv * xv, axis=-1, keepdims=True)
                h = xv * lax.rsqrt(var + EPS) * mul + shift
                h_ref[pl.ds(r0, NORM_ROWS), :] = h.astype(BF16)
            rows = pl.ds(c * NORM_CHUNK, NORM_CHUNK)
            o_ref[rows, :] = jnp.dot(h_ref[rows, :], w_ref[...], preferred_element_type=F32)

    @pl.when(jnp.logical_not(first_col))
    def _():
        o_ref[...] = jnp.dot(h_ref[...], w_ref[...], preferred_element_type=F32)


def _in_proj(x2d, gain, mod, w_in_bf16):
    return pl.pallas_call(
        _proj_kernel,
        grid=(SEQ // PROJ_TM, PROJ_WIDTH // PROJ_TN),
        in_specs=[
            pl.BlockSpec((PROJ_TM, D_MODEL), lambda i, j: (i, 0)),
            pl.BlockSpec((1, D_MODEL), lambda i, j: (0, 0)),
            pl.BlockSpec((1, D_MODEL), lambda i, j: (0, MOD_SHIFT)),
            pl.BlockSpec((1, D_MODEL), lambda i, j: (0, MOD_SCALE)),
            pl.BlockSpec((D_MODEL, PROJ_TN), lambda i, j: (0, j)),
        ],
        out_specs=pl.BlockSpec((PROJ_TM, PROJ_TN), lambda i, j: (i, j)),
        out_shape=jax.ShapeDtypeStruct((SEQ, PROJ_WIDTH), F32),
        scratch_shapes=[pltpu.VMEM((PROJ_TM, D_MODEL), BF16)],
        compiler_params=pltpu.CompilerParams(
            dimension_semantics=("arbitrary", "arbitrary"),
            vmem_limit_bytes=VMEM_LIMIT_BYTES),
        name="in_proj",
    )(x2d, gain, mod, mod, w_in_bf16)


def _rows(start, size, stride):
    return pl.ds(start, size) if stride == 1 else pl.ds(start, size, stride=stride)


def _attn_kernel(q_ref, k_ref, v_ref, g_ref, bias_ref, o_ref,
                 kd, vd, s_s, mx_s, m_s, l_s, acc_s):
    sb = pl.program_id(1)
    lane = lax.broadcasted_iota(jnp.int32, (ATTN_QB, LANES), 1)
    is_h0 = lane < HEAD_DIM

    @pl.when(sb == 0)
    def _():
        for p, (_, d) in enumerate(DILATED_PATTERNS):
            per_residue = SEQ // d // DEINT_ROWS
            shift = per_residue.bit_length() - 1
            kd[p, pl.ds(0, ATTN_QB), :] = jnp.zeros((ATTN_QB, LANES), BF16)
            vd[p, pl.ds(0, ATTN_QB), :] = jnp.zeros((ATTN_QB, LANES), BF16)

            def deint(c, carry, p=p, d=d, per_residue=per_residue, shift=shift):
                r = lax.shift_right_logical(c, shift)
                j = jnp.bitwise_and(c, per_residue - 1)
                src = _rows(r + j * (DEINT_ROWS * d), DEINT_ROWS, d)
                dst = pl.ds(pl.multiple_of(ATTN_QB + c * DEINT_ROWS, ATTN_QB), DEINT_ROWS)
                kd[p, dst, :] = k_ref[src, :].astype(BF16)
                vd[p, dst, :] = v_ref[src, :].astype(BF16)
                return carry

            lax.fori_loop(0, SEQ // DEINT_ROWS, deint, 0, unroll=2)

    ones = jnp.ones((2 * ATTN_QB, LANES), BF16)

    for p, (_, d) in enumerate(DILATED_PATTERNS):
        log2d = d.bit_length() - 1
        blocks_per_step = ATTN_SB // (ATTN_QB * d)

        def place(ti, d=d, log2d=log2d, blocks_per_step=blocks_per_step):
            r = jnp.bitwise_and(ti, d - 1)
            n = lax.shift_right_logical(ti, log2d)
            blk = sb * blocks_per_step + n
            q_rows = _rows(r + n * (ATTN_QB * d), ATTN_QB, d)
            window = pl.ds(pl.multiple_of(r * (SEQ // d) + blk * ATTN_QB, ATTN_QB), 2 * ATTN_QB)
            return q_rows, window, blk == 0

        for g0 in range(0, ATTN_SB // ATTN_QB, ATTN_GROUP):

            def scores(i, carry, p=p, g0=g0, place=place):
                q_rows, window, first = place(g0 + i)
                q = q_ref[q_rows, :] * ATTN_SCALE
                zero = jnp.zeros_like(q)
                qs = jnp.concatenate(
                    [jnp.where(is_h0, q, zero), jnp.where(is_h0, zero, q)], axis=0).astype(BF16)
                s = lax.dot_general(qs, kd[p, window, :], (((1,), (1,)), ((), ())),
                                    preferred_element_type=F32)
                s = s + bias_ref[p, first.astype(jnp.int32)]
                s_s[i] = s
                mx_s[i] = jnp.broadcast_to(jnp.max(s, axis=1, keepdims=True), (2 * ATTN_QB, LANES))
                return carry

            lax.fori_loop(0, ATTN_GROUP, scores, 0, unroll=ATTN_UNROLL)

            def weighted(i, carry, p=p, g0=g0, place=place):
                q_rows, window, _ = place(g0 + i)
                mx = mx_s[i]
                e = jnp.concatenate(
                    [jnp.exp(s_s[i, :, :LANES] - mx), jnp.exp(s_s[i, :, LANES:] - mx)],
                    axis=1).astype(BF16)
                vc = jnp.concatenate([vd[p, window, :], ones], axis=1)
                pv = jnp.dot(e, vc, preferred_element_type=F32)
                acc_s[p, q_rows, :] = jnp.where(is_h0, pv[:ATTN_QB, :LANES], pv[ATTN_QB:, :LANES])
                l_s[p, q_rows, :] = jnp.where(is_h0, pv[:ATTN_QB, LANES:], pv[ATTN_QB:, LANES:])
                m_s[p, q_rows, :] = jnp.where(is_h0, mx[:ATTN_QB], mx[ATTN_QB:])
                return carry

            lax.fori_loop(0, ATTN_GROUP, weighted, 0, unroll=ATTN_UNROLL)

    def combine(i, carry):
        rows = pl.ds(pl.multiple_of(i * COMBINE_ROWS, COMBINE_ROWS), COMBINE_ROWS)
        m0, m1, m2 = m_s[0, rows, :], m_s[1, rows, :], m_s[2, rows, :]
        mm = jnp.maximum(jnp.maximum(m0, m1), m2)
        e0, e1, e2 = jnp.exp(m0 - mm), jnp.exp(m1 - mm), jnp.exp(m2 - mm)
        num = e0 * acc_s[0, rows, :] + e1 * acc_s[1, rows, :] + e2 * acc_s[2, rows, :]
        den = e0 * l_s[0, rows, :] + e1 * l_s[1, rows, :] + e2 * l_s[2, rows, :]
        o_ref[rows, :] = (num / den * _silu(g_ref[rows, :])).astype(BF16)
        return carry

    lax.fori_loop(0, ATTN_SB // COMBINE_ROWS, combine, 0)


def _attn_bias():
    slopes = 2.0 ** (-8.0 * jnp.arange(1, N_HEADS + 1, dtype=F32) / N_HEADS)
    a = jnp.arange(ATTN_QB, dtype=jnp.int32)[:, None]
    b = jnp.arange(2 * ATTN_QB, dtype=jnp.int32)[None, :]
    dist = a + ATTN_QB - b
    valid = (dist >= 0) & (dist <= ATTN_QB)
    per_pattern = []
    for _, d in DILATED_PATTERNS:
        offs = (dist * d).astype(F32)
        pen = -(slopes[:, None, None] * offs[None])
        normal = jnp.where(valid[None], pen, NEG_INF)
        no_prev = jnp.where((b >= ATTN_QB)[None], normal, NEG_INF)
        per_pattern.append(jnp.stack([normal, no_prev], axis=1))
    t = jnp.stack(per_pattern, axis=1)
    n_p = len(DILATED_PATTERNS)
    t = t.reshape(N_HEADS // HEADS_PER_BLOCK, HEADS_PER_BLOCK, n_p, 2, ATTN_QB, 2 * ATTN_QB)
    t = t.transpose(0, 2, 3, 1, 4, 5)
    return t.reshape(N_HEADS // HEADS_PER_BLOCK, n_p, 2, HEADS_PER_BLOCK * ATTN_QB, 2 * ATTN_QB)


def _attention(proj, bias):
    n_hb = ATTN_WIDTH // LANES
    n_p = len(DILATED_PATTERNS)
    return pl.pallas_call(
        _attn_kernel,
        grid=(n_hb, SEQ // ATTN_SB),
        in_specs=[
            pl.BlockSpec((ATTN_SB, LANES), lambda h, s: (s, h)),
            pl.BlockSpec((SEQ, LANES), lambda h, s: (0, n_hb + h)),
            pl.BlockSpec((SEQ, LANES), lambda h, s: (0, 2 * n_hb + h)),
            pl.BlockSpec((ATTN_SB, LANES), lambda h, s: (s, 3 * n_hb + h)),
            pl.BlockSpec((None, n_p, 2, HEADS_PER_BLOCK * ATTN_QB, 2 * ATTN_QB),
                         lambda h, s: (h, 0, 0, 0, 0)),
        ],
        out_specs=pl.BlockSpec((ATTN_SB, LANES), lambda h, s: (s, h)),
        out_shape=jax.ShapeDtypeStruct((SEQ, ATTN_WIDTH), BF16),
        scratch_shapes=[
            pltpu.VMEM((n_p, ATTN_QB + SEQ, LANES), BF16),
            pltpu.VMEM((n_p, ATTN_QB + SEQ, LANES), BF16),
            pltpu.VMEM((ATTN_GROUP, 2 * ATTN_QB, 2 * ATTN_QB), F32),
            pltpu.VMEM((ATTN_GROUP, 2 * ATTN_QB, LANES), F32),
        ] + [pltpu.VMEM((n_p, ATTN_SB, LANES), F32)] * 3,
        compiler_params=pltpu.CompilerParams(
            dimension_semantics=("arbitrary", "arbitrary"),
            vmem_limit_bytes=ATTN_VMEM_LIMIT_BYTES),
        name="dilated_attn",
    )(proj, proj, proj, proj, bias)


def _lru_kernel(u_ref, g_ref, cw_ref, cb_ref, wg_ref, bg_ref, lam_ref, o_ref,
                ubuf, a_s, b_s):
    lam = lam_ref[...]
    log_sig = jnp.minimum(lam, 0.0) - jnp.log1p(jnp.exp(-jnp.abs(lam)))
    sub = lax.broadcasted_iota(jnp.int32, (LRU_CHUNK // SUBLANES, SUBLANES, LANES), 1)

    ubuf[pl.ds(0, SUBLANES), :] = jnp.zeros((SUBLANES, LANES), F32)

    def chunk(c, carry):
        rows = pl.ds(pl.multiple_of(c * LRU_CHUNK, LRU_CHUNK), LRU_CHUNK)
        ubuf[pl.ds(SUBLANES, LRU_CHUNK), :] = u_ref[rows, :]
        xc = cb_ref[...]
        for j in range(CONV_WIDTH):
            off = SUBLANES - (CONV_WIDTH - 1) + j
            xc = xc + ubuf[pl.ds(off, LRU_CHUNK), :] * cw_ref[pl.ds(j, 1), :]
        ubuf[pl.ds(0, SUBLANES), :] = ubuf[pl.ds(LRU_CHUNK, SUBLANES), :]

        z = jnp.dot(xc.astype(BF16), wg_ref[...], preferred_element_type=F32) + bg_ref[...]
        r = _sigmoid(z[:, :LANES])
        i = _sigmoid(z[:, LANES:])
        log_a = LRU_C * r * log_sig
        a = jnp.exp(log_a)
        th = jnp.tanh(log_a)
        mult = jnp.sqrt(-2.0 * th / (1.0 - th))
        b = mult * (i * xc)

        a3 = a.reshape(LRU_CHUNK // SUBLANES, SUBLANES, LANES)
        b3 = b.reshape(LRU_CHUNK // SUBLANES, SUBLANES, LANES)
        for k in (1, 2, 4):
            keep = sub >= k
            a_prev = jnp.where(keep, pltpu.roll(a3, k, 1), 1.0)
            b_prev = jnp.where(keep, pltpu.roll(b3, k, 1), 0.0)
            b3 = a3 * b_prev + b3
            a3 = a3 * a_prev
        a_s[rows, :] = a3.reshape(LRU_CHUNK, LANES)
        b_s[rows, :] = b3.reshape(LRU_CHUNK, LANES)
        return carry

    lax.fori_loop(0, SEQ // LRU_CHUNK, chunk, 0)

    def group(gi, c):
        r0 = pl.multiple_of(gi * SUBLANES, SUBLANES)
        rows = pl.ds(r0, SUBLANES)
        ag = a_s[rows, :]
        bg = b_s[rows, :]
        a_last = jnp.broadcast_to(a_s[pl.ds(r0 + SUBLANES - 1, 1), :], (SUBLANES, LANES))
        b_last = jnp.broadcast_to(b_s[pl.ds(r0 + SUBLANES - 1, 1), :], (SUBLANES, LANES))
        b_s[rows, :] = bg + ag * c
        return a_last * c + b_last

    lax.fori_loop(0, SEQ // SUBLANES, group, jnp.zeros((SUBLANES, LANES), F32), unroll=8)

    def gate(c, carry):
        rows = pl.ds(pl.multiple_of(c * LRU_CHUNK, LRU_CHUNK), LRU_CHUNK)
        o_ref[rows, :] = (b_s[rows, :] * _silu(g_ref[rows, :])).astype(BF16)
        return carry

    lax.fori_loop(0, SEQ // LRU_CHUNK, gate, 0)


def _lru(proj, conv_w, conv_b, w_gates, b_gates, lam):
    n_cb = LRU_WIDTH // LANES
    u_col0 = 4 * ATTN_WIDTH // LANES
    g_col0 = u_col0 + n_cb
    return pl.pallas_call(
        _lru_kernel,
        grid=(n_cb,),
        in_specs=[
            pl.BlockSpec((SEQ, LANES), lambda j: (0, u_col0 + j)),
            pl.BlockSpec((SEQ, LANES), lambda j: (0, g_col0 + j)),
            pl.BlockSpec((CONV_WIDTH, LANES), lambda j: (0, j)),
            pl.BlockSpec((1, LANES), lambda j: (0, j)),
            pl.BlockSpec((None, LANES, 2 * LANES), lambda j: (j, 0, 0)),
            pl.BlockSpec((None, 1, 2 * LANES), lambda j: (j, 0, 0)),
            pl.BlockSpec((1, LANES), lambda j: (0, j)),
        ],
        out_specs=pl.BlockSpec((SEQ, LANES), lambda j: (0, j)),
        out_shape=jax.ShapeDtypeStruct((SEQ, LRU_WIDTH), BF16),
        scratch_shapes=[
            pltpu.VMEM((LRU_CHUNK + SUBLANES, LANES), F32),
            pltpu.VMEM((SEQ, LANES), F32),
            pltpu.VMEM((SEQ, LANES), F32),
        ],
        compiler_params=pltpu.CompilerParams(
            dimension_semantics=("arbitrary",), vmem_limit_bytes=VMEM_LIMIT_BYTES),
        name="rglru",
    )(proj, proj, conv_w, conv_b, w_gates, b_gates, lam)


def _block_diag_gates(w_rgate, b_rgate, w_igate, b_igate):
    per = LANES // LRU_BLOCK_W
    n_cb = LRU_WIDTH // LANES

    def bd(w):
        w = w.reshape(n_cb, per, LRU_BLOCK_W, LRU_BLOCK_W)
        eye = jnp.eye(per, dtype=w.dtype)
        return jnp.einsum('cpkj,pq->cpkqj', w, eye).reshape(n_cb, LANES, LANES)

    w = jnp.concatenate([bd(w_rgate), bd(w_igate)], axis=-1).astype(BF16)
    b = jnp.concatenate([b_rgate.reshape(n_cb, 1, LANES), b_igate.reshape(n_cb, 1, LANES)], axis=-1)
    return w, b


def _out_kernel(ma_ref, ml_ref, wa_ref, wl_ref, x_ref, gate_ref, fg_ref, o_ref):
    mix = jnp.dot(ma_ref[...], wa_ref[...], preferred_element_type=F32)
    mix = mix + jnp.dot(ml_ref[...], wl_ref[...], preferred_element_type=F32)
    y = x_ref[...] + gate_ref[...] * mix
    var = jnp.mean(y * y, axis=-1, keepdims=True)
    o_ref[...] = y * lax.rsqrt(var + EPS) * fg_ref[...]


def _out_proj(mixed_attn, mixed_lru, w_out_bf16, x2d, mod, final_gain):
    return pl.pallas_call(
        _out_kernel,
        grid=(SEQ // OUT_TM,),
        in_specs=[
            pl.BlockSpec((OUT_TM, ATTN_WIDTH), lambda i: (i, 0)),
            pl.BlockSpec((OUT_TM, LRU_WIDTH), lambda i: (i, 0)),
            pl.BlockSpec((ATTN_WIDTH, D_MODEL), lambda i: (0, 0)),
            pl.BlockSpec((LRU_WIDTH, D_MODEL), lambda i: (1, 0)),
            pl.BlockSpec((OUT_TM, D_MODEL), lambda i: (i, 0)),
            pl.BlockSpec((1, D_MODEL), lambda i: (0, MOD_GATE)),
            pl.BlockSpec((1, D_MODEL), lambda i: (0, 0)),
        ],
        out_specs=pl.BlockSpec((OUT_TM, D_MODEL), lambda i: (i, 0)),
        out_shape=jax.ShapeDtypeStruct((SEQ, D_MODEL), F32),
        compiler_params=pltpu.CompilerParams(
            dimension_semantics=("arbitrary",), vmem_limit_bytes=VMEM_LIMIT_BYTES),
        name="out_proj",
    )(mixed_attn, mixed_lru, w_out_bf16, w_out_bf16, x2d, mod, final_gain)


@jax.jit
def kernel(x, c, norm_gain, w_ada, b_ada, w_in, conv_w, conv_b, w_rgate, b_rgate,
           w_igate, b_igate, lru_lambda, w_out, final_gain):
    assert x.shape == (1, SEQ, D_MODEL) and norm_gain.shape[0] == 1
    x2d = x.reshape(SEQ, D_MODEL)
    mod = _ada_mod(c.reshape(D_MODEL, 1), w_ada[0], b_ada)
    proj = _in_proj(x2d, norm_gain, mod, w_in[0].astype(BF16))
    mixed_attn = _attention(proj, _attn_bias())
    w_gates, b_gates = _block_diag_gates(w_rgate[0], b_rgate[0], w_igate[0], b_igate[0])
    mixed_lru = _lru(proj, conv_w[0], conv_b, w_gates, b_gates, lru_lambda)
    y = _out_proj(mixed_attn, mixed_lru, w_out[0].astype(BF16), x2d, mod,
                  final_gain.reshape(1, D_MODEL))
    return y.reshape(1, SEQ, D_MODEL)
```

```python
import functools
import math

import jax
import jax.numpy as jnp
from jax import lax
from jax.experimental import pallas as pl
from jax.experimental.pallas import tpu as pltpu

D_MODEL = 2048
SEQ = 8192
ATTN_WIDTH = D_MODEL // 2
LRU_WIDTH = D_MODEL - ATTN_WIDTH
HEAD_DIM = 64
N_HEADS = ATTN_WIDTH // HEAD_DIM
LRU_BLOCKS = 16
LRU_BLOCK_W = LRU_WIDTH // LRU_BLOCKS
CONV_WIDTH = 4
LRU_C = 8.0
DILATED_PATTERNS = ((128, 1), (512, 4), (2048, 16))
ATTN_SCALE = 1.0 / math.sqrt(HEAD_DIM)
NEG_INF = -1e30
LOG2E = math.log2(math.e)
EPS = 1e-6
PROJ_WIDTH = 4 * ATTN_WIDTH + 2 * LRU_WIDTH
MOD_SHIFT, MOD_SCALE, MOD_GATE = 0, 1, 2

LANES = 128
SUBLANES = 8
VMEM_LIMIT_BYTES = 56 * 1024 * 1024

ADA_TN = 512
PROJ_TM = 1024
PROJ_TN = 1024
NORM_ROWS = 64
NORM_CHUNK = 256
ATTN_QB = 128
ATTN_SB = 2048
ATTN_UNROLL = 4
ATTN_GROUP = 4
DEINT_ROWS = 256
DEINT_RATIO = 4
ATTN_VMEM_LIMIT_BYTES = 60 * 1024 * 1024
HEADS_PER_BLOCK = LANES // HEAD_DIM
COMBINE_ROWS = 256
LRU_CHUNK = 256
LRU_SEGMENTS = SUBLANES
LRU_SEG_LEN = SEQ // LRU_SEGMENTS
LRU_SEG_PITCH = LRU_SEG_LEN + SUBLANES
OUT_TM = 512

F32 = jnp.float32
BF16 = jnp.bfloat16


def _sigmoid(x):
    return 0.5 * jnp.tanh(0.5 * x) + 0.5


def _silu(x):
    return x * _sigmoid(x)


def _ada_kernel(c_ref, w_ref, b_ref, o_ref):
    c_act = _silu(c_ref[...])
    acc = jnp.sum(w_ref[...] * c_act, axis=0, keepdims=True)
    o_ref[...] = acc + b_ref[...]


def _ada_mod(c_col, w_ada, b_ada):
    n = w_ada.shape[1]
    return pl.pallas_call(
        _ada_kernel,
        grid=(n // ADA_TN,),
        in_specs=[
            pl.BlockSpec((D_MODEL, 1), lambda j: (0, 0)),
            pl.BlockSpec((D_MODEL, ADA_TN), lambda j: (0, j)),
            pl.BlockSpec((1, ADA_TN), lambda j: (0, j)),
        ],
        out_specs=pl.BlockSpec((1, ADA_TN), lambda j: (0, j)),
        out_shape=jax.ShapeDtypeStruct((1, n), F32),
        compiler_params=pltpu.CompilerParams(
            dimension_semantics=("arbitrary",), vmem_limit_bytes=VMEM_LIMIT_BYTES),
        name="ada_mod",
    )(c_col, w_ada, b_ada)


def _proj_kernel(x_ref, gain_ref, shift_ref, scale_ref, w_ref, o_ref, h_ref):
    first_col = pl.program_id(1) == 0

    @pl.when(first_col)
    def _():
        mul = gain_ref[...] * (1.0 + scale_ref[...])
        shift = shift_ref[...]
        for c in range(PROJ_TM // NORM_CHUNK):
            for r0 in range(c * NORM_CHUNK, (c + 1) * NORM_CHUNK, NORM_ROWS):
                xv = x_ref[pl.ds(r0, NORM_ROWS), :]
                var = jnp.mean(xv * xv, axis=-1, keepdims=True)
                h = xv * lax.rsqrt(var + EPS) * mul + shift
                h_ref[pl.ds(r0, NORM_ROWS), :] = h.astype(BF16)
            rows = pl.ds(c * NORM_CHUNK, NORM_CHUNK)
            o_ref[rows, :] = jnp.dot(h_ref[rows, :], w_ref[...], preferred_element_type=F32)

    @pl.when(jnp.logical_not(first_col))
    def _():
        o_ref[...] = jnp.dot(h_ref[...], w_ref[...], preferred_element_type=F32)


def _in_proj(x2d, gain, mod, w_in_bf16):
    return pl.pallas_call(
        _proj_kernel,
        grid=(SEQ // PROJ_TM, PROJ_WIDTH // PROJ_TN),
        in_specs=[
            pl.BlockSpec((PROJ_TM, D_MODEL), lambda i, j: (i, 0)),
            pl.BlockSpec((1, D_MODEL), lambda i, j: (0, 0)),
            pl.BlockSpec((1, D_MODEL), lambda i, j: (0, MOD_SHIFT)),
            pl.BlockSpec((1, D_MODEL), lambda i, j: (0, MOD_SCALE)),
            pl.BlockSpec((D_MODEL, PROJ_TN), lambda i, j: (0, j)),
        ],
        out_specs=pl.BlockSpec((PROJ_TM, PROJ_TN), lambda i, j: (i, j)),
        out_shape=jax.ShapeDtypeStruct((SEQ, PROJ_WIDTH), F32),
        scratch_shapes=[pltpu.VMEM((PROJ_TM, D_MODEL), BF16)],
        compiler_params=pltpu.CompilerParams(
            dimension_semantics=("arbitrary", "arbitrary"),
            vmem_limit_bytes=VMEM_LIMIT_BYTES),
        name="in_proj",
    )(x2d, gain, mod, mod, w_in_bf16)


def _rows(start, size, stride):
    return pl.ds(start, size) if stride == 1 else pl.ds(start, size, stride=stride)


def _attn_kernel(q_ref, k_ref, v_ref, g_ref, bias_ref, o_ref,
                 kd, vd, tmp, s_s, mx_s, m_s, l_s, acc_s):
    sb = pl.program_id(1)
    lane = lax.broadcasted_iota(jnp.int32, (ATTN_QB, LANES), 1)
    is_h0 = lane < HEAD_DIM

    @pl.when(sb == 0)
    def _():
        for src_ref, dst in ((k_ref, kd), (v_ref, vd)):
            for p, (_, d) in enumerate(DILATED_PATTERNS):
                dst[p, pl.ds(0, ATTN_QB), :] = jnp.zeros((ATTN_QB, LANES), BF16)
                d_prev = DILATED_PATTERNS[p - 1][1] if p else 1
                ratio = d // d_prev
                assert p == 0 and d == 1 or ratio == DEINT_RATIO
                per_residue = SEQ // d // DEINT_ROWS
                keep_f32 = 0 < p < len(DILATED_PATTERNS) - 1
                from_ref = src_ref if p <= 1 else tmp

                def deint(c, carry, p=p, d_prev=d_prev, ratio=ratio, per_residue=per_residue,
                          keep_f32=keep_f32, from_ref=from_ref, dst=dst):
                    r = lax.shift_right_logical(c, per_residue.bit_length() - 1)
                    chunk = jnp.bitwise_and(c, per_residue - 1)
                    r_prev = jnp.bitwise_and(r, d_prev - 1)
                    j = lax.shift_right_logical(r, d_prev.bit_length() - 1)
                    start = r_prev * (SEQ // d_prev) + chunk * (DEINT_ROWS * ratio) + j
                    x = from_ref[_rows(start, DEINT_ROWS, ratio), :]
                    if keep_f32:
                        tmp[pl.ds(pl.multiple_of(c * DEINT_ROWS, DEINT_ROWS), DEINT_ROWS), :] = x
                    rows = pl.ds(pl.multiple_of(ATTN_QB + c * DEINT_ROWS, ATTN_QB), DEINT_ROWS)
                    dst[p, rows, :] = x.astype(BF16)
                    return carry

                lax.fori_loop(0, SEQ // DEINT_ROWS, deint, 0, unroll=4)

    ones = jnp.ones((2 * ATTN_QB, LANES), BF16)

    for p, (_, d) in enumerate(DILATED_PATTERNS):
        log2d = d.bit_length() - 1
        blocks_per_step = ATTN_SB // (ATTN_QB * d)

        def place(ti, d=d, log2d=log2d, blocks_per_step=blocks_per_step):
            r = jnp.bitwise_and(ti, d - 1)
            n = lax.shift_right_logical(ti, log2d)
            blk = sb * blocks_per_step + n
            q_rows = _rows(r + n * (ATTN_QB * d), ATTN_QB, d)
            window = pl.ds(pl.multiple_of(r * (SEQ // d) + blk * ATTN_QB, ATTN_QB), 2 * ATTN_QB)
            return q_rows, window, blk == 0

        for g0 in range(0, ATTN_SB // ATTN_QB, ATTN_GROUP):

            def scores(i, carry, p=p, g0=g0, place=place):
                q_rows, window, first = place(g0 + i)
                q = (q_ref[q_rows, :] * (ATTN_SCALE * LOG2E)).astype(BF16)
                zero = jnp.zeros_like(q)
                qs = jnp.concatenate(
                    [jnp.where(is_h0, q, zero), jnp.where(is_h0, zero, q)], axis=0)
                s = lax.dot_general(qs, kd[p, window, :], (((1,), (1,)), ((), ())),
                                    preferred_element_type=F32)
                s = s + bias_ref[p, first.astype(jnp.int32)]
                s_s[i] = s
                mx_s[i] = jnp.broadcast_to(jnp.max(s, axis=1, keepdims=True), (2 * ATTN_QB, LANES))
                return carry

            lax.fori_loop(0, ATTN_GROUP, scores, 0, unroll=ATTN_UNROLL)

            def weighted(i, carry, p=p, g0=g0, place=place):
                q_rows, window, _ = place(g0 + i)
                mx = mx_s[i]
                e = jnp.concatenate(
                    [jnp.exp2(s_s[i, :, :LANES] - mx), jnp.exp2(s_s[i, :, LANES:] - mx)],
                    axis=1).astype(BF16)
                vc = jnp.concatenate([vd[p, window, :], ones], axis=1)
                pv = jnp.dot(e, vc, preferred_element_type=F32)
                acc_s[p, q_rows, :] = jnp.where(is_h0, pv[:ATTN_QB, :LANES], pv[ATTN_QB:, :LANES])
                l_s[p, q_rows, :] = jnp.where(is_h0, pv[:ATTN_QB, LANES:], pv[ATTN_QB:, LANES:])
                m_s[p, q_rows, :] = jnp.where(is_h0, mx[:ATTN_QB], mx[ATTN_QB:])
                return carry

            lax.fori_loop(0, ATTN_GROUP, weighted, 0, unroll=ATTN_UNROLL)

    def combine(i, carry):
        rows = pl.ds(pl.multiple_of(i * COMBINE_ROWS, COMBINE_ROWS), COMBINE_ROWS)
        m0, m1, m2 = m_s[0, rows, :], m_s[1, rows, :], m_s[2, rows, :]
        mm = jnp.maximum(jnp.maximum(m0, m1), m2)
        e0, e1, e2 = jnp.exp2(m0 - mm), jnp.exp2(m1 - mm), jnp.exp2(m2 - mm)
        num = e0 * acc_s[0, rows, :] + e1 * acc_s[1, rows, :] + e2 * acc_s[2, rows, :]
        den = e0 * l_s[0, rows, :] + e1 * l_s[1, rows, :] + e2 * l_s[2, rows, :]
        o_ref[rows, :] = (num / den * _silu(g_ref[rows, :])).astype(BF16)
        return carry

    lax.fori_loop(0, ATTN_SB // COMBINE_ROWS, combine, 0)


def _attn_bias():
    slopes = 2.0 ** (-8.0 * jnp.arange(1, N_HEADS + 1, dtype=F32) / N_HEADS)
    a = jnp.arange(ATTN_QB, dtype=jnp.int32)[:, None]
    b = jnp.arange(2 * ATTN_QB, dtype=jnp.int32)[None, :]
    dist = a + ATTN_QB - b
    valid = (dist >= 0) & (dist <= ATTN_QB)
    per_pattern = []
    for _, d in DILATED_PATTERNS:
        offs = (dist * d).astype(F32)
        pen = -(slopes[:, None, None] * offs[None]) * LOG2E
        normal = jnp.where(valid[None], pen, NEG_INF)
        no_prev = jnp.where((b >= ATTN_QB)[None], normal, NEG_INF)
        per_pattern.append(jnp.stack([normal, no_prev], axis=1))
    t = jnp.stack(per_pattern, axis=1)
    n_p = len(DILATED_PATTERNS)
    t = t.reshape(N_HEADS // HEADS_PER_BLOCK, HEADS_PER_BLOCK, n_p, 2, ATTN_QB, 2 * ATTN_QB)
    t = t.transpose(0, 2, 3, 1, 4, 5)
    return t.reshape(N_HEADS // HEADS_PER_BLOCK, n_p, 2, HEADS_PER_BLOCK * ATTN_QB, 2 * ATTN_QB)


def _attention(proj, bias):
    n_hb = ATTN_WIDTH // LANES
    n_p = len(DILATED_PATTERNS)
    return pl.pallas_call(
        _attn_kernel,
        grid=(n_hb, SEQ // ATTN_SB),
        in_specs=[
            pl.BlockSpec((ATTN_SB, LANES), lambda h, s: (s, h)),
            pl.BlockSpec((SEQ, LANES), lambda h, s: (0, n_hb + h)),
            pl.BlockSpec((SEQ, LANES), lambda h, s: (0, 2 * n_hb + h)),
            pl.BlockSpec((ATTN_SB, LANES), lambda h, s: (s, 3 * n_hb + h)),
            pl.BlockSpec((None, n_p, 2, HEADS_PER_BLOCK * ATTN_QB, 2 * ATTN_QB),
                         lambda h, s: (h, 0, 0, 0, 0)),
        ],
        out_specs=pl.BlockSpec((ATTN_SB, LANES), lambda h, s: (s, h)),
        out_shape=jax.ShapeDtypeStruct((SEQ, ATTN_WIDTH), BF16),
        scratch_shapes=[
            pltpu.VMEM((n_p, ATTN_QB + SEQ, LANES), BF16),
            pltpu.VMEM((n_p, ATTN_QB + SEQ, LANES), BF16),
            pltpu.VMEM((SEQ, LANES), F32),
            pltpu.VMEM((ATTN_GROUP, 2 * ATTN_QB, 2 * ATTN_QB), F32),
            pltpu.VMEM((ATTN_GROUP, 2 * ATTN_QB, LANES), F32),
        ] + [pltpu.VMEM((n_p, ATTN_SB, LANES), F32)] * 3,
        compiler_params=pltpu.CompilerParams(
            dimension_semantics=("arbitrary", "arbitrary"),
            vmem_limit_bytes=ATTN_VMEM_LIMIT_BYTES),
        name="dilated_attn",
    )(proj, proj, proj, proj, bias)


def _lru_kernel(u_ref, g_ref, cw_ref, cb_ref, wg_ref, bg_ref, lam_ref, o_ref,
                ubuf, a_s, b_s, h_s, p_s):
    lam = lam_ref[...]
    log_sig = jnp.minimum(lam, 0.0) - jnp.log1p(jnp.exp(-jnp.abs(lam)))
    chunks_per_seg = LRU_SEG_LEN // LRU_CHUNK

    ubuf[pl.ds(0, SUBLANES), :] = jnp.zeros((SUBLANES, LANES), F32)

    def chunk(c, carry):
        rows = pl.ds(pl.multiple_of(c * LRU_CHUNK, LRU_CHUNK), LRU_CHUNK)
        ubuf[pl.ds(SUBLANES, LRU_CHUNK), :] = u_ref[rows, :]
        xc = cb_ref[...]
        for j in range(CONV_WIDTH):
            off = SUBLANES - (CONV_WIDTH - 1) + j
            xc = xc + ubuf[pl.ds(off, LRU_CHUNK), :] * cw_ref[pl.ds(j, 1), :]
        ubuf[pl.ds(0, SUBLANES), :] = ubuf[pl.ds(LRU_CHUNK, SUBLANES), :]

        z = jnp.dot(xc.astype(BF16), wg_ref[...], preferred_element_type=F32) + bg_ref[...]
        r = _sigmoid(z[:, :LANES])
        i = _sigmoid(z[:, LANES:])
        log_a = LRU_C * r * log_sig
        a = jnp.exp(log_a)
        th = jnp.tanh(log_a)
        x2 = -2.0 * th / (1.0 - th)
        mult = jnp.where(x2 > 0.0, x2 * lax.rsqrt(x2), 0.0)
        b = mult * (i * xc)

        seg = lax.shift_right_logical(c, chunks_per_seg.bit_length() - 1)
        within = jnp.bitwise_and(c, chunks_per_seg - 1)
        dst = pl.ds(pl.multiple_of(seg * LRU_SEG_PITCH + within * LRU_CHUNK, SUBLANES), LRU_CHUNK)
        a_s[dst, :] = a
        b_s[dst, :] = b
        return carry

    lax.fori_loop(0, SEQ // LRU_CHUNK, chunk, 0, unroll=2)

    def step(t, carry):
        h, prod = carry
        rows = pl.ds(t, LRU_SEGMENTS, stride=LRU_SEG_PITCH)
        a = a_s[rows, :]
        h = a * h + b_s[rows, :]
        prod = prod * a
        h_s[rows, :] = h
        p_s[rows, :] = prod
        return h, prod

    h_end, p_end = lax.fori_loop(
        0, LRU_SEG_LEN, step,
        (jnp.zeros((LRU_SEGMENTS, LANES), F32), jnp.ones((LRU_SEGMENTS, LANES), F32)), unroll=8)

    seg_id = lax.broadcasted_iota(jnp.int32, (LRU_SEGMENTS, LANES), 0)
    c_in = jnp.zeros((LRU_SEGMENTS, LANES), F32)
    for _ in range(LRU_SEGMENTS - 1):
        c_in = jnp.where(seg_id == 0, 0.0, pltpu.roll(h_end + p_end * c_in, 1, 0))

    for s in range(LRU_SEGMENTS):
        c_s = c_in[s:s + 1, :]

        def gate(k, carry, s=s, c_s=c_s):
            off = pl.multiple_of(k * LRU_CHUNK, LRU_CHUNK)
            src = pl.ds(s * LRU_SEG_PITCH + off, LRU_CHUNK)
            rows = pl.ds(s * LRU_SEG_LEN + off, LRU_CHUNK)
            h = h_s[src, :] + p_s[src, :] * c_s
            o_ref[rows, :] = (h * _silu(g_ref[rows, :])).astype(BF16)
            return carry

        lax.fori_loop(0, chunks_per_seg, gate, 0)


def _lru(proj, conv_w, conv_b, w_gates, b_gates, lam):
    n_cb = LRU_WIDTH // LANES
    u_col0 = 4 * ATTN_WIDTH // LANES
    g_col0 = u_col0 + n_cb
    return pl.pallas_call(
        _lru_kernel,
        grid=(n_cb,),
        in_specs=[
            pl.BlockSpec((SEQ, LANES), lambda j: (0, u_col0 + j)),
            pl.BlockSpec((SEQ, LANES), lambda j: (0, g_col0 + j)),
            pl.BlockSpec((CONV_WIDTH, LANES), lambda j: (0, j)),
            pl.BlockSpec((1, LANES), lambda j: (0, j)),
            pl.BlockSpec((None, LANES, 2 * LANES), lambda j: (j, 0, 0)),
            pl.BlockSpec((None, 1, 2 * LANES), lambda j: (j, 0, 0)),
            pl.BlockSpec((1, LANES), lambda j: (0, j)),
        ],
        out_specs=pl.BlockSpec((SEQ, LANES), lambda j: (0, j)),
        out_shape=jax.ShapeDtypeStruct((SEQ, LRU_WIDTH), BF16),
        scratch_shapes=[
            pltpu.VMEM((LRU_CHUNK + SUBLANES, LANES), F32),
        ] + [pltpu.VMEM((LRU_SEGMENTS * LRU_SEG_PITCH, LANES), F32)] * 4,
        compiler_params=pltpu.CompilerParams(
            dimension_semantics=("arbitrary",), vmem_limit_bytes=VMEM_LIMIT_BYTES),
        name="rglru",
    )(proj, proj, conv_w, conv_b, w_gates, b_gates, lam)


def _block_diag_gates(w_rgate, b_rgate, w_igate, b_igate):
    per = LANES // LRU_BLOCK_W
    n_cb = LRU_WIDTH // LANES

    def bd(w):
        w = w.reshape(n_cb, per, LRU_BLOCK_W, LRU_BLOCK_W)
        eye = jnp.eye(per, dtype=w.dtype)
        return jnp.einsum('cpkj,pq->cpkqj', w, eye).reshape(n_cb, LANES, LANES)

    w = jnp.concatenate([bd(w_rgate), bd(w_igate)], axis=-1).astype(BF16)
    b = jnp.concatenate([b_rgate.reshape(n_cb, 1, LANES), b_igate.reshape(n_cb, 1, LANES)], axis=-1)
    return w, b


def _out_kernel(ma_ref, ml_ref, wa_ref, wl_ref, x_ref, gate_ref, fg_ref, o_ref):
    mix = jnp.dot(ma_ref[...], wa_ref[...], preferred_element_type=F32)
    mix = mix + jnp.dot(ml_ref[...], wl_ref[...], preferred_element_type=F32)
    y = x_ref[...] + gate_ref[...] * mix
    var = jnp.mean(y * y, axis=-1, keepdims=True)
    o_ref[...] = y * lax.rsqrt(var + EPS) * fg_ref[...]


def _out_proj(mixed_attn, mixed_lru, w_out_bf16, x2d, mod, final_gain):
    return pl.pallas_call(
        _out_kernel,
        grid=(SEQ // OUT_TM,),
        in_specs=[
            pl.BlockSpec((OUT_TM, ATTN_WIDTH), lambda i: (i, 0)),
            pl.BlockSpec((OUT_TM, LRU_WIDTH), lambda i: (i, 0)),
            pl.BlockSpec((ATTN_WIDTH, D_MODEL), lambda i: (0, 0)),
            pl.BlockSpec((LRU_WIDTH, D_MODEL), lambda i: (1, 0)),
            pl.BlockSpec((OUT_TM, D_MODEL), lambda i: (i, 0)),
            pl.BlockSpec((1, D_MODEL), lambda i: (0, MOD_GATE)),
            pl.BlockSpec((1, D_MODEL), lambda i: (0, 0)),
        ],
        out_specs=pl.BlockSpec((OUT_TM, D_MODEL), lambda i: (i, 0)),
        out_shape=jax.ShapeDtypeStruct((SEQ, D_MODEL), F32),
        compiler_params=pltpu.CompilerParams(
            dimension_semantics=("arbitrary",), vmem_limit_bytes=VMEM_LIMIT_BYTES),
        name="out_proj",
    )(mixed_attn, mixed_lru, w_out_bf16, w_out_bf16, x2d, mod, final_gain)


@jax.jit
def kernel(x, c, norm_gain, w_ada, b_ada, w_in, conv_w, conv_b, w_rgate, b_rgate,
           w_igate, b_igate, lru_lambda, w_out, final_gain):
    assert x.shape == (1, SEQ, D_MODEL) and norm_gain.shape[0] == 1
    x2d = x.reshape(SEQ, D_MODEL)
    mod = _ada_mod(c.reshape(D_MODEL, 1), w_ada[0], b_ada)
    proj = _in_proj(x2d, norm_gain, mod, w_in[0].astype(BF16))
    mixed_attn = _attention(proj, _attn_bias())
    w_gates, b_gates = _block_diag_gates(w_rgate[0], b_rgate[0], w_igate[0], b_igate[0])
    mixed_lru = _lru(proj, conv_w[0], conv_b, w_gates, b_gates, lru_lambda)
    y = _out_proj(mixed_attn, mixed_lru, w_out[0].astype(BF16), x2d, mod,
                  final_gain.reshape(1, D_MODEL))
    return y.reshape(1, SEQ, D_MODEL)
```

```python
import functools
import math

import jax
import jax.numpy as jnp
from jax import lax
from jax.experimental import pallas as pl
from jax.experimental.pallas import tpu as pltpu

D_MODEL = 2048
SEQ = 8192
ATTN_WIDTH = D_MODEL // 2
LRU_WIDTH = D_MODEL - ATTN_WIDTH
HEAD_DIM = 64
N_HEADS = ATTN_WIDTH // HEAD_DIM
LRU_BLOCKS = 16
LRU_BLOCK_W = LRU_WIDTH // LRU_BLOCKS
CONV_WIDTH = 4
LRU_C = 8.0
DILATED_PATTERNS = ((128, 1), (512, 4), (2048, 16))
ATTN_SCALE = 1.0 / math.sqrt(HEAD_DIM)
NEG_INF = -1e30
LOG2E = math.log2(math.e)
EPS = 1e-6
PROJ_WIDTH = 4 * ATTN_WIDTH + 2 * LRU_WIDTH
MOD_SHIFT, MOD_SCALE, MOD_GATE = 0, 1, 2

LANES = 128
SUBLANES = 8
VMEM_LIMIT_BYTES = 56 * 1024 * 1024

ADA_TK = 256
PROJ_TM = 256
PROJ_W_CHUNK = 512
PROJ_W_SLOTS = 2
PROJ_VMEM_LIMIT_BYTES = 60 * 1024 * 1024
NORM_ROWS = 64
ATTN_QB = 128
ATTN_SB = 2048
ATTN_UNROLL = 4
ATTN_GROUP = 4
DEINT_ROWS = 256
DEINT_RATIO = 4
ATTN_VMEM_LIMIT_BYTES = 60 * 1024 * 1024
HEADS_PER_BLOCK = LANES // HEAD_DIM
COMBINE_ROWS = 256
LRU_CHUNK = 256
LRU_SEGMENTS = SUBLANES
LRU_SEG_LEN = SEQ // LRU_SEGMENTS
LRU_SEG_PITCH = LRU_SEG_LEN + SUBLANES
OUT_TM = 512

F32 = jnp.float32
BF16 = jnp.bfloat16


def _sigmoid(x):
    return 0.5 * jnp.tanh(0.5 * x) + 0.5


def _silu(x):
    return x * _sigmoid(x)


def _ada_kernel(c_ref, w_ref, b_ref, o_ref):
    @pl.when(pl.program_id(0) == 0)
    def _():
        o_ref[...] = b_ref[...]

    c_act = _silu(c_ref[...])
    o_ref[...] += jnp.sum(w_ref[...] * c_act, axis=0, keepdims=True)


def _ada_mod(c_col, w_ada, b_ada):
    n = w_ada.shape[1]
    return pl.pallas_call(
        _ada_kernel,
        grid=(D_MODEL // ADA_TK,),
        in_specs=[
            pl.BlockSpec((ADA_TK, 1), lambda k: (k, 0)),
            pl.BlockSpec((ADA_TK, n), lambda k: (k, 0)),
            pl.BlockSpec((1, n), lambda k: (0, 0)),
        ],
        out_specs=pl.BlockSpec((1, n), lambda k: (0, 0)),
        out_shape=jax.ShapeDtypeStruct((1, n), F32),
        compiler_params=pltpu.CompilerParams(
            dimension_semantics=("arbitrary",), vmem_limit_bytes=VMEM_LIMIT_BYTES),
        name="ada_mod",
    )(c_col, w_ada, b_ada)


def _w_chunk_copy(w_hbm, stage, sem, c):
    slot = c % PROJ_W_SLOTS
    cols = pl.ds(c * PROJ_W_CHUNK, PROJ_W_CHUNK)
    return pltpu.make_async_copy(w_hbm.at[:, cols], stage.at[slot], sem.at[slot])


def _proj_kernel(x_ref, gain_ref, shift_ref, scale_ref, w_hbm, o_ref,
                 w_res, stage, h_even, h_odd, sem):
    s = pl.program_id(0)
    n_tiles = pl.num_programs(0) - 1
    n_chunks = PROJ_WIDTH // PROJ_W_CHUNK

    def normalise(h_ref):
        mul = gain_ref[...] * (1.0 + scale_ref[...])
        shift = shift_ref[...]
        for r0 in range(0, PROJ_TM, NORM_ROWS):
            xv = x_ref[pl.ds(r0, NORM_ROWS), :]
            var = jnp.mean(xv * xv, axis=-1, keepdims=True)
            h = xv * lax.rsqrt(var + EPS) * mul + shift
            h_ref[pl.ds(r0, NORM_ROWS), :] = h.astype(BF16)

    def project(h_ref):
        o_ref[...] = jnp.dot(h_ref[...], w_res[...], preferred_element_type=F32)

    @pl.when(s == 0)
    def _():
        for c in range(PROJ_W_SLOTS):
            _w_chunk_copy(w_hbm, stage, sem, c).start()
        normalise(h_even)
        for c in range(n_chunks):
            _w_chunk_copy(w_hbm, stage, sem, c).wait()
            cols = pl.ds(c * PROJ_W_CHUNK, PROJ_W_CHUNK)
            w_res[:, cols] = stage[c % PROJ_W_SLOTS].astype(BF16)
            if c + PROJ_W_SLOTS < n_chunks:
                _w_chunk_copy(w_hbm, stage, sem, c + PROJ_W_SLOTS).start()

    is_even = jnp.bitwise_and(s, 1) == 0

    @pl.when(jnp.logical_and(s > 0, is_even))
    def _():
        project(h_odd)

        @pl.when(s < n_tiles)
        def _():
            normalise(h_even)

    @pl.when(jnp.logical_not(is_even))
    def _():
        project(h_even)

        @pl.when(s < n_tiles)
        def _():
            normalise(h_odd)


def _in_proj(x2d, gain, mod, w_in):
    n_tiles = SEQ // PROJ_TM
    return pl.pallas_call(
        _proj_kernel,
        grid=(n_tiles + 1,),
        in_specs=[
            pl.BlockSpec((PROJ_TM, D_MODEL), lambda s: (jnp.minimum(s, n_tiles - 1), 0)),
            pl.BlockSpec((1, D_MODEL), lambda s: (0, 0)),
            pl.BlockSpec((1, D_MODEL), lambda s: (0, MOD_SHIFT)),
            pl.BlockSpec((1, D_MODEL), lambda s: (0, MOD_SCALE)),
            pl.BlockSpec(memory_space=pl.ANY),
        ],
        out_specs=pl.BlockSpec((PROJ_TM, PROJ_WIDTH), lambda s: (jnp.maximum(s - 1, 0), 0)),
        out_shape=jax.ShapeDtypeStruct((SEQ, PROJ_WIDTH), F32),
        scratch_shapes=[
            pltpu.VMEM((D_MODEL, PROJ_WIDTH), BF16),
            pltpu.VMEM((PROJ_W_SLOTS, D_MODEL, PROJ_W_CHUNK), F32),
            pltpu.VMEM((PROJ_TM, D_MODEL), BF16),
            pltpu.VMEM((PROJ_TM, D_MODEL), BF16),
            pltpu.SemaphoreType.DMA((PROJ_W_SLOTS,)),
        ],
        compiler_params=pltpu.CompilerParams(
            dimension_semantics=("arbitrary",), vmem_limit_bytes=PROJ_VMEM_LIMIT_BYTES),
        name="in_proj",
    )(x2d, gain, mod, mod, w_in)


def _rows(start, size, stride):
    return pl.ds(start, size) if stride == 1 else pl.ds(start, size, stride=stride)


def _attn_kernel(q_ref, k_ref, v_ref, g_ref, bias_ref, o_ref,
                 kd, vd, tmp, s_s, mx_s, m_s, l_s, acc_s):
    sb = pl.program_id(1)
    lane = lax.broadcasted_iota(jnp.int32, (ATTN_QB, LANES), 1)
    is_h0 = lane < HEAD_DIM

    @pl.when(sb == 0)
    def _():
        for src_ref, dst in ((k_ref, kd), (v_ref, vd)):
            for p, (_, d) in enumerate(DILATED_PATTERNS):
                dst[p, pl.ds(0, ATTN_QB), :] = jnp.zeros((ATTN_QB, LANES), BF16)
                d_prev = DILATED_PATTERNS[p - 1][1] if p else 1
                ratio = d // d_prev
                assert p == 0 and d == 1 or ratio == DEINT_RATIO
                per_residue = SEQ // d // DEINT_ROWS
                keep_f32 = 0 < p < len(DILATED_PATTERNS) - 1
                from_ref = src_ref if p <= 1 else tmp

                def deint(c, carry, p=p, d_prev=d_prev, ratio=ratio, per_residue=per_residue,
                          keep_f32=keep_f32, from_ref=from_ref, dst=dst):
                    r = lax.shift_right_logical(c, per_residue.bit_length() - 1)
                    chunk = jnp.bitwise_and(c, per_residue - 1)
                    r_prev = jnp.bitwise_and(r, d_prev - 1)
                    j = lax.shift_right_logical(r, d_prev.bit_length() - 1)
                    start = r_prev * (SEQ // d_prev) + chunk * (DEINT_ROWS * ratio) + j
                    x = from_ref[_rows(start, DEINT_ROWS, ratio), :]
                    if keep_f32:
                        tmp[pl.ds(pl.multiple_of(c * DEINT_ROWS, DEINT_ROWS), DEINT_ROWS), :] = x
                    rows = pl.ds(pl.multiple_of(ATTN_QB + c * DEINT_ROWS, ATTN_QB), DEINT_ROWS)
                    dst[p, rows, :] = x.astype(BF16)
                    return carry

                lax.fori_loop(0, SEQ // DEINT_ROWS, deint, 0, unroll=4)

    ones = jnp.ones((2 * ATTN_QB, LANES), BF16)

    for p, (_, d) in enumerate(DILATED_PATTERNS):
        log2d = d.bit_length() - 1
        blocks_per_step = ATTN_SB // (ATTN_QB * d)

        def place(ti, d=d, log2d=log2d, blocks_per_step=blocks_per_step):
            r = jnp.bitwise_and(ti, d - 1)
            n = lax.shift_right_logical(ti, log2d)
            blk = sb * blocks_per_step + n
            q_rows = _rows(r + n * (ATTN_QB * d), ATTN_QB, d)
            window = pl.ds(pl.multiple_of(r * (SEQ // d) + blk * ATTN_QB, ATTN_QB), 2 * ATTN_QB)
            return q_rows, window, blk == 0

        for g0 in range(0, ATTN_SB // ATTN_QB, ATTN_GROUP):

            def scores(i, carry, p=p, g0=g0, place=place):
                q_rows, window, first = place(g0 + i)
                q = (q_ref[q_rows, :] * (ATTN_SCALE * LOG2E)).astype(BF16)
                zero = jnp.zeros_like(q)
                qs = jnp.concatenate(
                    [jnp.where(is_h0, q, zero), jnp.where(is_h0, zero, q)], axis=0)
                s = lax.dot_general(qs, kd[p, window, :], (((1,), (1,)), ((), ())),
                                    preferred_element_type=F32)
                s = s + bias_ref[p, first.astype(jnp.int32)]
                s_s[i] = s
                mx_s[i] = jnp.broadcast_to(jnp.max(s, axis=1, keepdims=True), (2 * ATTN_QB, LANES))
                return carry

            lax.fori_loop(0, ATTN_GROUP, scores, 0, unroll=ATTN_UNROLL)

            def weighted(i, carry, p=p, g0=g0, place=place):
                q_rows, window, _ = place(g0 + i)
                mx = mx_s[i]
                e = jnp.concatenate(
                    [jnp.exp2(s_s[i, :, :LANES] - mx), jnp.exp2(s_s[i, :, LANES:] - mx)],
                    axis=1).astype(BF16)
                vc = jnp.concatenate([vd[p, window, :], ones], axis=1)
                pv = jnp.dot(e, vc, preferred_element_type=F32)
                acc_s[p, q_rows, :] = jnp.where(is_h0, pv[:ATTN_QB, :LANES], pv[ATTN_QB:, :LANES])
                l_s[p, q_rows, :] = jnp.where(is_h0, pv[:ATTN_QB, LANES:], pv[ATTN_QB:, LANES:])
                m_s[p, q_rows, :] = jnp.where(is_h0, mx[:ATTN_QB], mx[ATTN_QB:])
                return carry

            lax.fori_loop(0, ATTN_GROUP, weighted, 0, unroll=ATTN_UNROLL)

    def combine(i, carry):
        rows = pl.ds(pl.multiple_of(i * COMBINE_ROWS, COMBINE_ROWS), COMBINE_ROWS)
        m0, m1, m2 = m_s[0, rows, :], m_s[1, rows, :], m_s[2, rows, :]
        mm = jnp.maximum(jnp.maximum(m0, m1), m2)
        e0, e1, e2 = jnp.exp2(m0 - mm), jnp.exp2(m1 - mm), jnp.exp2(m2 - mm)
        num = e0 * acc_s[0, rows, :] + e1 * acc_s[1, rows, :] + e2 * acc_s[2, rows, :]
        den = e0 * l_s[0, rows, :] + e1 * l_s[1, rows, :] + e2 * l_s[2, rows, :]
        o_ref[rows, :] = (num / den * _silu(g_ref[rows, :])).astype(BF16)
        return carry

    lax.fori_loop(0, ATTN_SB // COMBINE_ROWS, combine, 0)


def _attn_bias():
    slopes = 2.0 ** (-8.0 * jnp.arange(1, N_HEADS + 1, dtype=F32) / N_HEADS)
    a = jnp.arange(ATTN_QB, dtype=jnp.int32)[:, None]
    b = jnp.arange(2 * ATTN_QB, dtype=jnp.int32)[None, :]
    dist = a + ATTN_QB - b
    valid = (dist >= 0) & (dist <= ATTN_QB)
    per_pattern = []
    for _, d in DILATED_PATTERNS:
        offs = (dist * d).astype(F32)
        pen = -(slopes[:, None, None] * offs[None]) * LOG2E
        normal = jnp.where(valid[None], pen, NEG_INF)
        no_prev = jnp.where((b >= ATTN_QB)[None], normal, NEG_INF)
        per_pattern.append(jnp.stack([normal, no_prev], axis=1))
    t = jnp.stack(per_pattern, axis=1)
    n_p = len(DILATED_PATTERNS)
    t = t.reshape(N_HEADS // HEADS_PER_BLOCK, HEADS_PER_BLOCK, n_p, 2, ATTN_QB, 2 * ATTN_QB)
    t = t.transpose(0, 2, 3, 1, 4, 5)
    return t.reshape(N_HEADS // HEADS_PER_BLOCK, n_p, 2, HEADS_PER_BLOCK * ATTN_QB, 2 * ATTN_QB)


def _attention(proj, bias):
    n_hb = ATTN_WIDTH // LANES
    n_p = len(DILATED_PATTERNS)
    return pl.pallas_call(
        _attn_kernel,
        grid=(n_hb, SEQ // ATTN_SB),
        in_specs=[
            pl.BlockSpec((ATTN_SB, LANES), lambda h, s: (s, h)),
            pl.BlockSpec((SEQ, LANES), lambda h, s: (0, n_hb + h)),
            pl.BlockSpec((SEQ, LANES), lambda h, s: (0, 2 * n_hb + h)),
            pl.BlockSpec((ATTN_SB, LANES), lambda h, s: (s, 3 * n_hb + h)),
            pl.BlockSpec((None, n_p, 2, HEADS_PER_BLOCK * ATTN_QB, 2 * ATTN_QB),
                         lambda h, s: (h, 0, 0, 0, 0)),
        ],
        out_specs=pl.BlockSpec((ATTN_SB, LANES), lambda h, s: (s, h)),
        out_shape=jax.ShapeDtypeStruct((SEQ, ATTN_WIDTH), BF16),
        scratch_shapes=[
            pltpu.VMEM((n_p, ATTN_QB + SEQ, LANES), BF16),
            pltpu.VMEM((n_p, ATTN_QB + SEQ, LANES), BF16),
            pltpu.VMEM((SEQ, LANES), F32),
            pltpu.VMEM((ATTN_GROUP, 2 * ATTN_QB, 2 * ATTN_QB), F32),
            pltpu.VMEM((ATTN_GROUP, 2 * ATTN_QB, LANES), F32),
        ] + [pltpu.VMEM((n_p, ATTN_SB, LANES), F32)] * 3,
        compiler_params=pltpu.CompilerParams(
            dimension_semantics=("arbitrary", "arbitrary"),
            vmem_limit_bytes=ATTN_VMEM_LIMIT_BYTES),
        name="dilated_attn",
    )(proj, proj, proj, proj, bias)


def _lru_kernel(u_ref, g_ref, cw_ref, cb_ref, wg_ref, bg_ref, lam_ref, wout_ref,
                o_ref, wout_bf16_ref, ubuf, a_s, b_s, h_s, p_s):
    wout_bf16_ref[...] = wout_ref[...].astype(BF16)

    lam = lam_ref[...]
    log_sig = jnp.minimum(lam, 0.0) - jnp.log1p(jnp.exp(-jnp.abs(lam)))
    chunks_per_seg = LRU_SEG_LEN // LRU_CHUNK

    ubuf[pl.ds(0, SUBLANES), :] = jnp.zeros((SUBLANES, LANES), F32)

    def chunk(c, carry):
        rows = pl.ds(pl.multiple_of(c * LRU_CHUNK, LRU_CHUNK), LRU_CHUNK)
        ubuf[pl.ds(SUBLANES, LRU_CHUNK), :] = u_ref[rows, :]
        xc = cb_ref[...]
        for j in range(CONV_WIDTH):
            off = SUBLANES - (CONV_WIDTH - 1) + j
            xc = xc + ubuf[pl.ds(off, LRU_CHUNK), :] * cw_ref[pl.ds(j, 1), :]
        ubuf[pl.ds(0, SUBLANES), :] = ubuf[pl.ds(LRU_CHUNK, SUBLANES), :]

        z = jnp.dot(xc.astype(BF16), wg_ref[...], preferred_element_type=F32) + bg_ref[...]
        r = _sigmoid(z[:, :LANES])
        i = _sigmoid(z[:, LANES:])
        log_a = LRU_C * r * log_sig
        a = jnp.exp(log_a)
        th = jnp.tanh(log_a)
        x2 = -2.0 * th / (1.0 - th)
        mult = jnp.where(x2 > 0.0, x2 * lax.rsqrt(x2), 0.0)
        b = mult * (i * xc)

        seg = lax.shift_right_logical(c, chunks_per_seg.bit_length() - 1)
        within = jnp.bitwise_and(c, chunks_per_seg - 1)
        dst = pl.ds(pl.multiple_of(seg * LRU_SEG_PITCH + within * LRU_CHUNK, SUBLANES), LRU_CHUNK)
        a_s[dst, :] = a
        b_s[dst, :] = b
        return carry

    lax.fori_loop(0, SEQ // LRU_CHUNK, chunk, 0, unroll=2)

    def step(t, carry):
        h, prod = carry
        rows = pl.ds(t, LRU_SEGMENTS, stride=LRU_SEG_PITCH)
        a = a_s[rows, :]
        h = a * h + b_s[rows, :]
        prod = prod * a
        h_s[rows, :] = h
        p_s[rows, :] = prod
        return h, prod

    h_end, p_end = lax.fori_loop(
        0, LRU_SEG_LEN, step,
        (jnp.zeros((LRU_SEGMENTS, LANES), F32), jnp.ones((LRU_SEGMENTS, LANES), F32)), unroll=8)

    seg_id = lax.broadcasted_iota(jnp.int32, (LRU_SEGMENTS, LANES), 0)
    c_in = jnp.zeros((LRU_SEGMENTS, LANES), F32)
    for _ in range(LRU_SEGMENTS - 1):
        c_in = jnp.where(seg_id == 0, 0.0, pltpu.roll(h_end + p_end * c_in, 1, 0))

    for s in range(LRU_SEGMENTS):
        c_s = c_in[s:s + 1, :]

        def gate(k, carry, s=s, c_s=c_s):
            off = pl.multiple_of(k * LRU_CHUNK, LRU_CHUNK)
            src = pl.ds(s * LRU_SEG_PITCH + off, LRU_CHUNK)
            rows = pl.ds(s * LRU_SEG_LEN + off, LRU_CHUNK)
            h = h_s[src, :] + p_s[src, :] * c_s
            o_ref[rows, :] = (h * _silu(g_ref[rows, :])).astype(BF16)
            return carry

        lax.fori_loop(0, chunks_per_seg, gate, 0)


def _lru(proj, conv_w, conv_b, w_gates, b_gates, lam, w_out):
    n_cb = LRU_WIDTH // LANES
    u_col0 = 4 * ATTN_WIDTH // LANES
    g_col0 = u_col0 + n_cb
    return pl.pallas_call(
        _lru_kernel,
        grid=(n_cb,),
        in_specs=[
            pl.BlockSpec((SEQ, LANES), lambda j: (0, u_col0 + j)),
            pl.BlockSpec((SEQ, LANES), lambda j: (0, g_col0 + j)),
            pl.BlockSpec((CONV_WIDTH, LANES), lambda j: (0, j)),
            pl.BlockSpec((1, LANES), lambda j: (0, j)),
            pl.BlockSpec((None, LANES, 2 * LANES), lambda j: (j, 0, 0)),
            pl.BlockSpec((None, 1, 2 * LANES), lambda j: (j, 0, 0)),
            pl.BlockSpec((1, LANES), lambda j: (0, j)),
            pl.BlockSpec((D_MODEL // n_cb, D_MODEL), lambda j: (j, 0)),
        ],
        out_specs=[
            pl.BlockSpec((SEQ, LANES), lambda j: (0, j)),
            pl.BlockSpec((D_MODEL // n_cb, D_MODEL), lambda j: (j, 0)),
        ],
        out_shape=[
            jax.ShapeDtypeStruct((SEQ, LRU_WIDTH), BF16),
            jax.ShapeDtypeStruct((D_MODEL, D_MODEL), BF16),
        ],
        scratch_shapes=[
            pltpu.VMEM((LRU_CHUNK + SUBLANES, LANES), F32),
        ] + [pltpu.VMEM((LRU_SEGMENTS * LRU_SEG_PITCH, LANES), F32)] * 4,
        compiler_params=pltpu.CompilerParams(
            dimension_semantics=("arbitrary",), vmem_limit_bytes=VMEM_LIMIT_BYTES),
        name="rglru",
    )(proj, proj, conv_w, conv_b, w_gates, b_gates, lam, w_out)


def _block_diag_gates(w_rgate, b_rgate, w_igate, b_igate):
    per = LANES // LRU_BLOCK_W
    n_cb = LRU_WIDTH // LANES

    def bd(w):
        w = w.reshape(n_cb, per, LRU_BLOCK_W, LRU_BLOCK_W)
        eye = jnp.eye(per, dtype=w.dtype)
        return jnp.einsum('cpkj,pq->cpkqj', w, eye).reshape(n_cb, LANES, LANES)

    w = jnp.concatenate([bd(w_rgate), bd(w_igate)], axis=-1).astype(BF16)
    b = jnp.concatenate([b_rgate.reshape(n_cb, 1, LANES), b_igate.reshape(n_cb, 1, LANES)], axis=-1)
    return w, b


def _out_kernel(ma_ref, ml_ref, wa_ref, wl_ref, x_ref, gate_ref, fg_ref, o_ref):
    mix = jnp.dot(ma_ref[...], wa_ref[...], preferred_element_type=F32)
    mix = mix + jnp.dot(ml_ref[...], wl_ref[...], preferred_element_type=F32)
    y = x_ref[...] + gate_ref[...] * mix
    var = jnp.mean(y * y, axis=-1, keepdims=True)
    o_ref[...] = y * lax.rsqrt(var + EPS) * fg_ref[...]


def _out_proj(mixed_attn, mixed_lru, w_out_bf16, x2d, mod, final_gain):
    return pl.pallas_call(
        _out_kernel,
        grid=(SEQ // OUT_TM,),
        in_specs=[
            pl.BlockSpec((OUT_TM, ATTN_WIDTH), lambda i: (i, 0)),
            pl.BlockSpec((OUT_TM, LRU_WIDTH), lambda i: (i, 0)),
            pl.BlockSpec((ATTN_WIDTH, D_MODEL), lambda i: (0, 0)),
            pl.BlockSpec((LRU_WIDTH, D_MODEL), lambda i: (1, 0)),
            pl.BlockSpec((OUT_TM, D_MODEL), lambda i: (i, 0)),
            pl.BlockSpec((1, D_MODEL), lambda i: (0, MOD_GATE)),
            pl.BlockSpec((1, D_MODEL), lambda i: (0, 0)),
        ],
        out_specs=pl.BlockSpec((OUT_TM, D_MODEL), lambda i: (i, 0)),
        out_shape=jax.ShapeDtypeStruct((SEQ, D_MODEL), F32),
        compiler_params=pltpu.CompilerParams(
            dimension_semantics=("arbitrary",), vmem_limit_bytes=VMEM_LIMIT_BYTES),
        name="out_proj",
    )(mixed_attn, mixed_lru, w_out_bf16, w_out_bf16, x2d, mod, final_gain)


@jax.jit
def kernel(x, c, norm_gain, w_ada, b_ada, w_in, conv_w, conv_b, w_rgate, b_rgate,
           w_igate, b_igate, lru_lambda, w_out, final_gain):
    assert x.shape == (1, SEQ, D_MODEL) and norm_gain.shape[0] == 1
    x2d = x.reshape(SEQ, D_MODEL)
    mod = _ada_mod(c.reshape(D_MODEL, 1), w_ada[0], b_ada)
    proj = _in_proj(x2d, norm_gain, mod, w_in[0])
    mixed_attn = _attention(proj, _attn_bias())
    w_gates, b_gates = _block_diag_gates(w_rgate[0], b_rgate[0], w_igate[0], b_igate[0])
    mixed_lru, w_out_bf16 = _lru(proj, conv_w[0], conv_b, w_gates, b_gates, lru_lambda, w_out[0])
    y = _out_proj(mixed_attn, mixed_lru, w_out_bf16, x2d, mod,
                  final_gain.reshape(1, D_MODEL))
    return y.reshape(1, SEQ, D_MODEL)
```

```python
import functools
import math

import jax
import jax.numpy as jnp
import numpy as np
from jax import lax
from jax.experimental import pallas as pl
from jax.experimental.pallas import tpu as pltpu

D_MODEL = 2048
SEQ = 8192
ATTN_WIDTH = D_MODEL // 2
LRU_WIDTH = D_MODEL - ATTN_WIDTH
HEAD_DIM = 64
N_HEADS = ATTN_WIDTH // HEAD_DIM
LRU_BLOCKS = 16
LRU_BLOCK_W = LRU_WIDTH // LRU_BLOCKS
CONV_WIDTH = 4
LRU_C = 8.0
DILATED_PATTERNS = ((128, 1), (512, 4), (2048, 16))
ATTN_SCALE = 1.0 / math.sqrt(HEAD_DIM)
NEG_INF = -1e30
LOG2E = math.log2(math.e)
EPS = 1e-6
PROJ_WIDTH = 4 * ATTN_WIDTH + 2 * LRU_WIDTH
MOD_SHIFT, MOD_SCALE, MOD_GATE = 0, 1, 2

LANES = 128
SUBLANES = 8
VMEM_LIMIT_BYTES = 56 * 1024 * 1024

ADA_TK = 256
PROJ_TM = 256
PROJ_W_CHUNK = 512
PROJ_W_SLOTS = 2
PROJ_VMEM_LIMIT_BYTES = 60 * 1024 * 1024
NORM_ROWS = 64
ATTN_QB = 128
ATTN_SB = 2048
ATTN_GROUP = 4
DEINT_ROWS = 256
DEINT_RATIO = 4
ATTN_VMEM_LIMIT_BYTES = 60 * 1024 * 1024
HEADS_PER_BLOCK = LANES // HEAD_DIM
COMBINE_ROWS = 256
LRU_CHUNK = 256
LRU_SEGMENTS = SUBLANES
LRU_SEG_LEN = SEQ // LRU_SEGMENTS
LRU_SEG_PITCH = LRU_SEG_LEN + SUBLANES
OUT_TM = 512

F32 = jnp.float32
BF16 = jnp.bfloat16


def _sigmoid(x):
    return 0.5 * jnp.tanh(0.5 * x) + 0.5


def _silu(x):
    return x * _sigmoid(x)


def _ada_kernel(c_ref, w_ref, b_ref, o_ref):
    @pl.when(pl.program_id(0) == 0)
    def _():
        o_ref[...] = b_ref[...]

    c_act = _silu(c_ref[...])
    o_ref[...] += jnp.sum(w_ref[...] * c_act, axis=0, keepdims=True)


def _ada_mod(c_col, w_ada, b_ada):
    n = w_ada.shape[1]
    return pl.pallas_call(
        _ada_kernel,
        grid=(D_MODEL // ADA_TK,),
        in_specs=[
            pl.BlockSpec((ADA_TK, 1), lambda k: (k, 0)),
            pl.BlockSpec((ADA_TK, n), lambda k: (k, 0)),
            pl.BlockSpec((1, n), lambda k: (0, 0)),
        ],
        out_specs=pl.BlockSpec((1, n), lambda k: (0, 0)),
        out_shape=jax.ShapeDtypeStruct((1, n), F32),
        compiler_params=pltpu.CompilerParams(
            dimension_semantics=("arbitrary",), vmem_limit_bytes=VMEM_LIMIT_BYTES),
        name="ada_mod",
    )(c_col, w_ada, b_ada)


def _w_chunk_copy(w_hbm, stage, sem, c):
    slot = c % PROJ_W_SLOTS
    cols = pl.ds(c * PROJ_W_CHUNK, PROJ_W_CHUNK)
    return pltpu.make_async_copy(w_hbm.at[:, cols], stage.at[slot], sem.at[slot])


def _proj_kernel(x_ref, gain_ref, shift_ref, scale_ref, w_hbm, o_ref,
                 w_res, stage, h_even, h_odd, sem):
    s = pl.program_id(0)
    n_tiles = pl.num_programs(0) - 1
    n_chunks = PROJ_WIDTH // PROJ_W_CHUNK

    def normalise(h_ref):
        mul = gain_ref[...] * (1.0 + scale_ref[...])
        shift = shift_ref[...]
        for r0 in range(0, PROJ_TM, NORM_ROWS):
            xv = x_ref[pl.ds(r0, NORM_ROWS), :]
            var = jnp.mean(xv * xv, axis=-1, keepdims=True)
            h = xv * lax.rsqrt(var + EPS) * mul + shift
            h_ref[pl.ds(r0, NORM_ROWS), :] = h.astype(BF16)

    def project(h_ref):
        o_ref[...] = jnp.dot(h_ref[...], w_res[...], preferred_element_type=F32)

    @pl.when(s == 0)
    def _():
        for c in range(PROJ_W_SLOTS):
            _w_chunk_copy(w_hbm, stage, sem, c).start()
        normalise(h_even)
        for c in range(n_chunks):
            _w_chunk_copy(w_hbm, stage, sem, c).wait()
            cols = pl.ds(c * PROJ_W_CHUNK, PROJ_W_CHUNK)
            w_res[:, cols] = stage[c % PROJ_W_SLOTS].astype(BF16)
            if c + PROJ_W_SLOTS < n_chunks:
                _w_chunk_copy(w_hbm, stage, sem, c + PROJ_W_SLOTS).start()

    is_even = jnp.bitwise_and(s, 1) == 0

    @pl.when(jnp.logical_and(s > 0, is_even))
    def _():
        project(h_odd)

        @pl.when(s < n_tiles)
        def _():
            normalise(h_even)

    @pl.when(jnp.logical_not(is_even))
    def _():
        project(h_even)

        @pl.when(s < n_tiles)
        def _():
            normalise(h_odd)


def _in_proj(x2d, gain, mod, w_in):
    n_tiles = SEQ // PROJ_TM
    return pl.pallas_call(
        _proj_kernel,
        grid=(n_tiles + 1,),
        in_specs=[
            pl.BlockSpec((PROJ_TM, D_MODEL), lambda s: (jnp.minimum(s, n_tiles - 1), 0)),
            pl.BlockSpec((1, D_MODEL), lambda s: (0, 0)),
            pl.BlockSpec((1, D_MODEL), lambda s: (0, MOD_SHIFT)),
            pl.BlockSpec((1, D_MODEL), lambda s: (0, MOD_SCALE)),
            pl.BlockSpec(memory_space=pl.ANY),
        ],
        out_specs=pl.BlockSpec((PROJ_TM, PROJ_WIDTH), lambda s: (jnp.maximum(s - 1, 0), 0)),
        out_shape=jax.ShapeDtypeStruct((SEQ, PROJ_WIDTH), F32),
        scratch_shapes=[
            pltpu.VMEM((D_MODEL, PROJ_WIDTH), BF16),
            pltpu.VMEM((PROJ_W_SLOTS, D_MODEL, PROJ_W_CHUNK), F32),
            pltpu.VMEM((PROJ_TM, D_MODEL), BF16),
            pltpu.VMEM((PROJ_TM, D_MODEL), BF16),
            pltpu.SemaphoreType.DMA((PROJ_W_SLOTS,)),
        ],
        compiler_params=pltpu.CompilerParams(
            dimension_semantics=("arbitrary",), vmem_limit_bytes=PROJ_VMEM_LIMIT_BYTES),
        name="in_proj",
    )(x2d, gain, mod, mod, w_in)


def _rows(start, size, stride):
    return pl.ds(start, size) if stride == 1 else pl.ds(start, size, stride=stride)


def _attn_kernel(q_ref, k_ref, v_ref, g_ref, coef_ref, o_ref,
                 kd, vd, tmp, bias_s, s_s, mx_s, m_s, l_s, acc_s):
    sb = pl.program_id(1)
    lane = lax.broadcasted_iota(jnp.int32, (ATTN_QB, LANES), 1)
    is_h0 = lane < HEAD_DIM

    @pl.when(sb == 0)
    def _():
        for src_ref, dst in ((k_ref, kd), (v_ref, vd)):
            for p, (_, d) in enumerate(DILATED_PATTERNS):
                dst[p, pl.ds(0, ATTN_QB), :] = jnp.zeros((ATTN_QB, LANES), BF16)
                d_prev = DILATED_PATTERNS[p - 1][1] if p else 1
                ratio = d // d_prev
                assert p == 0 and d == 1 or ratio == DEINT_RATIO
                per_residue = SEQ // d // DEINT_ROWS
                keep_f32 = 0 < p < len(DILATED_PATTERNS) - 1
                from_ref = src_ref if p <= 1 else tmp

                def deint(c, carry, p=p, d_prev=d_prev, ratio=ratio, per_residue=per_residue,
                          keep_f32=keep_f32, from_ref=from_ref, dst=dst):
                    r = lax.shift_right_logical(c, per_residue.bit_length() - 1)
                    chunk = jnp.bitwise_and(c, per_residue - 1)
                    r_prev = jnp.bitwise_and(r, d_prev - 1)
                    j = lax.shift_right_logical(r, d_prev.bit_length() - 1)
                    start = r_prev * (SEQ // d_prev) + chunk * (DEINT_ROWS * ratio) + j
                    x = from_ref[_rows(start, DEINT_ROWS, ratio), :]
                    if keep_f32:
                        tmp[pl.ds(pl.multiple_of(c * DEINT_ROWS, DEINT_ROWS), DEINT_ROWS), :] = x
                    rows = pl.ds(pl.multiple_of(ATTN_QB + c * DEINT_ROWS, ATTN_QB), DEINT_ROWS)
                    dst[p, rows, :] = x.astype(BF16)
                    return carry

                lax.fori_loop(0, SEQ // DEINT_ROWS, deint, 0, unroll=4)

        qi = lax.broadcasted_iota(jnp.int32, (ATTN_QB, 2 * ATTN_QB), 0)
        ki = lax.broadcasted_iota(jnp.int32, (ATTN_QB, 2 * ATTN_QB), 1)
        dist = qi + ATTN_QB - ki
        in_band = jnp.logical_and(dist >= 0, dist <= ATTN_QB)
        in_band_cur = jnp.logical_and(in_band, ki >= ATTN_QB)
        dist_f = dist.astype(F32)
        for p in range(len(DILATED_PATTERNS)):
            for h in range(HEADS_PER_BLOCK):
                pen = -(coef_ref[pl.program_id(0), p * HEADS_PER_BLOCK + h] * dist_f)
                rows = pl.ds(h * ATTN_QB, ATTN_QB)
                bias_s[p, 0, rows, :] = jnp.where(in_band, pen, NEG_INF)
                bias_s[p, 1, rows, :] = jnp.where(in_band_cur, pen, NEG_INF)

    ones = jnp.ones((2 * ATTN_QB, LANES), BF16)

    def place(d, ti):
        r, n = ti % d, ti // d
        blk = sb * (ATTN_SB // (ATTN_QB * d)) + n
        q_rows = _rows(r + n * (ATTN_QB * d), ATTN_QB, d)
        window = pl.ds(pl.multiple_of(r * (SEQ // d) + blk * ATTN_QB, ATTN_QB), 2 * ATTN_QB)
        return q_rows, window, blk == 0

    def scores(p, d, g0, half):
        for i in range(ATTN_GROUP):
            q_rows, window, first = place(d, g0 + i)
            q = (q_ref[q_rows, :] * (ATTN_SCALE * LOG2E)).astype(BF16)
            zero = jnp.zeros_like(q)
            qs = jnp.concatenate([jnp.where(is_h0, q, zero), jnp.where(is_h0, zero, q)], axis=0)
            s = lax.dot_general(qs, kd[p, window, :], (((1,), (1,)), ((), ())),
                                preferred_element_type=F32)
            s = s + bias_s[p, first.astype(jnp.int32)]
            slot = half * ATTN_GROUP + i
            s_s[slot] = s
            mx_s[slot] = jnp.broadcast_to(jnp.max(s, axis=1, keepdims=True), (2 * ATTN_QB, LANES))

    def weighted(p, d, g0, half):
        for i in range(ATTN_GROUP):
            q_rows, window, _ = place(d, g0 + i)
            slot = half * ATTN_GROUP + i
            mx = mx_s[slot]
            e = jnp.concatenate(
                [jnp.exp2(s_s[slot, :, :LANES] - mx), jnp.exp2(s_s[slot, :, LANES:] - mx)],
                axis=1).astype(BF16)
            vc = jnp.concatenate([vd[p, window, :], ones], axis=1)
            pv = jnp.dot(e, vc, preferred_element_type=F32)
            acc_s[p, q_rows, :] = jnp.where(is_h0, pv[:ATTN_QB, :LANES], pv[ATTN_QB:, :LANES])
            l_s[p, q_rows, :] = jnp.where(is_h0, pv[:ATTN_QB, LANES:], pv[ATTN_QB:, LANES:])
            m_s[p, q_rows, :] = jnp.where(is_h0, mx[:ATTN_QB], mx[ATTN_QB:])

    groups = [(p, d, g0) for p, (_, d) in enumerate(DILATED_PATTERNS)
              for g0 in range(0, ATTN_SB // ATTN_QB, ATTN_GROUP)]
    scores(*groups[0], 0)
    for k, group in enumerate(groups):
        if k + 1 < len(groups):
            scores(*groups[k + 1], (k + 1) % 2)
        weighted(*group, k % 2)

    def combine(i, carry):
        rows = pl.ds(pl.multiple_of(i * COMBINE_ROWS, COMBINE_ROWS), COMBINE_ROWS)
        m0, m1, m2 = m_s[0, rows, :], m_s[1, rows, :], m_s[2, rows, :]
        mm = jnp.maximum(jnp.maximum(m0, m1), m2)
        e0, e1, e2 = jnp.exp2(m0 - mm), jnp.exp2(m1 - mm), jnp.exp2(m2 - mm)
        num = e0 * acc_s[0, rows, :] + e1 * acc_s[1, rows, :] + e2 * acc_s[2, rows, :]
        den = e0 * l_s[0, rows, :] + e1 * l_s[1, rows, :] + e2 * l_s[2, rows, :]
        o_ref[rows, :] = (num / den * _silu(g_ref[rows, :])).astype(BF16)
        return carry

    lax.fori_loop(0, ATTN_SB // COMBINE_ROWS, combine, 0)


def _alibi_coefs():
    slopes = 2.0 ** (-8.0 * np.arange(1, N_HEADS + 1, dtype=np.float64) / N_HEADS)
    dil = np.array([d for _, d in DILATED_PATTERNS], dtype=np.float64)
    coef = slopes.reshape(-1, HEADS_PER_BLOCK)[:, None, :] * dil[None, :, None] * LOG2E
    return jnp.asarray(coef.reshape(N_HEADS // HEADS_PER_BLOCK, -1), dtype=F32)


def _attention(proj):
    n_hb = ATTN_WIDTH // LANES
    n_p = len(DILATED_PATTERNS)
    return pl.pallas_call(
        _attn_kernel,
        grid=(n_hb, SEQ // ATTN_SB),
        in_specs=[
            pl.BlockSpec((ATTN_SB, LANES), lambda h, s: (s, h)),
            pl.BlockSpec((SEQ, LANES), lambda h, s: (0, n_hb + h)),
            pl.BlockSpec((SEQ, LANES), lambda h, s: (0, 2 * n_hb + h)),
            pl.BlockSpec((ATTN_SB, LANES), lambda h, s: (s, 3 * n_hb + h)),
            pl.BlockSpec(memory_space=pltpu.SMEM),
        ],
        out_specs=pl.BlockSpec((ATTN_SB, LANES), lambda h, s: (s, h)),
        out_shape=jax.ShapeDtypeStruct((SEQ, ATTN_WIDTH), BF16),
        scratch_shapes=[
            pltpu.VMEM((n_p, ATTN_QB + SEQ, LANES), BF16),
            pltpu.VMEM((n_p, ATTN_QB + SEQ, LANES), BF16),
            pltpu.VMEM((SEQ, LANES), F32),
            pltpu.VMEM((n_p, 2, HEADS_PER_BLOCK * ATTN_QB, 2 * ATTN_QB), F32),
            pltpu.VMEM((2 * ATTN_GROUP, 2 * ATTN_QB, 2 * ATTN_QB), F32),
            pltpu.VMEM((2 * ATTN_GROUP, 2 * ATTN_QB, LANES), F32),
        ] + [pltpu.VMEM((n_p, ATTN_SB, LANES), F32)] * 3,
        compiler_params=pltpu.CompilerParams(
            dimension_semantics=("arbitrary", "arbitrary"),
            vmem_limit_bytes=ATTN_VMEM_LIMIT_BYTES),
        name="dilated_attn",
    )(proj, proj, proj, proj, _alibi_coefs())


def _lru_kernel(u_ref, g_ref, cw_ref, cb_ref, wg_ref, bg_ref, lam_ref, wout_ref,
                o_ref, wout_bf16_ref, ubuf, a_s, b_s, h_s, p_s):
    wout_bf16_ref[...] = wout_ref[...].astype(BF16)

    lam = lam_ref[...]
    log_sig = jnp.minimum(lam, 0.0) - jnp.log1p(jnp.exp(-jnp.abs(lam)))
    chunks_per_seg = LRU_SEG_LEN // LRU_CHUNK

    ubuf[pl.ds(0, SUBLANES), :] = jnp.zeros((SUBLANES, LANES), F32)

    def chunk(c, carry):
        rows = pl.ds(pl.multiple_of(c * LRU_CHUNK, LRU_CHUNK), LRU_CHUNK)
        ubuf[pl.ds(SUBLANES, LRU_CHUNK), :] = u_ref[rows, :]
        xc = cb_ref[...]
        for j in range(CONV_WIDTH):
            off = SUBLANES - (CONV_WIDTH - 1) + j
            xc = xc + ubuf[pl.ds(off, LRU_CHUNK), :] * cw_ref[pl.ds(j, 1), :]
        ubuf[pl.ds(0, SUBLANES), :] = ubuf[pl.ds(LRU_CHUNK, SUBLANES), :]

        z = jnp.dot(xc.astype(BF16), wg_ref[...], preferred_element_type=F32) + bg_ref[...]
        r = _sigmoid(z[:, :LANES])
        i = _sigmoid(z[:, LANES:])
        log_a = LRU_C * r * log_sig
        a = jnp.exp(log_a)
        th = jnp.tanh(log_a)
        x2 = -2.0 * th / (1.0 - th)
        mult = jnp.where(x2 > 0.0, x2 * lax.rsqrt(x2), 0.0)
        b = mult * (i * xc)

        seg = lax.shift_right_logical(c, chunks_per_seg.bit_length() - 1)
        within = jnp.bitwise_and(c, chunks_per_seg - 1)
        dst = pl.ds(pl.multiple_of(seg * LRU_SEG_PITCH + within * LRU_CHUNK, SUBLANES), LRU_CHUNK)
        a_s[dst, :] = a
        b_s[dst, :] = b
        return carry

    lax.fori_loop(0, SEQ // LRU_CHUNK, chunk, 0, unroll=2)

    def step(t, carry):
        h, prod = carry
        rows = pl.ds(t, LRU_SEGMENTS, stride=LRU_SEG_PITCH)
        a = a_s[rows, :]
        h = a * h + b_s[rows, :]
        prod = prod * a
        h_s[rows, :] = h
        p_s[rows, :] = prod
        return h, prod

    h_end, p_end = lax.fori_loop(
        0, LRU_SEG_LEN, step,
        (jnp.zeros((LRU_SEGMENTS, LANES), F32), jnp.ones((LRU_SEGMENTS, LANES), F32)), unroll=8)

    seg_id = lax.broadcasted_iota(jnp.int32, (LRU_SEGMENTS, LANES), 0)
    c_in = jnp.zeros((LRU_SEGMENTS, LANES), F32)
    for _ in range(LRU_SEGMENTS - 1):
        c_in = jnp.where(seg_id == 0, 0.0, pltpu.roll(h_end + p_end * c_in, 1, 0))

    for s in range(LRU_SEGMENTS):
        c_s = c_in[s:s + 1, :]

        def gate(k, carry, s=s, c_s=c_s):
            off = pl.multiple_of(k * LRU_CHUNK, LRU_CHUNK)
            src = pl.ds(s * LRU_SEG_PITCH + off, LRU_CHUNK)
            rows = pl.ds(s * LRU_SEG_LEN + off, LRU_CHUNK)
            h = h_s[src, :] + p_s[src, :] * c_s
            o_ref[rows, :] = (h * _silu(g_ref[rows, :])).astype(BF16)
            return carry

        lax.fori_loop(0, chunks_per_seg, gate, 0)


def _lru(proj, conv_w, conv_b, w_gates, b_gates, lam, w_out):
    n_cb = LRU_WIDTH // LANES
    u_col0 = 4 * ATTN_WIDTH // LANES
    g_col0 = u_col0 + n_cb
    return pl.pallas_call(
        _lru_kernel,
        grid=(n_cb,),
        in_specs=[
            pl.BlockSpec((SEQ, LANES), lambda j: (0, u_col0 + j)),
            pl.BlockSpec((SEQ, LANES), lambda j: (0, g_col0 + j)),
            pl.BlockSpec((CONV_WIDTH, LANES), lambda j: (0, j)),
            pl.BlockSpec((1, LANES), lambda j: (0, j)),
            pl.BlockSpec((None, LANES, 2 * LANES), lambda j: (j, 0, 0)),
            pl.BlockSpec((None, 1, 2 * LANES), lambda j: (j, 0, 0)),
            pl.BlockSpec((1, LANES), lambda j: (0, j)),
            pl.BlockSpec((D_MODEL // n_cb, D_MODEL), lambda j: (j, 0)),
        ],
        out_specs=[
            pl.BlockSpec((SEQ, LANES), lambda j: (0, j)),
            pl.BlockSpec((D_MODEL // n_cb, D_MODEL), lambda j: (j, 0)),
        ],
        out_shape=[
            jax.ShapeDtypeStruct((SEQ, LRU_WIDTH), BF16),
            jax.ShapeDtypeStruct((D_MODEL, D_MODEL), BF16),
        ],
        scratch_shapes=[
            pltpu.VMEM((LRU_CHUNK + SUBLANES, LANES), F32),
        ] + [pltpu.VMEM((LRU_SEGMENTS * LRU_SEG_PITCH, LANES), F32)] * 4,
        compiler_params=pltpu.CompilerParams(
            dimension_semantics=("arbitrary",), vmem_limit_bytes=VMEM_LIMIT_BYTES),
        name="rglru",
    )(proj, proj, conv_w, conv_b, w_gates, b_gates, lam, w_out)


def _block_diag_gates(w_rgate, b_rgate, w_igate, b_igate):
    per = LANES // LRU_BLOCK_W
    n_cb = LRU_WIDTH // LANES

    def bd(w):
        w = w.reshape(n_cb, per, LRU_BLOCK_W, LRU_BLOCK_W)
        eye = jnp.eye(per, dtype=w.dtype)
        return jnp.einsum('cpkj,pq->cpkqj', w, eye).reshape(n_cb, LANES, LANES)

    w = jnp.concatenate([bd(w_rgate), bd(w_igate)], axis=-1).astype(BF16)
    b = jnp.concatenate([b_rgate.reshape(n_cb, 1, LANES), b_igate.reshape(n_cb, 1, LANES)], axis=-1)
    return w, b


def _out_kernel(ma_ref, ml_ref, wa_ref, wl_ref, x_ref, gate_ref, fg_ref, o_ref):
    mix = jnp.dot(ma_ref[...], wa_ref[...], preferred_element_type=F32)
    mix = mix + jnp.dot(ml_ref[...], wl_ref[...], preferred_element_type=F32)
    y = x_ref[...] + gate_ref[...] * mix
    var = jnp.mean(y * y, axis=-1, keepdims=True)
    o_ref[...] = y * lax.rsqrt(var + EPS) * fg_ref[...]


def _out_proj(mixed_attn, mixed_lru, w_out_bf16, x2d, mod, final_gain):
    return pl.pallas_call(
        _out_kernel,
        grid=(SEQ // OUT_TM,),
        in_specs=[
            pl.BlockSpec((OUT_TM, ATTN_WIDTH), lambda i: (i, 0)),
            pl.BlockSpec((OUT_TM, LRU_WIDTH), lambda i: (i, 0)),
            pl.BlockSpec((ATTN_WIDTH, D_MODEL), lambda i: (0, 0)),
            pl.BlockSpec((LRU_WIDTH, D_MODEL), lambda i: (1, 0)),
            pl.BlockSpec((OUT_TM, D_MODEL), lambda i: (i, 0)),
            pl.BlockSpec((1, D_MODEL), lambda i: (0, MOD_GATE)),
            pl.BlockSpec((1, D_MODEL), lambda i: (0, 0)),
        ],
        out_specs=pl.BlockSpec((OUT_TM, D_MODEL), lambda i: (i, 0)),
        out_shape=jax.ShapeDtypeStruct((SEQ, D_MODEL), F32),
        compiler_params=pltpu.CompilerParams(
            dimension_semantics=("arbitrary",), vmem_limit_bytes=VMEM_LIMIT_BYTES),
        name="out_proj",
    )(mixed_attn, mixed_lru, w_out_bf16, w_out_bf16, x2d, mod, final_gain)


@jax.jit
def kernel(x, c, norm_gain, w_ada, b_ada, w_in, conv_w, conv_b, w_rgate, b_rgate,
           w_igate, b_igate, lru_lambda, w_out, final_gain):
    assert x.shape == (1, SEQ, D_MODEL) and norm_gain.shape[0] == 1
    x2d = x.reshape(SEQ, D_MODEL)
    mod = _ada_mod(c.reshape(D_MODEL, 1), w_ada[0], b_ada)
    proj = _in_proj(x2d, norm_gain, mod, w_in[0])
    mixed_attn = _attention(proj)
    w_gates, b_gates = _block_diag_gates(w_rgate[0], b_rgate[0], w_igate[0], b_igate[0])
    mixed_lru, w_out_bf16 = _lru(proj, conv_w[0], conv_b, w_gates, b_gates, lru_lambda, w_out[0])
    y = _out_proj(mixed_attn, mixed_lru, w_out_bf16, x2d, mod,
                  final_gain.reshape(1, D_MODEL))
    return y.reshape(1, SEQ, D_MODEL)
```

```python
import functools
import math

import jax
import jax.numpy as jnp
import numpy as np
from jax import lax
from jax.experimental import pallas as pl
from jax.experimental.pallas import tpu as pltpu

D_MODEL = 2048
SEQ = 8192
ATTN_WIDTH = D_MODEL // 2
LRU_WIDTH = D_MODEL - ATTN_WIDTH
HEAD_DIM = 64
N_HEADS = ATTN_WIDTH // HEAD_DIM
LRU_BLOCKS = 16
LRU_BLOCK_W = LRU_WIDTH // LRU_BLOCKS
CONV_WIDTH = 4
LRU_C = 8.0
DILATED_PATTERNS = ((128, 1), (512, 4), (2048, 16))
ATTN_SCALE = 1.0 / math.sqrt(HEAD_DIM)
NEG_INF = -1e30
LOG2E = math.log2(math.e)
EPS = 1e-6
PROJ_WIDTH = 4 * ATTN_WIDTH + 2 * LRU_WIDTH
MOD_SHIFT, MOD_SCALE, MOD_GATE = 0, 1, 2

LANES = 128
SUBLANES = 8
VMEM_LIMIT_BYTES = 56 * 1024 * 1024

ADA_TK = 256
PROJ_TM = 256
PROJ_W_CHUNK = 512
PROJ_W_SLOTS = 2
PROJ_VMEM_LIMIT_BYTES = 60 * 1024 * 1024
NORM_ROWS = 64
ATTN_QB = 128
ATTN_SB = 2048
ATTN_GROUP = 4
DEINT_ROWS = 256
DEINT_RATIO = 4
ATTN_VMEM_LIMIT_BYTES = 60 * 1024 * 1024
HEADS_PER_BLOCK = LANES // HEAD_DIM
COMBINE_ROWS = 256
LRU_CHUNK = 256
LRU_SEGMENTS = SUBLANES
LRU_SEG_LEN = SEQ // LRU_SEGMENTS
LRU_SEG_PITCH = LRU_SEG_LEN + SUBLANES
OUT_TM = 512

F32 = jnp.float32
BF16 = jnp.bfloat16


def _sigmoid(x):
    return 0.5 * jnp.tanh(0.5 * x) + 0.5


def _silu(x):
    return x * _sigmoid(x)


def _ada_kernel(c_ref, w_ref, b_ref, o_ref):
    @pl.when(pl.program_id(0) == 0)
    def _():
        o_ref[...] = b_ref[...]

    c_act = _silu(c_ref[...])
    o_ref[...] += jnp.sum(w_ref[...] * c_act, axis=0, keepdims=True)


def _ada_mod(c_col, w_ada, b_ada):
    n = w_ada.shape[1]
    return pl.pallas_call(
        _ada_kernel,
        grid=(D_MODEL // ADA_TK,),
        in_specs=[
            pl.BlockSpec((ADA_TK, 1), lambda k: (k, 0)),
            pl.BlockSpec((ADA_TK, n), lambda k: (k, 0)),
            pl.BlockSpec((1, n), lambda k: (0, 0)),
        ],
        out_specs=pl.BlockSpec((1, n), lambda k: (0, 0)),
        out_shape=jax.ShapeDtypeStruct((1, n), F32),
        compiler_params=pltpu.CompilerParams(
            dimension_semantics=("arbitrary",), vmem_limit_bytes=VMEM_LIMIT_BYTES),
        name="ada_mod",
    )(c_col, w_ada, b_ada)


def _w_chunk_copy(w_hbm, stage, sem, c):
    slot = c % PROJ_W_SLOTS
    cols = pl.ds(c * PROJ_W_CHUNK, PROJ_W_CHUNK)
    return pltpu.make_async_copy(w_hbm.at[:, cols], stage.at[slot], sem.at[slot])


def _proj_kernel(x_ref, gain_ref, shift_ref, scale_ref, w_hbm, o_ref,
                 w_res, stage, h_even, h_odd, sem):
    s = pl.program_id(0)
    n_chunks = PROJ_WIDTH // PROJ_W_CHUNK

    def normalise(h_ref):
        mul = gain_ref[...] * (1.0 + scale_ref[...])
        shift = shift_ref[...]
        for r0 in range(0, PROJ_TM, NORM_ROWS):
            xv = x_ref[pl.ds(r0, NORM_ROWS), :]
            var = jnp.mean(xv * xv, axis=-1, keepdims=True)
            h = xv * lax.rsqrt(var + EPS) * mul + shift
            h_ref[pl.ds(r0, NORM_ROWS), :] = h.astype(BF16)

    def project(h_ref):
        acc = jnp.dot(h_ref[...], w_res[...], preferred_element_type=F32)
        for b in range(PROJ_WIDTH // LANES):
            o_ref[b] = acc[:, b * LANES:(b + 1) * LANES]

    @pl.when(s == 0)
    def _():
        for c in range(PROJ_W_SLOTS):
            _w_chunk_copy(w_hbm, stage, sem, c).start()
        normalise(h_even)
        for c in range(n_chunks):
            _w_chunk_copy(w_hbm, stage, sem, c).wait()
            cols = pl.ds(c * PROJ_W_CHUNK, PROJ_W_CHUNK)
            w_res[:, cols] = stage[c % PROJ_W_SLOTS].astype(BF16)
            if c + PROJ_W_SLOTS < n_chunks:
                _w_chunk_copy(w_hbm, stage, sem, c + PROJ_W_SLOTS).start()

    is_even = jnp.bitwise_and(s, 1) == 0

    @pl.when(jnp.logical_and(s > 0, is_even))
    def _():
        normalise(h_even)
        project(h_odd)

    @pl.when(jnp.logical_not(is_even))
    def _():
        normalise(h_odd)
        project(h_even)


def _in_proj(x2d, gain, mod, w_in):
    n_tiles = SEQ // PROJ_TM
    return pl.pallas_call(
        _proj_kernel,
        grid=(n_tiles + 1,),
        in_specs=[
            pl.BlockSpec((PROJ_TM, D_MODEL), lambda s: (jnp.minimum(s, n_tiles - 1), 0)),
            pl.BlockSpec((1, D_MODEL), lambda s: (0, 0)),
            pl.BlockSpec((1, D_MODEL), lambda s: (0, MOD_SHIFT)),
            pl.BlockSpec((1, D_MODEL), lambda s: (0, MOD_SCALE)),
            pl.BlockSpec(memory_space=pl.ANY),
        ],
        out_specs=pl.BlockSpec((PROJ_WIDTH // LANES, PROJ_TM, LANES),
                               lambda s: (0, jnp.maximum(s - 1, 0), 0)),
        out_shape=jax.ShapeDtypeStruct((PROJ_WIDTH // LANES, SEQ, LANES), F32),
        scratch_shapes=[
            pltpu.VMEM((D_MODEL, PROJ_WIDTH), BF16),
            pltpu.VMEM((PROJ_W_SLOTS, D_MODEL, PROJ_W_CHUNK), F32),
            pltpu.VMEM((PROJ_TM, D_MODEL), BF16),
            pltpu.VMEM((PROJ_TM, D_MODEL), BF16),
            pltpu.SemaphoreType.DMA((PROJ_W_SLOTS,)),
        ],
        compiler_params=pltpu.CompilerParams(
            dimension_semantics=("arbitrary",), vmem_limit_bytes=PROJ_VMEM_LIMIT_BYTES),
        name="in_proj",
    )(x2d, gain, mod, mod, w_in)


def _rows(start, size, stride):
    return pl.ds(start, size) if stride == 1 else pl.ds(start, size, stride=stride)


def _attn_kernel(q_ref, k_ref, v_ref, g_ref, coef_ref, o_ref,
                 kd, vd, tmp, bias_s, s_s, mx_s, m_s, l_s, acc_s):
    sb = pl.program_id(1)
    lane = lax.broadcasted_iota(jnp.int32, (ATTN_QB, LANES), 1)
    is_h0 = lane < HEAD_DIM

    @pl.when(sb == 0)
    def _():
        for src_ref, dst in ((k_ref, kd), (v_ref, vd)):
            for p, (_, d) in enumerate(DILATED_PATTERNS):
                dst[p, pl.ds(0, ATTN_QB), :] = jnp.zeros((ATTN_QB, LANES), BF16)
                d_prev = DILATED_PATTERNS[p - 1][1] if p else 1
                ratio = d // d_prev
                assert p == 0 and d == 1 or ratio == DEINT_RATIO
                per_residue = SEQ // d // DEINT_ROWS
                keep_f32 = 0 < p < len(DILATED_PATTERNS) - 1
                from_ref = src_ref if p <= 1 else tmp

                def deint(c, carry, p=p, d_prev=d_prev, ratio=ratio, per_residue=per_residue,
                          keep_f32=keep_f32, from_ref=from_ref, dst=dst):
                    r = lax.shift_right_logical(c, per_residue.bit_length() - 1)
                    chunk = jnp.bitwise_and(c, per_residue - 1)
                    r_prev = jnp.bitwise_and(r, d_prev - 1)
                    j = lax.shift_right_logical(r, d_prev.bit_length() - 1)
                    start = r_prev * (SEQ // d_prev) + chunk * (DEINT_ROWS * ratio) + j
                    x = from_ref[_rows(start, DEINT_ROWS, ratio), :]
                    if keep_f32:
                        tmp[pl.ds(pl.multiple_of(c * DEINT_ROWS, DEINT_ROWS), DEINT_ROWS), :] = x
                    rows = pl.ds(pl.multiple_of(ATTN_QB + c * DEINT_ROWS, ATTN_QB), DEINT_ROWS)
                    dst[p, rows, :] = x.astype(BF16)
                    return carry

                lax.fori_loop(0, SEQ // DEINT_ROWS, deint, 0, unroll=4)

        qi = lax.broadcasted_iota(jnp.int32, (ATTN_QB, 2 * ATTN_QB), 0)
        ki = lax.broadcasted_iota(jnp.int32, (ATTN_QB, 2 * ATTN_QB), 1)
        dist = qi + ATTN_QB - ki
        in_band = jnp.logical_and(dist >= 0, dist <= ATTN_QB)
        in_band_cur = jnp.logical_and(in_band, ki >= ATTN_QB)
        dist_f = dist.astype(F32)
        for p in range(len(DILATED_PATTERNS)):
            for h in range(HEADS_PER_BLOCK):
                pen = -(coef_ref[pl.program_id(0), p * HEADS_PER_BLOCK + h] * dist_f)
                rows = pl.ds(h * ATTN_QB, ATTN_QB)
                bias_s[p, 0, rows, :] = jnp.where(in_band, pen, NEG_INF)
                bias_s[p, 1, rows, :] = jnp.where(in_band_cur, pen, NEG_INF)

    ones = jnp.ones((2 * ATTN_QB, LANES), BF16)

    def place(d, ti):
        r, n = ti % d, ti // d
        blk = sb * (ATTN_SB // (ATTN_QB * d)) + n
        q_rows = _rows(r + n * (ATTN_QB * d), ATTN_QB, d)
        window = pl.ds(pl.multiple_of(r * (SEQ // d) + blk * ATTN_QB, ATTN_QB), 2 * ATTN_QB)
        return q_rows, window, blk == 0

    def scores(p, d, g0, half):
        for i in range(ATTN_GROUP):
            q_rows, window, first = place(d, g0 + i)
            q = (q_ref[q_rows, :] * (ATTN_SCALE * LOG2E)).astype(BF16)
            zero = jnp.zeros_like(q)
            qs = jnp.concatenate([jnp.where(is_h0, q, zero), jnp.where(is_h0, zero, q)], axis=0)
            s = lax.dot_general(qs, kd[p, window, :], (((1,), (1,)), ((), ())),
                                preferred_element_type=F32)
            s = s + bias_s[p, first.astype(jnp.int32)]
            slot = half * ATTN_GROUP + i
            s_s[slot] = s
            mx_s[slot] = jnp.broadcast_to(jnp.max(s, axis=1, keepdims=True), (2 * ATTN_QB, LANES))

    def weighted(p, d, g0, half):
        for i in range(ATTN_GROUP):
            q_rows, window, _ = place(d, g0 + i)
            slot = half * ATTN_GROUP + i
            mx = mx_s[slot]
            e = jnp.concatenate(
                [jnp.exp2(s_s[slot, :, :LANES] - mx), jnp.exp2(s_s[slot, :, LANES:] - mx)],
                axis=1).astype(BF16)
            vc = jnp.concatenate([vd[p, window, :], ones], axis=1)
            pv = jnp.dot(e, vc, preferred_element_type=F32)
            acc_s[p, q_rows, :] = jnp.where(is_h0, pv[:ATTN_QB, :LANES], pv[ATTN_QB:, :LANES])
            l_s[p, q_rows, :] = jnp.where(is_h0, pv[:ATTN_QB, LANES:], pv[ATTN_QB:, LANES:])
            m_s[p, q_rows, :] = jnp.where(is_h0, mx[:ATTN_QB], mx[ATTN_QB:])

    groups = [(p, d, g0) for p, (_, d) in enumerate(DILATED_PATTERNS)
              for g0 in range(0, ATTN_SB // ATTN_QB, ATTN_GROUP)]
    scores(*groups[0], 0)
    for k, group in enumerate(groups):
        if k + 1 < len(groups):
            scores(*groups[k + 1], (k + 1) % 2)
        weighted(*group, k % 2)

    def combine(i, carry):
        rows = pl.ds(pl.multiple_of(i * COMBINE_ROWS, COMBINE_ROWS), COMBINE_ROWS)
        m0, m1, m2 = m_s[0, rows, :], m_s[1, rows, :], m_s[2, rows, :]
        mm = jnp.maximum(jnp.maximum(m0, m1), m2)
        e0, e1, e2 = jnp.exp2(m0 - mm), jnp.exp2(m1 - mm), jnp.exp2(m2 - mm)
        num = e0 * acc_s[0, rows, :] + e1 * acc_s[1, rows, :] + e2 * acc_s[2, rows, :]
        den = e0 * l_s[0, rows, :] + e1 * l_s[1, rows, :] + e2 * l_s[2, rows, :]
        o_ref[rows, :] = (num / den * _silu(g_ref[rows, :])).astype(BF16)
        return carry

    lax.fori_loop(0, ATTN_SB // COMBINE_ROWS, combine, 0)


def _alibi_coefs():
    slopes = 2.0 ** (-8.0 * np.arange(1, N_HEADS + 1, dtype=np.float64) / N_HEADS)
    dil = np.array([d for _, d in DILATED_PATTERNS], dtype=np.float64)
    coef = slopes.reshape(-1, HEADS_PER_BLOCK)[:, None, :] * dil[None, :, None] * LOG2E
    return jnp.asarray(coef.reshape(N_HEADS // HEADS_PER_BLOCK, -1), dtype=F32)


def _attention(proj):
    n_hb = ATTN_WIDTH // LANES
    n_p = len(DILATED_PATTERNS)
    return pl.pallas_call(
        _attn_kernel,
        grid=(n_hb, SEQ // ATTN_SB),
        in_specs=[
            pl.BlockSpec((None, ATTN_SB, LANES), lambda h, s: (h, s, 0)),
            pl.BlockSpec((None, SEQ, LANES), lambda h, s: (n_hb + h, 0, 0)),
            pl.BlockSpec((None, SEQ, LANES), lambda h, s: (2 * n_hb + h, 0, 0)),
            pl.BlockSpec((None, ATTN_SB, LANES), lambda h, s: (3 * n_hb + h, s, 0)),
            pl.BlockSpec(memory_space=pltpu.SMEM),
        ],
        out_specs=pl.BlockSpec((ATTN_SB, LANES), lambda h, s: (s, h)),
        out_shape=jax.ShapeDtypeStruct((SEQ, ATTN_WIDTH), BF16),
        scratch_shapes=[
            pltpu.VMEM((n_p, ATTN_QB + SEQ, LANES), BF16),
            pltpu.VMEM((n_p, ATTN_QB + SEQ, LANES), BF16),
            pltpu.VMEM((SEQ, LANES), F32),
            pltpu.VMEM((n_p, 2, HEADS_PER_BLOCK * ATTN_QB, 2 * ATTN_QB), F32),
            pltpu.VMEM((2 * ATTN_GROUP, 2 * ATTN_QB, 2 * ATTN_QB), F32),
            pltpu.VMEM((2 * ATTN_GROUP, 2 * ATTN_QB, LANES), F32),
        ] + [pltpu.VMEM((n_p, ATTN_SB, LANES), F32)] * 3,
        compiler_params=pltpu.CompilerParams(
            dimension_semantics=("arbitrary", "arbitrary"),
            vmem_limit_bytes=ATTN_VMEM_LIMIT_BYTES),
        name="dilated_attn",
    )(proj, proj, proj, proj, _alibi_coefs())


def _lru_kernel(u_ref, g_ref, cw_ref, cb_ref, wg_ref, bg_ref, lam_ref, wout_ref,
                o_ref, wout_bf16_ref, ubuf, a_s, b_s, h_s, p_s):
    wout_bf16_ref[...] = wout_ref[...].astype(BF16)

    lam = lam_ref[...]
    log_sig = jnp.minimum(lam, 0.0) - jnp.log1p(jnp.exp(-jnp.abs(lam)))
    chunks_per_seg = LRU_SEG_LEN // LRU_CHUNK

    ubuf[pl.ds(0, SUBLANES), :] = jnp.zeros((SUBLANES, LANES), F32)

    def chunk(c, carry):
        rows = pl.ds(pl.multiple_of(c * LRU_CHUNK, LRU_CHUNK), LRU_CHUNK)
        ubuf[pl.ds(SUBLANES, LRU_CHUNK), :] = u_ref[rows, :]
        xc = cb_ref[...]
        for j in range(CONV_WIDTH):
            off = SUBLANES - (CONV_WIDTH - 1) + j
            xc = xc + ubuf[pl.ds(off, LRU_CHUNK), :] * cw_ref[pl.ds(j, 1), :]
        ubuf[pl.ds(0, SUBLANES), :] = ubuf[pl.ds(LRU_CHUNK, SUBLANES), :]

        z = jnp.dot(xc.astype(BF16), wg_ref[...], preferred_element_type=F32) + bg_ref[...]
        r = _sigmoid(z[:, :LANES])
        i = _sigmoid(z[:, LANES:])
        log_a = LRU_C * r * log_sig
        a = jnp.exp(log_a)
        th = jnp.tanh(log_a)
        x2 = -2.0 * th / (1.0 - th)
        mult = jnp.where(x2 > 0.0, x2 * lax.rsqrt(x2), 0.0)
        b = mult * (i * xc)

        seg = lax.shift_right_logical(c, chunks_per_seg.bit_length() - 1)
        within = jnp.bitwise_and(c, chunks_per_seg - 1)
        dst = pl.ds(pl.multiple_of(seg * LRU_SEG_PITCH + within * LRU_CHUNK, SUBLANES), LRU_CHUNK)
        a_s[dst, :] = a
        b_s[dst, :] = b
        return carry

    lax.fori_loop(0, SEQ // LRU_CHUNK, chunk, 0, unroll=2)

    def step(t, carry):
        h, prod = carry
        rows = pl.ds(t, LRU_SEGMENTS, stride=LRU_SEG_PITCH)
        a = a_s[rows, :]
        h = a * h + b_s[rows, :]
        prod = prod * a
        h_s[rows, :] = h
        p_s[rows, :] = prod
        return h, prod

    h_end, p_end = lax.fori_loop(
        0, LRU_SEG_LEN, step,
        (jnp.zeros((LRU_SEGMENTS, LANES), F32), jnp.ones((LRU_SEGMENTS, LANES), F32)), unroll=8)

    seg_id = lax.broadcasted_iota(jnp.int32, (LRU_SEGMENTS, LANES), 0)
    c_in = jnp.zeros((LRU_SEGMENTS, LANES), F32)
    for _ in range(LRU_SEGMENTS - 1):
        c_in = jnp.where(seg_id == 0, 0.0, pltpu.roll(h_end + p_end * c_in, 1, 0))

    for s in range(LRU_SEGMENTS):
        c_s = c_in[s:s + 1, :]

        def gate(k, carry, s=s, c_s=c_s):
            off = pl.multiple_of(k * LRU_CHUNK, LRU_CHUNK)
            src = pl.ds(s * LRU_SEG_PITCH + off, LRU_CHUNK)
            rows = pl.ds(s * LRU_SEG_LEN + off, LRU_CHUNK)
            h = h_s[src, :] + p_s[src, :] * c_s
            o_ref[rows, :] = (h * _silu(g_ref[rows, :])).astype(BF16)
            return carry

        lax.fori_loop(0, chunks_per_seg, gate, 0)


def _lru(proj, conv_w, conv_b, w_gates, b_gates, lam, w_out):
    n_cb = LRU_WIDTH // LANES
    u_col0 = 4 * ATTN_WIDTH // LANES
    g_col0 = u_col0 + n_cb
    return pl.pallas_call(
        _lru_kernel,
        grid=(n_cb,),
        in_specs=[
            pl.BlockSpec((None, SEQ, LANES), lambda j: (u_col0 + j, 0, 0)),
            pl.BlockSpec((None, SEQ, LANES), lambda j: (g_col0 + j, 0, 0)),
            pl.BlockSpec((CONV_WIDTH, LANES), lambda j: (0, j)),
            pl.BlockSpec((1, LANES), lambda j: (0, j)),
            pl.BlockSpec((None, LANES, 2 * LANES), lambda j: (j, 0, 0)),
            pl.BlockSpec((None, 1, 2 * LANES), lambda j: (j, 0, 0)),
            pl.BlockSpec((1, LANES), lambda j: (0, j)),
            pl.BlockSpec((D_MODEL // n_cb, D_MODEL), lambda j: (j, 0)),
        ],
        out_specs=[
            pl.BlockSpec((SEQ, LANES), lambda j: (0, j)),
            pl.BlockSpec((D_MODEL // n_cb, D_MODEL), lambda j: (j, 0)),
        ],
        out_shape=[
            jax.ShapeDtypeStruct((SEQ, LRU_WIDTH), BF16),
            jax.ShapeDtypeStruct((D_MODEL, D_MODEL), BF16),
        ],
        scratch_shapes=[
            pltpu.VMEM((LRU_CHUNK + SUBLANES, LANES), F32),
        ] + [pltpu.VMEM((LRU_SEGMENTS * LRU_SEG_PITCH, LANES), F32)] * 4,
        compiler_params=pltpu.CompilerParams(
            dimension_semantics=("arbitrary",), vmem_limit_bytes=VMEM_LIMIT_BYTES),
        name="rglru",
    )(proj, proj, conv_w, conv_b, w_gates, b_gates, lam, w_out)


def _block_diag_gates(w_rgate, b_rgate, w_igate, b_igate):
    per = LANES // LRU_BLOCK_W
    n_cb = LRU_WIDTH // LANES

    def bd(w):
        w = w.reshape(n_cb, per, LRU_BLOCK_W, LRU_BLOCK_W)
        eye = jnp.eye(per, dtype=w.dtype)
        return jnp.einsum('cpkj,pq->cpkqj', w, eye).reshape(n_cb, LANES, LANES)

    w = jnp.concatenate([bd(w_rgate), bd(w_igate)], axis=-1).astype(BF16)
    b = jnp.concatenate([b_rgate.reshape(n_cb, 1, LANES), b_igate.reshape(n_cb, 1, LANES)], axis=-1)
    return w, b


def _out_kernel(ma_ref, ml_ref, wa_ref, wl_ref, x_ref, gate_ref, fg_ref, o_ref):
    mix = jnp.dot(ma_ref[...], wa_ref[...], preferred_element_type=F32)
    mix = mix + jnp.dot(ml_ref[...], wl_ref[...], preferred_element_type=F32)
    y = x_ref[...] + gate_ref[...] * mix
    var = jnp.mean(y * y, axis=-1, keepdims=True)
    o_ref[...] = y * lax.rsqrt(var + EPS) * fg_ref[...]


def _out_proj(mixed_attn, mixed_lru, w_out_bf16, x2d, mod, final_gain):
    return pl.pallas_call(
        _out_kernel,
        grid=(SEQ // OUT_TM,),
        in_specs=[
            pl.BlockSpec((OUT_TM, ATTN_WIDTH), lambda i: (i, 0)),
            pl.BlockSpec((OUT_TM, LRU_WIDTH), lambda i: (i, 0)),
            pl.BlockSpec((ATTN_WIDTH, D_MODEL), lambda i: (0, 0)),
            pl.BlockSpec((LRU_WIDTH, D_MODEL), lambda i: (1, 0)),
            pl.BlockSpec((OUT_TM, D_MODEL), lambda i: (i, 0)),
            pl.BlockSpec((1, D_MODEL), lambda i: (0, MOD_GATE)),
            pl.BlockSpec((1, D_MODEL), lambda i: (0, 0)),
        ],
        out_specs=pl.BlockSpec((OUT_TM, D_MODEL), lambda i: (i, 0)),
        out_shape=jax.ShapeDtypeStruct((SEQ, D_MODEL), F32),
        compiler_params=pltpu.CompilerParams(
            dimension_semantics=("arbitrary",), vmem_limit_bytes=VMEM_LIMIT_BYTES),
        name="out_proj",
    )(mixed_attn, mixed_lru, w_out_bf16, w_out_bf16, x2d, mod, final_gain)


@jax.jit
def kernel(x, c, norm_gain, w_ada, b_ada, w_in, conv_w, conv_b, w_rgate, b_rgate,
           w_igate, b_igate, lru_lambda, w_out, final_gain):
    assert x.shape == (1, SEQ, D_MODEL) and norm_gain.shape[0] == 1
    x2d = x.reshape(SEQ, D_MODEL)
    mod = _ada_mod(c.reshape(D_MODEL, 1), w_ada[0], b_ada)
    proj = _in_proj(x2d, norm_gain, mod, w_in[0])
    mixed_attn = _attention(proj)
    w_gates, b_gates = _block_diag_gates(w_rgate[0], b_rgate[0], w_igate[0], b_igate[0])
    mixed_lru, w_out_bf16 = _lru(proj, conv_w[0], conv_b, w_gates, b_gates, lru_lambda, w_out[0])
    y = _out_proj(mixed_attn, mixed_lru, w_out_bf16, x2d, mod,
                  final_gain.reshape(1, D_MODEL))
    return y.reshape(1, SEQ, D_MODEL)
```

```python
import functools
import math

import jax
import jax.numpy as jnp
import numpy as np
from jax import lax
from jax.experimental import pallas as pl
from jax.experimental.pallas import tpu as pltpu

D_MODEL = 2048
SEQ = 8192
ATTN_WIDTH = D_MODEL // 2
LRU_WIDTH = D_MODEL - ATTN_WIDTH
HEAD_DIM = 64
N_HEADS = ATTN_WIDTH // HEAD_DIM
LRU_BLOCKS = 16
LRU_BLOCK_W = LRU_WIDTH // LRU_BLOCKS
CONV_WIDTH = 4
LRU_C = 8.0
DILATED_PATTERNS = ((128, 1), (512, 4), (2048, 16))
ATTN_SCALE = 1.0 / math.sqrt(HEAD_DIM)
NEG_INF = -1e30
LOG2E = math.log2(math.e)
EPS = 1e-6
PROJ_WIDTH = 4 * ATTN_WIDTH + 2 * LRU_WIDTH
MOD_SHIFT, MOD_SCALE, MOD_GATE = 0, 1, 2

LANES = 128
SUBLANES = 8
VMEM_LIMIT_BYTES = 56 * 1024 * 1024

ADA_TK = 256
PROJ_TM = 256
PROJ_W_CHUNK = 512
PROJ_W_SLOTS = 2
PROJ_VMEM_LIMIT_BYTES = 60 * 1024 * 1024
NORM_ROWS = 64
ATTN_QB = 128
ATTN_SB = 2048
ATTN_GROUP = 4
DEINT_ROWS = 256
DEINT_RATIO = 4
ATTN_VMEM_LIMIT_BYTES = 60 * 1024 * 1024
HEADS_PER_BLOCK = LANES // HEAD_DIM
COMBINE_ROWS = 256
LRU_CHUNK = 256
LRU_SEGMENTS = SUBLANES
LRU_SEG_LEN = SEQ // LRU_SEGMENTS
LRU_SEG_PITCH = LRU_SEG_LEN + SUBLANES
OUT_TM = 512

F32 = jnp.float32
BF16 = jnp.bfloat16


def _silu(x):
    h = 0.5 * x
    return h + h * jnp.tanh(h)


def _ada_kernel(c_ref, w_ref, b_ref, o_ref):
    @pl.when(pl.program_id(0) == 0)
    def _():
        o_ref[...] = b_ref[...]

    c_act = _silu(c_ref[...])
    o_ref[...] += jnp.sum(w_ref[...] * c_act, axis=0, keepdims=True)


def _ada_mod(c_col, w_ada, b_ada):
    n = w_ada.shape[1]
    return pl.pallas_call(
        _ada_kernel,
        grid=(D_MODEL // ADA_TK,),
        in_specs=[
            pl.BlockSpec((ADA_TK, 1), lambda k: (k, 0)),
            pl.BlockSpec((ADA_TK, n), lambda k: (k, 0)),
            pl.BlockSpec((1, n), lambda k: (0, 0)),
        ],
        out_specs=pl.BlockSpec((1, n), lambda k: (0, 0)),
        out_shape=jax.ShapeDtypeStruct((1, n), F32),
        compiler_params=pltpu.CompilerParams(
            dimension_semantics=("arbitrary",), vmem_limit_bytes=VMEM_LIMIT_BYTES),
        name="ada_mod",
    )(c_col, w_ada, b_ada)


def _w_chunk_copy(w_hbm, stage, sem, c):
    slot = c % PROJ_W_SLOTS
    cols = pl.ds(c * PROJ_W_CHUNK, PROJ_W_CHUNK)
    return pltpu.make_async_copy(w_hbm.at[:, cols], stage.at[slot], sem.at[slot])


def _proj_kernel(x_ref, gain_ref, shift_ref, scale_ref, w_hbm, o_ref,
                 w_res, stage, h_even, h_odd, sem):
    s = pl.program_id(0)
    n_chunks = PROJ_WIDTH // PROJ_W_CHUNK

    def normalise(h_ref):
        mul = gain_ref[...] * (1.0 + scale_ref[...])
        shift = shift_ref[...]
        for r0 in range(0, PROJ_TM, NORM_ROWS):
            xv = x_ref[pl.ds(r0, NORM_ROWS), :]
            var = jnp.mean(xv * xv, axis=-1, keepdims=True)
            h = xv * lax.rsqrt(var + EPS) * mul + shift
            h_ref[pl.ds(r0, NORM_ROWS), :] = h.astype(BF16)

    def project(h_ref):
        acc = jnp.dot(h_ref[...], w_res[...], preferred_element_type=F32)
        for b in range(PROJ_WIDTH // LANES):
            o_ref[b] = acc[:, b * LANES:(b + 1) * LANES]

    @pl.when(s == 0)
    def _():
        for c in range(PROJ_W_SLOTS):
            _w_chunk_copy(w_hbm, stage, sem, c).start()
        normalise(h_even)
        for c in range(n_chunks):
            _w_chunk_copy(w_hbm, stage, sem, c).wait()
            cols = pl.ds(c * PROJ_W_CHUNK, PROJ_W_CHUNK)
            w_res[:, cols] = stage[c % PROJ_W_SLOTS].astype(BF16)
            if c + PROJ_W_SLOTS < n_chunks:
                _w_chunk_copy(w_hbm, stage, sem, c + PROJ_W_SLOTS).start()

    is_even = jnp.bitwise_and(s, 1) == 0

    @pl.when(jnp.logical_and(s > 0, is_even))
    def _():
        normalise(h_even)
        project(h_odd)

    @pl.when(jnp.logical_not(is_even))
    def _():
        normalise(h_odd)
        project(h_even)


def _in_proj(x2d, gain, mod, w_in):
    n_tiles = SEQ // PROJ_TM
    return pl.pallas_call(
        _proj_kernel,
        grid=(n_tiles + 1,),
        in_specs=[
            pl.BlockSpec((PROJ_TM, D_MODEL), lambda s: (jnp.minimum(s, n_tiles - 1), 0)),
            pl.BlockSpec((1, D_MODEL), lambda s: (0, 0)),
            pl.BlockSpec((1, D_MODEL), lambda s: (0, MOD_SHIFT)),
            pl.BlockSpec((1, D_MODEL), lambda s: (0, MOD_SCALE)),
            pl.BlockSpec(memory_space=pl.ANY),
        ],
        out_specs=pl.BlockSpec((PROJ_WIDTH // LANES, PROJ_TM, LANES),
                               lambda s: (0, jnp.maximum(s - 1, 0), 0)),
        out_shape=jax.ShapeDtypeStruct((PROJ_WIDTH // LANES, SEQ, LANES), F32),
        scratch_shapes=[
            pltpu.VMEM((D_MODEL, PROJ_WIDTH), BF16),
            pltpu.VMEM((PROJ_W_SLOTS, D_MODEL, PROJ_W_CHUNK), F32),
            pltpu.VMEM((PROJ_TM, D_MODEL), BF16),
            pltpu.VMEM((PROJ_TM, D_MODEL), BF16),
            pltpu.SemaphoreType.DMA((PROJ_W_SLOTS,)),
        ],
        compiler_params=pltpu.CompilerParams(
            dimension_semantics=("arbitrary",), vmem_limit_bytes=PROJ_VMEM_LIMIT_BYTES),
        name="in_proj",
    )(x2d, gain, mod, mod, w_in)


def _rows(start, size, stride):
    return pl.ds(start, size) if stride == 1 else pl.ds(start, size, stride=stride)


def _attn_kernel(q_ref, k_ref, v_ref, g_ref, coef_ref, o_ref,
                 kd, vd, tmp, bias_s, s_s, mx_s, m_s, l_s, acc_s):
    sb = pl.program_id(1)
    lane = lax.broadcasted_iota(jnp.int32, (ATTN_QB, LANES), 1)
    is_h0 = lane < HEAD_DIM

    @pl.when(sb == 0)
    def _():
        for src_ref, dst in ((k_ref, kd), (v_ref, vd)):
            for p, (_, d) in enumerate(DILATED_PATTERNS):
                dst[p, pl.ds(0, ATTN_QB), :] = jnp.zeros((ATTN_QB, LANES), BF16)
                d_prev = DILATED_PATTERNS[p - 1][1] if p else 1
                ratio = d // d_prev
                assert p == 0 and d == 1 or ratio == DEINT_RATIO
                per_residue = SEQ // d // DEINT_ROWS
                keep_f32 = 0 < p < len(DILATED_PATTERNS) - 1
                from_ref = src_ref if p <= 1 else tmp

                def deint(c, carry, p=p, d_prev=d_prev, ratio=ratio, per_residue=per_residue,
                          keep_f32=keep_f32, from_ref=from_ref, dst=dst):
                    r = lax.shift_right_logical(c, per_residue.bit_length() - 1)
                    chunk = jnp.bitwise_and(c, per_residue - 1)
                    r_prev = jnp.bitwise_and(r, d_prev - 1)
                    j = lax.shift_right_logical(r, d_prev.bit_length() - 1)
                    start = r_prev * (SEQ // d_prev) + chunk * (DEINT_ROWS * ratio) + j
                    x = from_ref[_rows(start, DEINT_ROWS, ratio), :]
                    if keep_f32:
                        tmp[pl.ds(pl.multiple_of(c * DEINT_ROWS, DEINT_ROWS), DEINT_ROWS), :] = x
                    rows = pl.ds(pl.multiple_of(ATTN_QB + c * DEINT_ROWS, ATTN_QB), DEINT_ROWS)
                    dst[p, rows, :] = x.astype(BF16)
                    return carry

                lax.fori_loop(0, SEQ // DEINT_ROWS, deint, 0, unroll=4)

        qi = lax.broadcasted_iota(jnp.int32, (ATTN_QB, 2 * ATTN_QB), 0)
        ki = lax.broadcasted_iota(jnp.int32, (ATTN_QB, 2 * ATTN_QB), 1)
        dist = qi + ATTN_QB - ki
        in_band = jnp.logical_and(dist >= 0, dist <= ATTN_QB)
        in_band_cur = jnp.logical_and(in_band, ki >= ATTN_QB)
        dist_f = dist.astype(F32)
        for p in range(len(DILATED_PATTERNS)):
            for h in range(HEADS_PER_BLOCK):
                pen = -(coef_ref[pl.program_id(0), p * HEADS_PER_BLOCK + h] * dist_f)
                rows = pl.ds(h * ATTN_QB, ATTN_QB)
                bias_s[p, 0, rows, :] = jnp.where(in_band, pen, NEG_INF)
                bias_s[p, 1, rows, :] = jnp.where(in_band_cur, pen, NEG_INF)

    key_lane = lax.broadcasted_iota(jnp.int32, (2 * ATTN_QB, LANES), 1)
    is_h0_keys = key_lane < HEAD_DIM
    ones_h0 = jnp.where(is_h0_keys, 1.0, 0.0).astype(BF16)
    ones_h1 = jnp.where(is_h0_keys, 0.0, 1.0).astype(BF16)

    def place(d, ti):
        r, n = ti % d, ti // d
        blk = sb * (ATTN_SB // (ATTN_QB * d)) + n
        q_rows = _rows(r + n * (ATTN_QB * d), ATTN_QB, d)
        window = pl.ds(pl.multiple_of(r * (SEQ // d) + blk * ATTN_QB, ATTN_QB), 2 * ATTN_QB)
        return q_rows, window, blk == 0

    def scores(p, d, g0, half):
        for i in range(ATTN_GROUP):
            q_rows, window, first = place(d, g0 + i)
            q = (q_ref[q_rows, :] * (ATTN_SCALE * LOG2E)).astype(BF16)
            zero = jnp.zeros_like(q)
            qs = jnp.concatenate([jnp.where(is_h0, q, zero), jnp.where(is_h0, zero, q)], axis=0)
            s = lax.dot_general(qs, kd[p, window, :], (((1,), (1,)), ((), ())),
                                preferred_element_type=F32)
            s = s + bias_s[p, first.astype(jnp.int32)]
            slot = half * ATTN_GROUP + i
            s_s[slot] = s
            mx_s[slot] = jnp.broadcast_to(jnp.max(s, axis=1, keepdims=True), (2 * ATTN_QB, LANES))

    def weighted(p, d, g0, half):
        for i in range(ATTN_GROUP):
            q_rows, window, _ = place(d, g0 + i)
            slot = half * ATTN_GROUP + i
            mx = mx_s[slot]
            e = jnp.concatenate(
                [jnp.exp2(s_s[slot, :, :LANES] - mx), jnp.exp2(s_s[slot, :, LANES:] - mx)],
                axis=1).astype(BF16)
            e2 = jnp.concatenate([e[:ATTN_QB], e[ATTN_QB:]], axis=1)
            v = vd[p, window, :]
            vz = jnp.zeros_like(v)
            rhs = jnp.concatenate(
                [jnp.concatenate([jnp.where(is_h0_keys, v, vz), ones_h0], axis=1),
                 jnp.concatenate([jnp.where(is_h0_keys, vz, v), ones_h1], axis=1)], axis=0)
            pv = jnp.dot(e2, rhs, preferred_element_type=F32)
            acc_s[p, q_rows, :] = pv[:, :LANES]
            l_s[p, q_rows, :] = pv[:, LANES:]
            m_s[p, q_rows, :] = jnp.where(is_h0, mx[:ATTN_QB], mx[ATTN_QB:])

    groups = [(p, d, g0) for p, (_, d) in enumerate(DILATED_PATTERNS)
              for g0 in range(0, ATTN_SB // ATTN_QB, ATTN_GROUP)]
    scores(*groups[0], 0)
    for k, group in enumerate(groups):
        if k + 1 < len(groups):
            scores(*groups[k + 1], (k + 1) % 2)
        weighted(*group, k % 2)

    def combine(i, carry):
        rows = pl.ds(pl.multiple_of(i * COMBINE_ROWS, COMBINE_ROWS), COMBINE_ROWS)
        m0, m1, m2 = m_s[0, rows, :], m_s[1, rows, :], m_s[2, rows, :]
        mm = jnp.maximum(jnp.maximum(m0, m1), m2)
        e0, e1, e2 = jnp.exp2(m0 - mm), jnp.exp2(m1 - mm), jnp.exp2(m2 - mm)
        num = e0 * acc_s[0, rows, :] + e1 * acc_s[1, rows, :] + e2 * acc_s[2, rows, :]
        den = e0 * l_s[0, rows, :] + e1 * l_s[1, rows, :] + e2 * l_s[2, rows, :]
        o_ref[rows, :] = (num / den * _silu(g_ref[rows, :])).astype(BF16)
        return carry

    lax.fori_loop(0, ATTN_SB // COMBINE_ROWS, combine, 0, unroll=2)


def _alibi_coefs():
    slopes = 2.0 ** (-8.0 * np.arange(1, N_HEADS + 1, dtype=np.float64) / N_HEADS)
    dil = np.array([d for _, d in DILATED_PATTERNS], dtype=np.float64)
    coef = slopes.reshape(-1, HEADS_PER_BLOCK)[:, None, :] * dil[None, :, None] * LOG2E
    return jnp.asarray(coef.reshape(N_HEADS // HEADS_PER_BLOCK, -1), dtype=F32)


def _attention(proj):
    n_hb = ATTN_WIDTH // LANES
    n_p = len(DILATED_PATTERNS)
    return pl.pallas_call(
        _attn_kernel,
        grid=(n_hb, SEQ // ATTN_SB),
        in_specs=[
            pl.BlockSpec((None, ATTN_SB, LANES), lambda h, s: (h, s, 0)),
            pl.BlockSpec((None, SEQ, LANES), lambda h, s: (n_hb + h, 0, 0)),
            pl.BlockSpec((None, SEQ, LANES), lambda h, s: (2 * n_hb + h, 0, 0)),
            pl.BlockSpec((None, ATTN_SB, LANES), lambda h, s: (3 * n_hb + h, s, 0)),
            pl.BlockSpec(memory_space=pltpu.SMEM),
        ],
        out_specs=pl.BlockSpec((ATTN_SB, LANES), lambda h, s: (s, h)),
        out_shape=jax.ShapeDtypeStruct((SEQ, ATTN_WIDTH), BF16),
        scratch_shapes=[
            pltpu.VMEM((n_p, ATTN_QB + SEQ, LANES), BF16),
            pltpu.VMEM((n_p, ATTN_QB + SEQ, LANES), BF16),
            pltpu.VMEM((SEQ, LANES), F32),
            pltpu.VMEM((n_p, 2, HEADS_PER_BLOCK * ATTN_QB, 2 * ATTN_QB), F32),
            pltpu.VMEM((2 * ATTN_GROUP, 2 * ATTN_QB, 2 * ATTN_QB), F32),
            pltpu.VMEM((2 * ATTN_GROUP, 2 * ATTN_QB, LANES), F32),
        ] + [pltpu.VMEM((n_p, ATTN_SB, LANES), F32)] * 3,
        compiler_params=pltpu.CompilerParams(
            dimension_semantics=("arbitrary", "arbitrary"),
            vmem_limit_bytes=ATTN_VMEM_LIMIT_BYTES),
        name="dilated_attn",
    )(proj, proj, proj, proj, _alibi_coefs())


def _lru_kernel(u_ref, g_ref, cw_ref, cb_ref, wg_ref, bg_ref, lam_ref, wout_ref,
                o_ref, wout_bf16_ref, ubuf, a_s, b_s, h_s, p_s):
    wout_bf16_ref[...] = wout_ref[...].astype(BF16)

    lam = lam_ref[...]
    log_sig = jnp.minimum(lam, 0.0) - jnp.log1p(jnp.exp(-jnp.abs(lam)))
    half_c_log_sig = (0.5 * LRU_C) * log_sig
    chunks_per_seg = LRU_SEG_LEN // LRU_CHUNK

    ubuf[pl.ds(0, SUBLANES), :] = jnp.zeros((SUBLANES, LANES), F32)

    def chunk(c, carry):
        rows = pl.ds(pl.multiple_of(c * LRU_CHUNK, LRU_CHUNK), LRU_CHUNK)
        ubuf[pl.ds(SUBLANES, LRU_CHUNK), :] = u_ref[rows, :]
        xc = cb_ref[...]
        for j in range(CONV_WIDTH):
            off = SUBLANES - (CONV_WIDTH - 1) + j
            xc = xc + ubuf[pl.ds(off, LRU_CHUNK), :] * cw_ref[pl.ds(j, 1), :]
        ubuf[pl.ds(0, SUBLANES), :] = ubuf[pl.ds(LRU_CHUNK, SUBLANES), :]

        z = jnp.dot(xc.astype(BF16), wg_ref[...], preferred_element_type=F32) + bg_ref[...]
        i = 0.5 * jnp.tanh(z[:, LANES:]) + 0.5
        log_a = half_c_log_sig * jnp.tanh(z[:, :LANES]) + half_c_log_sig
        a = jnp.exp(log_a)
        th = jnp.tanh(log_a)
        x2 = -2.0 * th / (1.0 - th)
        mult = jnp.where(x2 > 0.0, x2 * lax.rsqrt(x2), 0.0)
        b = mult * (i * xc)

        seg = lax.shift_right_logical(c, chunks_per_seg.bit_length() - 1)
        within = jnp.bitwise_and(c, chunks_per_seg - 1)
        dst = pl.ds(pl.multiple_of(seg * LRU_SEG_PITCH + within * LRU_CHUNK, SUBLANES), LRU_CHUNK)
        a_s[dst, :] = a
        b_s[dst, :] = b
        return carry

    lax.fori_loop(0, SEQ // LRU_CHUNK, chunk, 0, unroll=4)

    def step(t, carry):
        h, prod = carry
        rows = pl.ds(t, LRU_SEGMENTS, stride=LRU_SEG_PITCH)
        a = a_s[rows, :]
        h = a * h + b_s[rows, :]
        prod = prod * a
        h_s[rows, :] = h
        p_s[rows, :] = prod
        return h, prod

    h_end, p_end = lax.fori_loop(
        0, LRU_SEG_LEN, step,
        (jnp.zeros((LRU_SEGMENTS, LANES), F32), jnp.ones((LRU_SEGMENTS, LANES), F32)), unroll=8)

    seg_id = lax.broadcasted_iota(jnp.int32, (LRU_SEGMENTS, LANES), 0)
    c_in = jnp.zeros((LRU_SEGMENTS, LANES), F32)
    for _ in range(LRU_SEGMENTS - 1):
        c_in = jnp.where(seg_id == 0, 0.0, pltpu.roll(h_end + p_end * c_in, 1, 0))

    for s in range(LRU_SEGMENTS):
        c_s = c_in[s:s + 1, :]

        def gate(k, carry, s=s, c_s=c_s):
            off = pl.multiple_of(k * LRU_CHUNK, LRU_CHUNK)
            src = pl.ds(s * LRU_SEG_PITCH + off, LRU_CHUNK)
            rows = pl.ds(s * LRU_SEG_LEN + off, LRU_CHUNK)
            h = h_s[src, :] + p_s[src, :] * c_s
            o_ref[rows, :] = (h * _silu(g_ref[rows, :])).astype(BF16)
            return carry

        lax.fori_loop(0, chunks_per_seg, gate, 0)


def _lru(proj, conv_w, conv_b, w_gates, b_gates, lam, w_out):
    n_cb = LRU_WIDTH // LANES
    u_col0 = 4 * ATTN_WIDTH // LANES
    g_col0 = u_col0 + n_cb
    return pl.pallas_call(
        _lru_kernel,
        grid=(n_cb,),
        in_specs=[
            pl.BlockSpec((None, SEQ, LANES), lambda j: (u_col0 + j, 0, 0)),
            pl.BlockSpec((None, SEQ, LANES), lambda j: (g_col0 + j, 0, 0)),
            pl.BlockSpec((CONV_WIDTH, LANES), lambda j: (0, j)),
            pl.BlockSpec((1, LANES), lambda j: (0, j)),
            pl.BlockSpec((None, LANES, 2 * LANES), lambda j: (j, 0, 0)),
            pl.BlockSpec((None, 1, 2 * LANES), lambda j: (j, 0, 0)),
            pl.BlockSpec((1, LANES), lambda j: (0, j)),
            pl.BlockSpec((D_MODEL // n_cb, D_MODEL), lambda j: (j, 0)),
        ],
        out_specs=[
            pl.BlockSpec((SEQ, LANES), lambda j: (0, j)),
            pl.BlockSpec((D_MODEL // n_cb, D_MODEL), lambda j: (j, 0)),
        ],
        out_shape=[
            jax.ShapeDtypeStruct((SEQ, LRU_WIDTH), BF16),
            jax.ShapeDtypeStruct((D_MODEL, D_MODEL), BF16),
        ],
        scratch_shapes=[
            pltpu.VMEM((LRU_CHUNK + SUBLANES, LANES), F32),
        ] + [pltpu.VMEM((LRU_SEGMENTS * LRU_SEG_PITCH, LANES), F32)] * 4,
        compiler_params=pltpu.CompilerParams(
            dimension_semantics=("arbitrary",), vmem_limit_bytes=VMEM_LIMIT_BYTES),
        name="rglru",
    )(proj, proj, conv_w, conv_b, w_gates, b_gates, lam, w_out)


def _block_diag_gates(w_rgate, b_rgate, w_igate, b_igate):
    per = LANES // LRU_BLOCK_W
    n_cb = LRU_WIDTH // LANES

    def bd(w):
        w = w.reshape(n_cb, per, LRU_BLOCK_W, LRU_BLOCK_W)
        eye = jnp.eye(per, dtype=w.dtype)
        return jnp.einsum('cpkj,pq->cpkqj', w, eye).reshape(n_cb, LANES, LANES)

    w = (0.5 * jnp.concatenate([bd(w_rgate), bd(w_igate)], axis=-1)).astype(BF16)
    b = 0.5 * jnp.concatenate(
        [b_rgate.reshape(n_cb, 1, LANES), b_igate.reshape(n_cb, 1, LANES)], axis=-1)
    return w, b


def _out_kernel(ma_ref, ml_ref, wa_ref, wl_ref, x_ref, gate_ref, fg_ref, o_ref):
    mix = jnp.dot(ma_ref[...], wa_ref[...], preferred_element_type=F32)
    mix = mix + jnp.dot(ml_ref[...], wl_ref[...], preferred_element_type=F32)
    y = x_ref[...] + gate_ref[...] * mix
    var = jnp.mean(y * y, axis=-1, keepdims=True)
    o_ref[...] = y * lax.rsqrt(var + EPS) * fg_ref[...]


def _out_proj(mixed_attn, mixed_lru, w_out_bf16, x2d, mod, final_gain):
    return pl.pallas_call(
        _out_kernel,
        grid=(SEQ // OUT_TM,),
        in_specs=[
            pl.BlockSpec((OUT_TM, ATTN_WIDTH), lambda i: (i, 0)),
            pl.BlockSpec((OUT_TM, LRU_WIDTH), lambda i: (i, 0)),
            pl.BlockSpec((ATTN_WIDTH, D_MODEL), lambda i: (0, 0)),
            pl.BlockSpec((LRU_WIDTH, D_MODEL), lambda i: (1, 0)),
            pl.BlockSpec((OUT_TM, D_MODEL), lambda i: (i, 0)),
            pl.BlockSpec((1, D_MODEL), lambda i: (0, MOD_GATE)),
            pl.BlockSpec((1, D_MODEL), lambda i: (0, 0)),
        ],
        out_specs=pl.BlockSpec((OUT_TM, D_MODEL), lambda i: (i, 0)),
        out_shape=jax.ShapeDtypeStruct((SEQ, D_MODEL), F32),
        compiler_params=pltpu.CompilerParams(
            dimension_semantics=("arbitrary",), vmem_limit_bytes=VMEM_LIMIT_BYTES),
        name="out_proj",
    )(mixed_attn, mixed_lru, w_out_bf16, w_out_bf16, x2d, mod, final_gain)


@jax.jit
def kernel(x, c, norm_gain, w_ada, b_ada, w_in, conv_w, conv_b, w_rgate, b_rgate,
           w_igate, b_igate, lru_lambda, w_out, final_gain):
    assert x.shape == (1, SEQ, D_MODEL) and norm_gain.shape[0] == 1
    x2d = x.reshape(SEQ, D_MODEL)
    mod = _ada_mod(c.reshape(D_MODEL, 1), w_ada[0], b_ada)
    proj = _in_proj(x2d, norm_gain, mod, w_in[0])
    mixed_attn = _attention(proj)
    w_gates, b_gates = _block_diag_gates(w_rgate[0], b_rgate[0], w_igate[0], b_igate[0])
    mixed_lru, w_out_bf16 = _lru(proj, conv_w[0], conv_b, w_gates, b_gates, lru_lambda, w_out[0])
    y = _out_proj(mixed_attn, mixed_lru, w_out_bf16, x2d, mod,
                  final_gain.reshape(1, D_MODEL))
    return y.reshape(1, SEQ, D_MODEL)
```

```python
import functools
import math

import jax
import jax.numpy as jnp
import numpy as np
from jax import lax
from jax.experimental import pallas as pl
from jax.experimental.pallas import tpu as pltpu

D_MODEL = 2048
SEQ = 8192
ATTN_WIDTH = D_MODEL // 2
LRU_WIDTH = D_MODEL - ATTN_WIDTH
HEAD_DIM = 64
N_HEADS = ATTN_WIDTH // HEAD_DIM
LRU_BLOCKS = 16
LRU_BLOCK_W = LRU_WIDTH // LRU_BLOCKS
CONV_WIDTH = 4
LRU_C = 8.0
DILATED_PATTERNS = ((128, 1), (512, 4), (2048, 16))
ATTN_SCALE = 1.0 / math.sqrt(HEAD_DIM)
NEG_INF = -1e30
LOG2E = math.log2(math.e)
EPS = 1e-6
PROJ_WIDTH = 4 * ATTN_WIDTH + 2 * LRU_WIDTH
MOD_SHIFT, MOD_SCALE, MOD_GATE = 0, 1, 2

LANES = 128
SUBLANES = 8
VMEM_LIMIT_BYTES = 56 * 1024 * 1024

ADA_TK = 256
PROJ_TM = 256
PROJ_W_CHUNK = 512
PROJ_W_SLOTS = 2
PROJ_VMEM_LIMIT_BYTES = 60 * 1024 * 1024
NORM_ROWS = 64
ATTN_QB = 128
ATTN_SB = 2048
ATTN_GROUP = 4
DEINT_ROWS = 256
DEINT_RATIO = 4
ATTN_VMEM_LIMIT_BYTES = 60 * 1024 * 1024
HEADS_PER_BLOCK = LANES // HEAD_DIM
COMBINE_ROWS = 256
LRU_CHUNK = 256
LRU_SEGMENTS = SUBLANES
LRU_SEG_LEN = SEQ // LRU_SEGMENTS
LRU_SEG_PITCH = LRU_SEG_LEN + SUBLANES
OUT_TM = 512

F32 = jnp.float32
BF16 = jnp.bfloat16


def _silu(x):
    h = 0.5 * x
    return h + h * jnp.tanh(h)


def _ada_kernel(c_ref, w_ref, b_ref, o_ref):
    @pl.when(pl.program_id(0) == 0)
    def _():
        o_ref[...] = b_ref[...]

    c_act = _silu(c_ref[...])
    o_ref[...] += jnp.sum(w_ref[...] * c_act, axis=0, keepdims=True)


def _ada_mod(c_col, w_ada, b_ada):
    n = w_ada.shape[1]
    return pl.pallas_call(
        _ada_kernel,
        grid=(D_MODEL // ADA_TK,),
        in_specs=[
            pl.BlockSpec((ADA_TK, 1), lambda k: (k, 0)),
            pl.BlockSpec((ADA_TK, n), lambda k: (k, 0)),
            pl.BlockSpec((1, n), lambda k: (0, 0)),
        ],
        out_specs=pl.BlockSpec((1, n), lambda k: (0, 0)),
        out_shape=jax.ShapeDtypeStruct((1, n), F32),
        compiler_params=pltpu.CompilerParams(
            dimension_semantics=("arbitrary",), vmem_limit_bytes=VMEM_LIMIT_BYTES),
        name="ada_mod",
    )(c_col, w_ada, b_ada)


def _w_chunk_copy(w_hbm, stage, sem, c):
    slot = c % PROJ_W_SLOTS
    cols = pl.ds(c * PROJ_W_CHUNK, PROJ_W_CHUNK)
    return pltpu.make_async_copy(w_hbm.at[:, cols], stage.at[slot], sem.at[slot])


def _proj_kernel(x_ref, gain_ref, shift_ref, scale_ref, w_hbm, o_ref,
                 w_res, stage, h_even, h_odd, sem):
    s = pl.program_id(0)
    n_chunks = PROJ_WIDTH // PROJ_W_CHUNK

    def normalise(h_ref):
        mul = gain_ref[...] * (1.0 + scale_ref[...])
        shift = shift_ref[...]
        for r0 in range(0, PROJ_TM, NORM_ROWS):
            xv = x_ref[pl.ds(r0, NORM_ROWS), :]
            var = jnp.mean(xv * xv, axis=-1, keepdims=True)
            h = xv * lax.rsqrt(var + EPS) * mul + shift
            h_ref[pl.ds(r0, NORM_ROWS), :] = h.astype(BF16)

    def project(h_ref):
        acc = jnp.dot(h_ref[...], w_res[...], preferred_element_type=F32)
        for b in range(PROJ_WIDTH // LANES):
            o_ref[b] = acc[:, b * LANES:(b + 1) * LANES]

    @pl.when(s == 0)
    def _():
        for c in range(PROJ_W_SLOTS):
            _w_chunk_copy(w_hbm, stage, sem, c).start()
        normalise(h_even)
        for c in range(n_chunks):
            _w_chunk_copy(w_hbm, stage, sem, c).wait()
            cols = pl.ds(c * PROJ_W_CHUNK, PROJ_W_CHUNK)
            w_res[:, cols] = stage[c % PROJ_W_SLOTS].astype(BF16)
            if c + PROJ_W_SLOTS < n_chunks:
                _w_chunk_copy(w_hbm, stage, sem, c + PROJ_W_SLOTS).start()

    is_even = jnp.bitwise_and(s, 1) == 0

    @pl.when(jnp.logical_and(s > 0, is_even))
    def _():
        normalise(h_even)
        project(h_odd)

    @pl.when(jnp.logical_not(is_even))
    def _():
        normalise(h_odd)
        project(h_even)


def _in_proj(x2d, gain, mod, w_in):
    n_tiles = SEQ // PROJ_TM
    return pl.pallas_call(
        _proj_kernel,
        grid=(n_tiles + 1,),
        in_specs=[
            pl.BlockSpec((PROJ_TM, D_MODEL), lambda s: (jnp.minimum(s, n_tiles - 1), 0)),
            pl.BlockSpec((1, D_MODEL), lambda s: (0, 0)),
            pl.BlockSpec((1, D_MODEL), lambda s: (0, MOD_SHIFT)),
            pl.BlockSpec((1, D_MODEL), lambda s: (0, MOD_SCALE)),
            pl.BlockSpec(memory_space=pl.ANY),
        ],
        out_specs=pl.BlockSpec((PROJ_WIDTH // LANES, PROJ_TM, LANES),
                               lambda s: (0, jnp.maximum(s - 1, 0), 0)),
        out_shape=jax.ShapeDtypeStruct((PROJ_WIDTH // LANES, SEQ, LANES), F32),
        scratch_shapes=[
            pltpu.VMEM((D_MODEL, PROJ_WIDTH), BF16),
            pltpu.VMEM((PROJ_W_SLOTS, D_MODEL, PROJ_W_CHUNK), F32),
            pltpu.VMEM((PROJ_TM, D_MODEL), BF16),
            pltpu.VMEM((PROJ_TM, D_MODEL), BF16),
            pltpu.SemaphoreType.DMA((PROJ_W_SLOTS,)),
        ],
        compiler_params=pltpu.CompilerParams(
            dimension_semantics=("arbitrary",), vmem_limit_bytes=PROJ_VMEM_LIMIT_BYTES),
        name="in_proj",
    )(x2d, gain, mod, mod, w_in)


def _rows(start, size, stride):
    return pl.ds(start, size) if stride == 1 else pl.ds(start, size, stride=stride)


def _attn_kernel(q_ref, k_ref, v_ref, g_ref, coef_ref, o_ref,
                 kd, vd, tmp, bias_s, s_s, mx_s, m_s, l_s, acc_s):
    sb = pl.program_id(1)
    lane = lax.broadcasted_iota(jnp.int32, (ATTN_QB, LANES), 1)
    is_h0 = lane < HEAD_DIM

    @pl.when(sb == 0)
    def _():
        for src_ref, dst in ((k_ref, kd), (v_ref, vd)):
            for p, (_, d) in enumerate(DILATED_PATTERNS):
                dst[p, pl.ds(0, ATTN_QB), :] = jnp.zeros((ATTN_QB, LANES), BF16)
                d_prev = DILATED_PATTERNS[p - 1][1] if p else 1
                ratio = d // d_prev
                assert p == 0 and d == 1 or ratio == DEINT_RATIO
                per_residue = SEQ // d // DEINT_ROWS
                keep_f32 = 0 < p < len(DILATED_PATTERNS) - 1
                from_ref = src_ref if p <= 1 else tmp

                def deint(c, carry, p=p, d_prev=d_prev, ratio=ratio, per_residue=per_residue,
                          keep_f32=keep_f32, from_ref=from_ref, dst=dst):
                    r = lax.shift_right_logical(c, per_residue.bit_length() - 1)
                    chunk = jnp.bitwise_and(c, per_residue - 1)
                    r_prev = jnp.bitwise_and(r, d_prev - 1)
                    j = lax.shift_right_logical(r, d_prev.bit_length() - 1)
                    start = r_prev * (SEQ // d_prev) + chunk * (DEINT_ROWS * ratio) + j
                    x = from_ref[_rows(start, DEINT_ROWS, ratio), :]
                    if keep_f32:
                        tmp[pl.ds(pl.multiple_of(c * DEINT_ROWS, DEINT_ROWS), DEINT_ROWS), :] = x
                    rows = pl.ds(pl.multiple_of(ATTN_QB + c * DEINT_ROWS, ATTN_QB), DEINT_ROWS)
                    dst[p, rows, :] = x.astype(BF16)
                    return carry

                lax.fori_loop(0, SEQ // DEINT_ROWS, deint, 0, unroll=4)

        qi = lax.broadcasted_iota(jnp.int32, (ATTN_QB, 2 * ATTN_QB), 0)
        ki = lax.broadcasted_iota(jnp.int32, (ATTN_QB, 2 * ATTN_QB), 1)
        dist = qi + ATTN_QB - ki
        in_band = jnp.logical_and(dist >= 0, dist <= ATTN_QB)
        in_band_cur = jnp.logical_and(in_band, ki >= ATTN_QB)
        dist_f = dist.astype(F32)
        for p in range(len(DILATED_PATTERNS)):
            for h in range(HEADS_PER_BLOCK):
                pen = -(coef_ref[pl.program_id(0), p * HEADS_PER_BLOCK + h] * dist_f)
                rows = pl.ds(h * ATTN_QB, ATTN_QB)
                bias_s[p, 0, rows, :] = jnp.where(in_band, pen, NEG_INF)
                bias_s[p, 1, rows, :] = jnp.where(in_band_cur, pen, NEG_INF)

    ones = jnp.ones((2 * ATTN_QB, LANES), BF16)

    def place(d, ti):
        r, n = ti % d, ti // d
        blk = sb * (ATTN_SB // (ATTN_QB * d)) + n
        q_rows = _rows(r + n * (ATTN_QB * d), ATTN_QB, d)
        window = pl.ds(pl.multiple_of(r * (SEQ // d) + blk * ATTN_QB, ATTN_QB), 2 * ATTN_QB)
        return q_rows, window, blk == 0

    def scores(p, d, g0, half):
        for i in range(ATTN_GROUP):
            q_rows, window, first = place(d, g0 + i)
            q = (q_ref[q_rows, :] * (ATTN_SCALE * LOG2E)).astype(BF16)
            zero = jnp.zeros_like(q)
            qs = jnp.concatenate([jnp.where(is_h0, q, zero), jnp.where(is_h0, zero, q)], axis=0)
            s = lax.dot_general(qs, kd[p, window, :], (((1,), (1,)), ((), ())),
                                preferred_element_type=F32)
            s = s + bias_s[p, first.astype(jnp.int32)]
            slot = half * ATTN_GROUP + i
            s_s[slot] = s
            mx_s[slot] = jnp.broadcast_to(jnp.max(s, axis=1, keepdims=True), (2 * ATTN_QB, LANES))

    def weighted(p, d, g0, half):
        for i in range(ATTN_GROUP):
            q_rows, window, _ = place(d, g0 + i)
            slot = half * ATTN_GROUP + i
            mx = mx_s[slot]
            e = jnp.concatenate(
                [jnp.exp2(s_s[slot, :, :LANES] - mx), jnp.exp2(s_s[slot, :, LANES:] - mx)],
                axis=1).astype(BF16)
            vc = jnp.concatenate([vd[p, window, :], ones], axis=1)
            pv = jnp.dot(e, vc, preferred_element_type=F32)
            acc_s[p, q_rows, :] = jnp.where(is_h0, pv[:ATTN_QB, :LANES], pv[ATTN_QB:, :LANES])
            l_s[p, q_rows, :] = jnp.where(is_h0, pv[:ATTN_QB, LANES:], pv[ATTN_QB:, LANES:])
            m_s[p, q_rows, :] = jnp.where(is_h0, mx[:ATTN_QB], mx[ATTN_QB:])

    groups = [(p, d, g0) for p, (_, d) in enumerate(DILATED_PATTERNS)
              for g0 in range(0, ATTN_SB // ATTN_QB, ATTN_GROUP)]
    scores(*groups[0], 0)
    for k, group in enumerate(groups):
        if k + 1 < len(groups):
            scores(*groups[k + 1], (k + 1) % 2)
        weighted(*group, k % 2)

    def combine(i, carry):
        rows = pl.ds(pl.multiple_of(i * COMBINE_ROWS, COMBINE_ROWS), COMBINE_ROWS)
        m0, m1, m2 = m_s[0, rows, :], m_s[1, rows, :], m_s[2, rows, :]
        mm = jnp.maximum(jnp.maximum(m0, m1), m2)
        e0, e1, e2 = jnp.exp2(m0 - mm), jnp.exp2(m1 - mm), jnp.exp2(m2 - mm)
        num = e0 * acc_s[0, rows, :] + e1 * acc_s[1, rows, :] + e2 * acc_s[2, rows, :]
        den = e0 * l_s[0, rows, :] + e1 * l_s[1, rows, :] + e2 * l_s[2, rows, :]
        o_ref[rows, :] = (num / den * _silu(g_ref[rows, :])).astype(BF16)
        return carry

    lax.fori_loop(0, ATTN_SB // COMBINE_ROWS, combine, 0, unroll=2)


def _alibi_coefs():
    slopes = 2.0 ** (-8.0 * np.arange(1, N_HEADS + 1, dtype=np.float64) / N_HEADS)
    dil = np.array([d for _, d in DILATED_PATTERNS], dtype=np.float64)
    coef = slopes.reshape(-1, HEADS_PER_BLOCK)[:, None, :] * dil[None, :, None] * LOG2E
    return jnp.asarray(coef.reshape(N_HEADS // HEADS_PER_BLOCK, -1), dtype=F32)


def _attention(proj):
    n_hb = ATTN_WIDTH // LANES
    n_p = len(DILATED_PATTERNS)
    return pl.pallas_call(
        _attn_kernel,
        grid=(n_hb, SEQ // ATTN_SB),
        in_specs=[
            pl.BlockSpec((None, ATTN_SB, LANES), lambda h, s: (h, s, 0)),
            pl.BlockSpec((None, SEQ, LANES), lambda h, s: (n_hb + h, 0, 0)),
            pl.BlockSpec((None, SEQ, LANES), lambda h, s: (2 * n_hb + h, 0, 0)),
            pl.BlockSpec((None, ATTN_SB, LANES), lambda h, s: (3 * n_hb + h, s, 0)),
            pl.BlockSpec(memory_space=pltpu.SMEM),
        ],
        out_specs=pl.BlockSpec((ATTN_SB, LANES), lambda h, s: (s, h)),
        out_shape=jax.ShapeDtypeStruct((SEQ, ATTN_WIDTH), BF16),
        scratch_shapes=[
            pltpu.VMEM((n_p, ATTN_QB + SEQ, LANES), BF16),
            pltpu.VMEM((n_p, ATTN_QB + SEQ, LANES), BF16),
            pltpu.VMEM((SEQ, LANES), F32),
            pltpu.VMEM((n_p, 2, HEADS_PER_BLOCK * ATTN_QB, 2 * ATTN_QB), F32),
            pltpu.VMEM((2 * ATTN_GROUP, 2 * ATTN_QB, 2 * ATTN_QB), F32),
            pltpu.VMEM((2 * ATTN_GROUP, 2 * ATTN_QB, LANES), F32),
        ] + [pltpu.VMEM((n_p, ATTN_SB, LANES), F32)] * 3,
        compiler_params=pltpu.CompilerParams(
            dimension_semantics=("arbitrary", "arbitrary"),
            vmem_limit_bytes=ATTN_VMEM_LIMIT_BYTES),
        name="dilated_attn",
    )(proj, proj, proj, proj, _alibi_coefs())


def _lru_kernel(u_ref, g_ref, cw_ref, cb_ref, wg_ref, bg_ref, lam_ref, wout_ref,
                o_ref, wout_bf16_ref, ubuf, a_s, b_s, h_s, p_s):
    wout_bf16_ref[...] = wout_ref[...].astype(BF16)

    lam = lam_ref[...]
    log_sig = jnp.minimum(lam, 0.0) - jnp.log1p(jnp.exp(-jnp.abs(lam)))
    half_c_log_sig = (0.5 * LRU_C) * log_sig
    chunks_per_seg = LRU_SEG_LEN // LRU_CHUNK

    ubuf[pl.ds(0, SUBLANES), :] = jnp.zeros((SUBLANES, LANES), F32)

    def chunk(c, carry):
        rows = pl.ds(pl.multiple_of(c * LRU_CHUNK, LRU_CHUNK), LRU_CHUNK)
        ubuf[pl.ds(SUBLANES, LRU_CHUNK), :] = u_ref[rows, :]
        xc = cb_ref[...]
        for j in range(CONV_WIDTH):
            off = SUBLANES - (CONV_WIDTH - 1) + j
            xc = xc + ubuf[pl.ds(off, LRU_CHUNK), :] * cw_ref[pl.ds(j, 1), :]
        ubuf[pl.ds(0, SUBLANES), :] = ubuf[pl.ds(LRU_CHUNK, SUBLANES), :]

        z = jnp.dot(xc.astype(BF16), wg_ref[...], preferred_element_type=F32) + bg_ref[...]
        i = 0.5 * jnp.tanh(z[:, LANES:]) + 0.5
        log_a = half_c_log_sig * jnp.tanh(z[:, :LANES]) + half_c_log_sig
        a = jnp.exp(log_a)
        th = jnp.tanh(log_a)
        x2 = -2.0 * th / (1.0 - th)
        mult = jnp.where(x2 > 0.0, x2 * lax.rsqrt(x2), 0.0)
        b = mult * (i * xc)

        seg = lax.shift_right_logical(c, chunks_per_seg.bit_length() - 1)
        within = jnp.bitwise_and(c, chunks_per_seg - 1)
        dst = pl.ds(pl.multiple_of(seg * LRU_SEG_PITCH + within * LRU_CHUNK, SUBLANES), LRU_CHUNK)
        a_s[dst, :] = a
        b_s[dst, :] = b
        return carry

    lax.fori_loop(0, SEQ // LRU_CHUNK, chunk, 0, unroll=4)

    def step(t, carry):
        h, prod = carry
        rows = pl.ds(t, LRU_SEGMENTS, stride=LRU_SEG_PITCH)
        a = a_s[rows, :]
        h = a * h + b_s[rows, :]
        prod = prod * a
        h_s[rows, :] = h
        p_s[rows, :] = prod
        return h, prod

    h_end, p_end = lax.fori_loop(
        0, LRU_SEG_LEN, step,
        (jnp.zeros((LRU_SEGMENTS, LANES), F32), jnp.ones((LRU_SEGMENTS, LANES), F32)), unroll=8)

    seg_id = lax.broadcasted_iota(jnp.int32, (LRU_SEGMENTS, LANES), 0)
    c_in = jnp.zeros((LRU_SEGMENTS, LANES), F32)
    for _ in range(LRU_SEGMENTS - 1):
        c_in = jnp.where(seg_id == 0, 0.0, pltpu.roll(h_end + p_end * c_in, 1, 0))

    for s in range(LRU_SEGMENTS):
        c_s = c_in[s:s + 1, :]

        def gate(k, carry, s=s, c_s=c_s):
            off = pl.multiple_of(k * LRU_CHUNK, LRU_CHUNK)
            src = pl.ds(s * LRU_SEG_PITCH + off, LRU_CHUNK)
            rows = pl.ds(s * LRU_SEG_LEN + off, LRU_CHUNK)
            h = h_s[src, :] + p_s[src, :] * c_s
            o_ref[rows, :] = (h * _silu(g_ref[rows, :])).astype(BF16)
            return carry

        lax.fori_loop(0, chunks_per_seg, gate, 0)


def _lru(proj, conv_w, conv_b, w_gates, b_gates, lam, w_out):
    n_cb = LRU_WIDTH // LANES
    u_col0 = 4 * ATTN_WIDTH // LANES
    g_col0 = u_col0 + n_cb
    return pl.pallas_call(
        _lru_kernel,
        grid=(n_cb,),
        in_specs=[
            pl.BlockSpec((None, SEQ, LANES), lambda j: (u_col0 + j, 0, 0)),
            pl.BlockSpec((None, SEQ, LANES), lambda j: (g_col0 + j, 0, 0)),
            pl.BlockSpec((CONV_WIDTH, LANES), lambda j: (0, j)),
            pl.BlockSpec((1, LANES), lambda j: (0, j)),
            pl.BlockSpec((None, LANES, 2 * LANES), lambda j: (j, 0, 0)),
            pl.BlockSpec((None, 1, 2 * LANES), lambda j: (j, 0, 0)),
            pl.BlockSpec((1, LANES), lambda j: (0, j)),
            pl.BlockSpec((D_MODEL // n_cb, D_MODEL), lambda j: (j, 0)),
        ],
        out_specs=[
            pl.BlockSpec((SEQ, LANES), lambda j: (0, j)),
            pl.BlockSpec((D_MODEL // n_cb, D_MODEL), lambda j: (j, 0)),
        ],
        out_shape=[
            jax.ShapeDtypeStruct((SEQ, LRU_WIDTH), BF16),
            jax.ShapeDtypeStruct((D_MODEL, D_MODEL), BF16),
        ],
        scratch_shapes=[
            pltpu.VMEM((LRU_CHUNK + SUBLANES, LANES), F32),
        ] + [pltpu.VMEM((LRU_SEGMENTS * LRU_SEG_PITCH, LANES), F32)] * 4,
        compiler_params=pltpu.CompilerParams(
            dimension_semantics=("arbitrary",), vmem_limit_bytes=VMEM_LIMIT_BYTES),
        name="rglru",
    )(proj, proj, conv_w, conv_b, w_gates, b_gates, lam, w_out)


def _block_diag_gates(w_rgate, b_rgate, w_igate, b_igate):
    per = LANES // LRU_BLOCK_W
    n_cb = LRU_WIDTH // LANES

    def bd(w):
        w = w.reshape(n_cb, per, LRU_BLOCK_W, LRU_BLOCK_W)
        eye = jnp.eye(per, dtype=w.dtype)
        return jnp.einsum('cpkj,pq->cpkqj', w, eye).reshape(n_cb, LANES, LANES)

    w = (0.5 * jnp.concatenate([bd(w_rgate), bd(w_igate)], axis=-1)).astype(BF16)
    b = 0.5 * jnp.concatenate(
        [b_rgate.reshape(n_cb, 1, LANES), b_igate.reshape(n_cb, 1, LANES)], axis=-1)
    return w, b


def _out_kernel(ma_ref, ml_ref, wa_ref, wl_ref, x_ref, gate_ref, fg_ref, o_ref):
    mix = jnp.dot(ma_ref[...], wa_ref[...], preferred_element_type=F32)
    mix = mix + jnp.dot(ml_ref[...], wl_ref[...], preferred_element_type=F32)
    y = x_ref[...] + gate_ref[...] * mix
    var = jnp.mean(y * y, axis=-1, keepdims=True)
    o_ref[...] = y * lax.rsqrt(var + EPS) * fg_ref[...]


def _out_proj(mixed_attn, mixed_lru, w_out_bf16, x2d, mod, final_gain):
    return pl.pallas_call(
        _out_kernel,
        grid=(SEQ // OUT_TM,),
        in_specs=[
            pl.BlockSpec((OUT_TM, ATTN_WIDTH), lambda i: (i, 0)),
            pl.BlockSpec((OUT_TM, LRU_WIDTH), lambda i: (i, 0)),
            pl.BlockSpec((ATTN_WIDTH, D_MODEL), lambda i: (0, 0)),
            pl.BlockSpec((LRU_WIDTH, D_MODEL), lambda i: (1, 0)),
            pl.BlockSpec((OUT_TM, D_MODEL), lambda i: (i, 0)),
            pl.BlockSpec((1, D_MODEL), lambda i: (0, MOD_GATE)),
            pl.BlockSpec((1, D_MODEL), lambda i: (0, 0)),
        ],
        out_specs=pl.BlockSpec((OUT_TM, D_MODEL), lambda i: (i, 0)),
        out_shape=jax.ShapeDtypeStruct((SEQ, D_MODEL), F32),
        compiler_params=pltpu.CompilerParams(
            dimension_semantics=("arbitrary",), vmem_limit_bytes=VMEM_LIMIT_BYTES),
        name="out_proj",
    )(mixed_attn, mixed_lru, w_out_bf16, w_out_bf16, x2d, mod, final_gain)


@jax.jit
def kernel(x, c, norm_gain, w_ada, b_ada, w_in, conv_w, conv_b, w_rgate, b_rgate,
           w_igate, b_igate, lru_lambda, w_out, final_gain):
    assert x.shape == (1, SEQ, D_MODEL) and norm_gain.shape[0] == 1
    x2d = x.reshape(SEQ, D_MODEL)
    mod = _ada_mod(c.reshape(D_MODEL, 1), w_ada[0], b_ada)
    proj = _in_proj(x2d, norm_gain, mod, w_in[0])
    mixed_attn = _attention(proj)
    w_gates, b_gates = _block_diag_gates(w_rgate[0], b_rgate[0], w_igate[0], b_igate[0])
    mixed_lru, w_out_bf16 = _lru(proj, conv_w[0], conv_b, w_gates, b_gates, lru_lambda, w_out[0])
    y = _out_proj(mixed_attn, mixed_lru, w_out_bf16, x2d, mod,
                  final_gain.reshape(1, D_MODEL))
    return y.reshape(1, SEQ, D_MODEL)
```

```python
import functools
import math

import jax
import jax.numpy as jnp
import numpy as np
from jax import lax
from jax.experimental import pallas as pl
from jax.experimental.pallas import tpu as pltpu

D_MODEL = 2048
SEQ = 8192
ATTN_WIDTH = D_MODEL // 2
LRU_WIDTH = D_MODEL - ATTN_WIDTH
HEAD_DIM = 64
N_HEADS = ATTN_WIDTH // HEAD_DIM
LRU_BLOCKS = 16
LRU_BLOCK_W = LRU_WIDTH // LRU_BLOCKS
CONV_WIDTH = 4
LRU_C = 8.0
DILATED_PATTERNS = ((128, 1), (512, 4), (2048, 16))
ATTN_SCALE = 1.0 / math.sqrt(HEAD_DIM)
NEG_INF = -1e30
LOG2E = math.log2(math.e)
EPS = 1e-6
PROJ_WIDTH = 4 * ATTN_WIDTH + 2 * LRU_WIDTH
MOD_SHIFT, MOD_SCALE, MOD_GATE = 0, 1, 2

LANES = 128
SUBLANES = 8
VMEM_LIMIT_BYTES = 56 * 1024 * 1024

ADA_TK = 256
PROJ_TM = 256
PROJ_W_CHUNK = 512
PROJ_W_SLOTS = 2
PROJ_VMEM_LIMIT_BYTES = 60 * 1024 * 1024
NORM_ROWS = 64
ATTN_QB = 128
ATTN_SB = 2048
ATTN_GROUP = 4
DEINT_ROWS = 256
DEINT_RATIO = 4
ATTN_VMEM_LIMIT_BYTES = 60 * 1024 * 1024
HEADS_PER_BLOCK = LANES // HEAD_DIM
COMBINE_ROWS = 256
LRU_CHUNK = 256
LRU_SEGMENTS = SUBLANES
LRU_SEG_LEN = SEQ // LRU_SEGMENTS
LRU_SEG_PITCH = LRU_SEG_LEN + SUBLANES
OUT_TM = 512

F32 = jnp.float32
BF16 = jnp.bfloat16


def _silu(x):
    h = 0.5 * x
    return h + h * jnp.tanh(h)


def _ada_kernel(c_ref, w_ref, b_ref, o_ref):
    @pl.when(pl.program_id(0) == 0)
    def _():
        o_ref[...] = b_ref[...]

    c_act = _silu(c_ref[...])
    o_ref[...] += jnp.sum(w_ref[...] * c_act, axis=0, keepdims=True)


def _ada_mod(c_col, w_ada, b_ada):
    n = w_ada.shape[1]
    return pl.pallas_call(
        _ada_kernel,
        grid=(D_MODEL // ADA_TK,),
        in_specs=[
            pl.BlockSpec((ADA_TK, 1), lambda k: (k, 0)),
            pl.BlockSpec((ADA_TK, n), lambda k: (k, 0)),
            pl.BlockSpec((1, n), lambda k: (0, 0)),
        ],
        out_specs=pl.BlockSpec((1, n), lambda k: (0, 0)),
        out_shape=jax.ShapeDtypeStruct((1, n), F32),
        compiler_params=pltpu.CompilerParams(
            dimension_semantics=("arbitrary",), vmem_limit_bytes=VMEM_LIMIT_BYTES),
        name="ada_mod",
    )(c_col, w_ada, b_ada)


def _w_chunk_copy(w_hbm, stage, sem, c):
    slot = c % PROJ_W_SLOTS
    cols = pl.ds(c * PROJ_W_CHUNK, PROJ_W_CHUNK)
    return pltpu.make_async_copy(w_hbm.at[:, cols], stage.at[slot], sem.at[slot])


def _proj_kernel(x0_ref, xn_ref, gain_ref, shift_ref, scale_ref, w_hbm, o_ref,
                 w_res, stage, h_even, h_odd, sem):
    s = pl.program_id(0)
    n_chunks = PROJ_WIDTH // PROJ_W_CHUNK
    slabs_per_chunk = PROJ_W_CHUNK // LANES

    def normalise(x_ref, h_ref):
        mul = gain_ref[...] * (1.0 + scale_ref[...])
        shift = shift_ref[...]
        for r0 in range(0, PROJ_TM, NORM_ROWS):
            xv = x_ref[pl.ds(r0, NORM_ROWS), :]
            var = jnp.mean(xv * xv, axis=-1, keepdims=True)
            h = xv * lax.rsqrt(var + EPS) * mul + shift
            h_ref[pl.ds(r0, NORM_ROWS), :] = h.astype(BF16)

    def project(h_ref, first_slab, n_slabs):
        cols = pl.ds(first_slab * LANES, n_slabs * LANES)
        acc = jnp.dot(h_ref[...], w_res[:, cols], preferred_element_type=F32)
        for b in range(n_slabs):
            o_ref[first_slab + b] = acc[:, b * LANES:(b + 1) * LANES]

    @pl.when(s == 0)
    def _():
        for c in range(PROJ_W_SLOTS):
            _w_chunk_copy(w_hbm, stage, sem, c).start()
        normalise(x0_ref, h_even)
        normalise(xn_ref, h_odd)
        for c in range(n_chunks):
            _w_chunk_copy(w_hbm, stage, sem, c).wait()
            cols = pl.ds(c * PROJ_W_CHUNK, PROJ_W_CHUNK)
            w_res[:, cols] = stage[c % PROJ_W_SLOTS].astype(BF16)
            if c + PROJ_W_SLOTS < n_chunks:
                _w_chunk_copy(w_hbm, stage, sem, c + PROJ_W_SLOTS).start()
            project(h_even, c * slabs_per_chunk, slabs_per_chunk)

    is_even = jnp.bitwise_and(s, 1) == 0

    @pl.when(jnp.logical_and(s > 0, is_even))
    def _():
        normalise(xn_ref, h_odd)
        project(h_even, 0, PROJ_WIDTH // LANES)

    @pl.when(jnp.logical_not(is_even))
    def _():
        normalise(xn_ref, h_even)
        project(h_odd, 0, PROJ_WIDTH // LANES)


def _in_proj(x2d, gain, mod, w_in):
    n_tiles = SEQ // PROJ_TM
    return pl.pallas_call(
        _proj_kernel,
        grid=(n_tiles,),
        in_specs=[
            pl.BlockSpec((PROJ_TM, D_MODEL), lambda s: (0, 0), pipeline_mode=pl.Buffered(1)),
            pl.BlockSpec((PROJ_TM, D_MODEL), lambda s: (jnp.minimum(s + 1, n_tiles - 1), 0)),
            pl.BlockSpec((1, D_MODEL), lambda s: (0, 0)),
            pl.BlockSpec((1, D_MODEL), lambda s: (0, MOD_SHIFT)),
            pl.BlockSpec((1, D_MODEL), lambda s: (0, MOD_SCALE)),
            pl.BlockSpec(memory_space=pl.ANY),
        ],
        out_specs=pl.BlockSpec((PROJ_WIDTH // LANES, PROJ_TM, LANES), lambda s: (0, s, 0)),
        out_shape=jax.ShapeDtypeStruct((PROJ_WIDTH // LANES, SEQ, LANES), F32),
        scratch_shapes=[
            pltpu.VMEM((D_MODEL, PROJ_WIDTH), BF16),
            pltpu.VMEM((PROJ_W_SLOTS, D_MODEL, PROJ_W_CHUNK), F32),
            pltpu.VMEM((PROJ_TM, D_MODEL), BF16),
            pltpu.VMEM((PROJ_TM, D_MODEL), BF16),
            pltpu.SemaphoreType.DMA((PROJ_W_SLOTS,)),
        ],
        compiler_params=pltpu.CompilerParams(
            dimension_semantics=("arbitrary",), vmem_limit_bytes=PROJ_VMEM_LIMIT_BYTES),
        name="in_proj",
    )(x2d, x2d, gain, mod, mod, w_in)


def _rows(start, size, stride):
    return pl.ds(start, size) if stride == 1 else pl.ds(start, size, stride=stride)


def _attn_kernel(q_ref, k_ref, v_ref, g_ref, coef_ref, o_ref,
                 kd, vd, tmp, bias_s, s_s, mx_s, m_s, l_s, acc_s):
    sb = pl.program_id(1)
    lane = lax.broadcasted_iota(jnp.int32, (ATTN_QB, LANES), 1)
    is_h0 = lane < HEAD_DIM

    @pl.when(sb == 0)
    def _():
        for src_ref, dst in ((k_ref, kd), (v_ref, vd)):
            for p, (_, d) in enumerate(DILATED_PATTERNS):
                dst[p, pl.ds(0, ATTN_QB), :] = jnp.zeros((ATTN_QB, LANES), BF16)
                d_prev = DILATED_PATTERNS[p - 1][1] if p else 1
                ratio = d // d_prev
                assert p == 0 and d == 1 or ratio == DEINT_RATIO
                per_residue = SEQ // d // DEINT_ROWS
                keep_f32 = 0 < p < len(DILATED_PATTERNS) - 1
                from_ref = src_ref if p <= 1 else tmp

                def deint(c, carry, p=p, d_prev=d_prev, ratio=ratio, per_residue=per_residue,
                          keep_f32=keep_f32, from_ref=from_ref, dst=dst):
                    r = lax.shift_right_logical(c, per_residue.bit_length() - 1)
                    chunk = jnp.bitwise_and(c, per_residue - 1)
                    r_prev = jnp.bitwise_and(r, d_prev - 1)
                    j = lax.shift_right_logical(r, d_prev.bit_length() - 1)
                    start = r_prev * (SEQ // d_prev) + chunk * (DEINT_ROWS * ratio) + j
                    x = from_ref[_rows(start, DEINT_ROWS, ratio), :]
                    if keep_f32:
                        tmp[pl.ds(pl.multiple_of(c * DEINT_ROWS, DEINT_ROWS), DEINT_ROWS), :] = x
                    rows = pl.ds(pl.multiple_of(ATTN_QB + c * DEINT_ROWS, ATTN_QB), DEINT_ROWS)
                    dst[p, rows, :] = x.astype(BF16)
                    return carry

                lax.fori_loop(0, SEQ // DEINT_ROWS, deint, 0, unroll=4)

        qi = lax.broadcasted_iota(jnp.int32, (ATTN_QB, 2 * ATTN_QB), 0)
        ki = lax.broadcasted_iota(jnp.int32, (ATTN_QB, 2 * ATTN_QB), 1)
        dist = qi + ATTN_QB - ki
        in_band = jnp.logical_and(dist >= 0, dist <= ATTN_QB)
        in_band_cur = jnp.logical_and(in_band, ki >= ATTN_QB)
        dist_f = dist.astype(F32)
        for p in range(len(DILATED_PATTERNS)):
            for h in range(HEADS_PER_BLOCK):
                pen = -(coef_ref[pl.program_id(0), p * HEADS_PER_BLOCK + h] * dist_f)
                rows = pl.ds(h * ATTN_QB, ATTN_QB)
                bias_s[p, 0, rows, :] = jnp.where(in_band, pen, NEG_INF)
                bias_s[p, 1, rows, :] = jnp.where(in_band_cur, pen, NEG_INF)

    ones = jnp.ones((2 * ATTN_QB, LANES), BF16)

    def place(d, ti):
        r, n = ti % d, ti // d
        blk = sb * (ATTN_SB // (ATTN_QB * d)) + n
        q_rows = _rows(r + n * (ATTN_QB * d), ATTN_QB, d)
        window = pl.ds(pl.multiple_of(r * (SEQ // d) + blk * ATTN_QB, ATTN_QB), 2 * ATTN_QB)
        return q_rows, window, blk == 0

    def scores(p, d, g0, half):
        for i in range(ATTN_GROUP):
            q_rows, window, first = place(d, g0 + i)
            q = (q_ref[q_rows, :] * (ATTN_SCALE * LOG2E)).astype(BF16)
            zero = jnp.zeros_like(q)
            qs = jnp.concatenate([jnp.where(is_h0, q, zero), jnp.where(is_h0, zero, q)], axis=0)
            s = lax.dot_general(qs, kd[p, window, :], (((1,), (1,)), ((), ())),
                                preferred_element_type=F32)
            s = s + bias_s[p, first.astype(jnp.int32)]
            slot = half * ATTN_GROUP + i
            s_s[slot] = s
            mx_s[slot] = jnp.broadcast_to(jnp.max(s, axis=1, keepdims=True), (2 * ATTN_QB, LANES))

    def weighted(p, d, g0, half):
        for i in range(ATTN_GROUP):
            q_rows, window, _ = place(d, g0 + i)
            slot = half * ATTN_GROUP + i
            mx = mx_s[slot]
            e = jnp.concatenate(
                [jnp.exp2(s_s[slot, :, :LANES] - mx), jnp.exp2(s_s[slot, :, LANES:] - mx)],
                axis=1).astype(BF16)
            vc = jnp.concatenate([vd[p, window, :], ones], axis=1)
            pv = jnp.dot(e, vc, preferred_element_type=F32)
            acc_s[p, q_rows, :] = jnp.where(is_h0, pv[:ATTN_QB, :LANES], pv[ATTN_QB:, :LANES])
            l_s[p, q_rows, :] = jnp.where(is_h0, pv[:ATTN_QB, LANES:], pv[ATTN_QB:, LANES:])
            m_s[p, q_rows, :] = jnp.where(is_h0, mx[:ATTN_QB], mx[ATTN_QB:])

    groups = [(p, d, g0) for p, (_, d) in enumerate(DILATED_PATTERNS)
              for g0 in range(0, ATTN_SB // ATTN_QB, ATTN_GROUP)]
    scores(*groups[0], 0)
    for k, group in enumerate(groups):
        if k + 1 < len(groups):
            scores(*groups[k + 1], (k + 1) % 2)
        weighted(*group, k % 2)

    def combine(i, carry):
        rows = pl.ds(pl.multiple_of(i * COMBINE_ROWS, COMBINE_ROWS), COMBINE_ROWS)
        m0, m1, m2 = m_s[0, rows, :], m_s[1, rows, :], m_s[2, rows, :]
        mm = jnp.maximum(jnp.maximum(m0, m1), m2)
        e0, e1, e2 = jnp.exp2(m0 - mm), jnp.exp2(m1 - mm), jnp.exp2(m2 - mm)
        num = e0 * acc_s[0, rows, :] + e1 * acc_s[1, rows, :] + e2 * acc_s[2, rows, :]
        den = e0 * l_s[0, rows, :] + e1 * l_s[1, rows, :] + e2 * l_s[2, rows, :]
        o_ref[rows, :] = (num / den * _silu(g_ref[rows, :])).astype(BF16)
        return carry

    lax.fori_loop(0, ATTN_SB // COMBINE_ROWS, combine, 0, unroll=2)


def _alibi_coefs():
    slopes = 2.0 ** (-8.0 * np.arange(1, N_HEADS + 1, dtype=np.float64) / N_HEADS)
    dil = np.array([d for _, d in DILATED_PATTERNS], dtype=np.float64)
    coef = slopes.reshape(-1, HEADS_PER_BLOCK)[:, None, :] * dil[None, :, None] * LOG2E
    return jnp.asarray(coef.reshape(N_HEADS // HEADS_PER_BLOCK, -1), dtype=F32)


def _attention(proj):
    n_hb = ATTN_WIDTH // LANES
    n_p = len(DILATED_PATTERNS)
    return pl.pallas_call(
        _attn_kernel,
        grid=(n_hb, SEQ // ATTN_SB),
        in_specs=[
            pl.BlockSpec((None, ATTN_SB, LANES), lambda h, s: (h, s, 0)),
            pl.BlockSpec((None, SEQ, LANES), lambda h, s: (n_hb + h, 0, 0)),
            pl.BlockSpec((None, SEQ, LANES), lambda h, s: (2 * n_hb + h, 0, 0)),
            pl.BlockSpec((None, ATTN_SB, LANES), lambda h, s: (3 * n_hb + h, s, 0)),
            pl.BlockSpec(memory_space=pltpu.SMEM),
        ],
        out_specs=pl.BlockSpec((ATTN_SB, LANES), lambda h, s: (s, h)),
        out_shape=jax.ShapeDtypeStruct((SEQ, ATTN_WIDTH), BF16),
        scratch_shapes=[
            pltpu.VMEM((n_p, ATTN_QB + SEQ, LANES), BF16),
            pltpu.VMEM((n_p, ATTN_QB + SEQ, LANES), BF16),
            pltpu.VMEM((SEQ, LANES), F32),
            pltpu.VMEM((n_p, 2, HEADS_PER_BLOCK * ATTN_QB, 2 * ATTN_QB), F32),
            pltpu.VMEM((2 * ATTN_GROUP, 2 * ATTN_QB, 2 * ATTN_QB), F32),
            pltpu.VMEM((2 * ATTN_GROUP, 2 * ATTN_QB, LANES), F32),
        ] + [pltpu.VMEM((n_p, ATTN_SB, LANES), F32)] * 3,
        compiler_params=pltpu.CompilerParams(
            dimension_semantics=("arbitrary", "arbitrary"),
            vmem_limit_bytes=ATTN_VMEM_LIMIT_BYTES),
        name="dilated_attn",
    )(proj, proj, proj, proj, _alibi_coefs())


def _lru_kernel(u_ref, g_ref, cw_ref, cb_ref, wg_ref, bg_ref, lam_ref, wout_ref,
                o_ref, wout_bf16_ref, ubuf, a_s, b_s, h_s, p_s):
    wout_bf16_ref[...] = wout_ref[...].astype(BF16)

    lam = lam_ref[...]
    log_sig = jnp.minimum(lam, 0.0) - jnp.log1p(jnp.exp(-jnp.abs(lam)))
    half_c_log_sig = (0.5 * LRU_C) * log_sig
    chunks_per_seg = LRU_SEG_LEN // LRU_CHUNK

    ubuf[pl.ds(0, SUBLANES), :] = jnp.zeros((SUBLANES, LANES), F32)

    def chunk(c, carry):
        rows = pl.ds(pl.multiple_of(c * LRU_CHUNK, LRU_CHUNK), LRU_CHUNK)
        ubuf[pl.ds(SUBLANES, LRU_CHUNK), :] = u_ref[rows, :]
        xc = cb_ref[...]
        for j in range(CONV_WIDTH):
            off = SUBLANES - (CONV_WIDTH - 1) + j
            xc = xc + ubuf[pl.ds(off, LRU_CHUNK), :] * cw_ref[pl.ds(j, 1), :]
        ubuf[pl.ds(0, SUBLANES), :] = ubuf[pl.ds(LRU_CHUNK, SUBLANES), :]

        z = jnp.dot(xc.astype(BF16), wg_ref[...], preferred_element_type=F32) + bg_ref[...]
        i = 0.5 * jnp.tanh(z[:, LANES:]) + 0.5
        log_a = half_c_log_sig * jnp.tanh(z[:, :LANES]) + half_c_log_sig
        a = jnp.exp(log_a)
        th = jnp.tanh(log_a)
        x2 = -2.0 * th / (1.0 - th)
        mult = jnp.where(x2 > 0.0, x2 * lax.rsqrt(x2), 0.0)
        b = mult * (i * xc)

        seg = lax.shift_right_logical(c, chunks_per_seg.bit_length() - 1)
        within = jnp.bitwise_and(c, chunks_per_seg - 1)
        dst = pl.ds(pl.multiple_of(seg * LRU_SEG_PITCH + within * LRU_CHUNK, SUBLANES), LRU_CHUNK)
        a_s[dst, :] = a
        b_s[dst, :] = b
        return carry

    lax.fori_loop(0, SEQ // LRU_CHUNK, chunk, 0, unroll=4)

    def step(t, carry):
        h, prod = carry
        rows = pl.ds(t, LRU_SEGMENTS, stride=LRU_SEG_PITCH)
        a = a_s[rows, :]
        h = a * h + b_s[rows, :]
        prod = prod * a
        h_s[rows, :] = h
        p_s[rows, :] = prod
        return h, prod

    h_end, p_end = lax.fori_loop(
        0, LRU_SEG_LEN, step,
        (jnp.zeros((LRU_SEGMENTS, LANES), F32), jnp.ones((LRU_SEGMENTS, LANES), F32)), unroll=8)

    seg_id = lax.broadcasted_iota(jnp.int32, (LRU_SEGMENTS, LANES), 0)
    c_in = jnp.zeros((LRU_SEGMENTS, LANES), F32)
    for _ in range(LRU_SEGMENTS - 1):
        c_in = jnp.where(seg_id == 0, 0.0, pltpu.roll(h_end + p_end * c_in, 1, 0))

    for s in range(LRU_SEGMENTS):
        c_s = c_in[s:s + 1, :]

        def gate(k, carry, s=s, c_s=c_s):
            off = pl.multiple_of(k * LRU_CHUNK, LRU_CHUNK)
            src = pl.ds(s * LRU_SEG_PITCH + off, LRU_CHUNK)
            rows = pl.ds(s * LRU_SEG_LEN + off, LRU_CHUNK)
            h = h_s[src, :] + p_s[src, :] * c_s
            o_ref[rows, :] = (h * _silu(g_ref[rows, :])).astype(BF16)
            return carry

        lax.fori_loop(0, chunks_per_seg, gate, 0)


def _lru(proj, conv_w, conv_b, w_gates, b_gates, lam, w_out):
    n_cb = LRU_WIDTH // LANES
    u_col0 = 4 * ATTN_WIDTH // LANES
    g_col0 = u_col0 + n_cb
    return pl.pallas_call(
        _lru_kernel,
        grid=(n_cb,),
        in_specs=[
            pl.BlockSpec((None, SEQ, LANES), lambda j: (u_col0 + j, 0, 0)),
            pl.BlockSpec((None, SEQ, LANES), lambda j: (g_col0 + j, 0, 0)),
            pl.BlockSpec((CONV_WIDTH, LANES), lambda j: (0, j)),
            pl.BlockSpec((1, LANES), lambda j: (0, j)),
            pl.BlockSpec((None, LANES, 2 * LANES), lambda j: (j, 0, 0)),
            pl.BlockSpec((None, 1, 2 * LANES), lambda j: (j, 0, 0)),
            pl.BlockSpec((1, LANES), lambda j: (0, j)),
            pl.BlockSpec((D_MODEL // n_cb, D_MODEL), lambda j: (j, 0)),
        ],
        out_specs=[
            pl.BlockSpec((SEQ, LANES), lambda j: (0, j)),
            pl.BlockSpec((D_MODEL // n_cb, D_MODEL), lambda j: (j, 0)),
        ],
        out_shape=[
            jax.ShapeDtypeStruct((SEQ, LRU_WIDTH), BF16),
            jax.ShapeDtypeStruct((D_MODEL, D_MODEL), BF16),
        ],
        scratch_shapes=[
            pltpu.VMEM((LRU_CHUNK + SUBLANES, LANES), F32),
        ] + [pltpu.VMEM((LRU_SEGMENTS * LRU_SEG_PITCH, LANES), F32)] * 4,
        compiler_params=pltpu.CompilerParams(
            dimension_semantics=("arbitrary",), vmem_limit_bytes=VMEM_LIMIT_BYTES),
        name="rglru",
    )(proj, proj, conv_w, conv_b, w_gates, b_gates, lam, w_out)


def _block_diag_gates(w_rgate, b_rgate, w_igate, b_igate):
    per = LANES // LRU_BLOCK_W
    n_cb = LRU_WIDTH // LANES

    def bd(w):
        w = w.reshape(n_cb, per, LRU_BLOCK_W, LRU_BLOCK_W)
        eye = jnp.eye(per, dtype=w.dtype)
        return jnp.einsum('cpkj,pq->cpkqj', w, eye).reshape(n_cb, LANES, LANES)

    w = (0.5 * jnp.concatenate([bd(w_rgate), bd(w_igate)], axis=-1)).astype(BF16)
    b = 0.5 * jnp.concatenate(
        [b_rgate.reshape(n_cb, 1, LANES), b_igate.reshape(n_cb, 1, LANES)], axis=-1)
    return w, b


def _out_kernel(ma_ref, ml_ref, wa_ref, wl_ref, x_ref, gate_ref, fg_ref, o_ref):
    mix = jnp.dot(ma_ref[...], wa_ref[...], preferred_element_type=F32)
    mix = mix + jnp.dot(ml_ref[...], wl_ref[...], preferred_element_type=F32)
    y = x_ref[...] + gate_ref[...] * mix
    var = jnp.mean(y * y, axis=-1, keepdims=True)
    o_ref[...] = y * lax.rsqrt(var + EPS) * fg_ref[...]


def _out_proj(mixed_attn, mixed_lru, w_out_bf16, x2d, mod, final_gain):
    return pl.pallas_call(
        _out_kernel,
        grid=(SEQ // OUT_TM,),
        in_specs=[
            pl.BlockSpec((OUT_TM, ATTN_WIDTH), lambda i: (i, 0)),
            pl.BlockSpec((OUT_TM, LRU_WIDTH), lambda i: (i, 0)),
            pl.BlockSpec((ATTN_WIDTH, D_MODEL), lambda i: (0, 0)),
            pl.BlockSpec((LRU_WIDTH, D_MODEL), lambda i: (1, 0)),
            pl.BlockSpec((OUT_TM, D_MODEL), lambda i: (i, 0)),
            pl.BlockSpec((1, D_MODEL), lambda i: (0, MOD_GATE)),
            pl.BlockSpec((1, D_MODEL), lambda i: (0, 0)),
        ],
        out_specs=pl.BlockSpec((OUT_TM, D_MODEL), lambda i: (i, 0)),
        out_shape=jax.ShapeDtypeStruct((SEQ, D_MODEL), F32),
        compiler_params=pltpu.CompilerParams(
            dimension_semantics=("arbitrary",), vmem_limit_bytes=VMEM_LIMIT_BYTES),
        name="out_proj",
    )(mixed_attn, mixed_lru, w_out_bf16, w_out_bf16, x2d, mod, final_gain)


@jax.jit
def kernel(x, c, norm_gain, w_ada, b_ada, w_in, conv_w, conv_b, w_rgate, b_rgate,
           w_igate, b_igate, lru_lambda, w_out, final_gain):
    assert x.shape == (1, SEQ, D_MODEL) and norm_gain.shape[0] == 1
    x2d = x.reshape(SEQ, D_MODEL)
    mod = _ada_mod(c.reshape(D_MODEL, 1), w_ada[0], b_ada)
    proj = _in_proj(x2d, norm_gain, mod, w_in[0])
    mixed_attn = _attention(proj)
    w_gates, b_gates = _block_diag_gates(w_rgate[0], b_rgate[0], w_igate[0], b_igate[0])
    mixed_lru, w_out_bf16 = _lru(proj, conv_w[0], conv_b, w_gates, b_gates, lru_lambda, w_out[0])
    y = _out_proj(mixed_attn, mixed_lru, w_out_bf16, x2d, mod,
                  final_gain.reshape(1, D_MODEL))
    return y.reshape(1, SEQ, D_MODEL)
```

```python
import functools
import math

import jax
import jax.numpy as jnp
import numpy as np
from jax import lax
from jax.experimental import pallas as pl
from jax.experimental.pallas import tpu as pltpu

D_MODEL = 2048
SEQ = 8192
ATTN_WIDTH = D_MODEL // 2
LRU_WIDTH = D_MODEL - ATTN_WIDTH
HEAD_DIM = 64
N_HEADS = ATTN_WIDTH // HEAD_DIM
LRU_BLOCKS = 16
LRU_BLOCK_W = LRU_WIDTH // LRU_BLOCKS
CONV_WIDTH = 4
LRU_C = 8.0
DILATED_PATTERNS = ((128, 1), (512, 4), (2048, 16))
ATTN_SCALE = 1.0 / math.sqrt(HEAD_DIM)
NEG_INF = -1e30
LOG2E = math.log2(math.e)
EPS = 1e-6
PROJ_WIDTH = 4 * ATTN_WIDTH + 2 * LRU_WIDTH
MOD_SHIFT, MOD_SCALE, MOD_GATE = 0, 1, 2

LANES = 128
SUBLANES = 8
VMEM_LIMIT_BYTES = 56 * 1024 * 1024

ADA_TK = 256
PROJ_TM = 256
PROJ_W_CHUNK = 512
PROJ_W_SLOTS = 2
PROJ_VMEM_LIMIT_BYTES = 60 * 1024 * 1024
NORM_ROWS = 64
ATTN_QB = 128
ATTN_SB = 2048
ATTN_GROUP = 4
DEINT_ROWS = 256
DEINT_RATIO = 4
ATTN_VMEM_LIMIT_BYTES = 60 * 1024 * 1024
HEADS_PER_BLOCK = LANES // HEAD_DIM
COMBINE_ROWS = 256
LRU_CHUNK = 256
LRU_SEGMENTS = SUBLANES
LRU_SEG_LEN = SEQ // LRU_SEGMENTS
LRU_SEG_PITCH = LRU_SEG_LEN + SUBLANES
OUT_TM = 512

F32 = jnp.float32
BF16 = jnp.bfloat16


def _silu(x):
    h = 0.5 * x
    return h + h * jnp.tanh(h)


def _ada_accumulate(first, c_ref, w_ref, b_ref, o_ref):
    @pl.when(first)
    def _():
        o_ref[...] = b_ref[...]

    c_act = _silu(c_ref[...])
    o_ref[...] += jnp.sum(w_ref[...] * c_act, axis=0, keepdims=True)


def _ada_kernel(c_ref, w_ref, b_ref, o_ref):
    _ada_accumulate(pl.program_id(0) == 0, c_ref, w_ref, b_ref, o_ref)


def _ada_mod(c_col, w_ada, b_ada):
    n = 2 * D_MODEL
    return pl.pallas_call(
        _ada_kernel,
        grid=(D_MODEL // ADA_TK,),
        in_specs=[
            pl.BlockSpec((ADA_TK, 1), lambda k: (k, 0)),
            pl.BlockSpec((ADA_TK, n), lambda k: (k, 0)),
            pl.BlockSpec((1, n), lambda k: (0, 0)),
        ],
        out_specs=pl.BlockSpec((1, n), lambda k: (0, 0)),
        out_shape=jax.ShapeDtypeStruct((1, n), F32),
        compiler_params=pltpu.CompilerParams(
            dimension_semantics=("arbitrary",), vmem_limit_bytes=VMEM_LIMIT_BYTES),
        name="ada_mod",
    )(c_col, w_ada, b_ada)


def _w_chunk_copy(w_hbm, stage, sem, c):
    slot = c % PROJ_W_SLOTS
    cols = pl.ds(c * PROJ_W_CHUNK, PROJ_W_CHUNK)
    return pltpu.make_async_copy(w_hbm.at[:, cols], stage.at[slot], sem.at[slot])


def _proj_kernel(x0_ref, xn_ref, gain_ref, shift_ref, scale_ref, w_hbm, o_ref,
                 w_res, stage, h_even, h_odd, sem):
    s = pl.program_id(0)
    n_chunks = PROJ_WIDTH // PROJ_W_CHUNK
    slabs_per_chunk = PROJ_W_CHUNK // LANES

    def normalise(x_ref, h_ref):
        mul = gain_ref[...] * (1.0 + scale_ref[...])
        shift = shift_ref[...]
        for r0 in range(0, PROJ_TM, NORM_ROWS):
            xv = x_ref[pl.ds(r0, NORM_ROWS), :]
            var = jnp.mean(xv * xv, axis=-1, keepdims=True)
            h = xv * lax.rsqrt(var + EPS) * mul + shift
            h_ref[pl.ds(r0, NORM_ROWS), :] = h.astype(BF16)

    def project(h_ref, first_slab, n_slabs):
        cols = pl.ds(first_slab * LANES, n_slabs * LANES)
        acc = jnp.dot(h_ref[...], w_res[:, cols], preferred_element_type=F32)
        for b in range(n_slabs):
            o_ref[first_slab + b] = acc[:, b * LANES:(b + 1) * LANES]

    @pl.when(s == 0)
    def _():
        for c in range(PROJ_W_SLOTS):
            _w_chunk_copy(w_hbm, stage, sem, c).start()
        normalise(x0_ref, h_even)
        normalise(xn_ref, h_odd)
        for c in range(n_chunks):
            _w_chunk_copy(w_hbm, stage, sem, c).wait()
            cols = pl.ds(c * PROJ_W_CHUNK, PROJ_W_CHUNK)
            w_res[:, cols] = stage[c % PROJ_W_SLOTS].astype(BF16)
            if c + PROJ_W_SLOTS < n_chunks:
                _w_chunk_copy(w_hbm, stage, sem, c + PROJ_W_SLOTS).start()
            project(h_even, c * slabs_per_chunk, slabs_per_chunk)

    is_even = jnp.bitwise_and(s, 1) == 0

    @pl.when(jnp.logical_and(s > 0, is_even))
    def _():
        normalise(xn_ref, h_odd)
        project(h_even, 0, PROJ_WIDTH // LANES)

    @pl.when(jnp.logical_not(is_even))
    def _():
        normalise(xn_ref, h_even)
        project(h_odd, 0, PROJ_WIDTH // LANES)


def _in_proj(x2d, gain, mod, w_in):
    n_tiles = SEQ // PROJ_TM
    return pl.pallas_call(
        _proj_kernel,
        grid=(n_tiles,),
        in_specs=[
            pl.BlockSpec((PROJ_TM, D_MODEL), lambda s: (0, 0), pipeline_mode=pl.Buffered(1)),
            pl.BlockSpec((PROJ_TM, D_MODEL), lambda s: (jnp.minimum(s + 1, n_tiles - 1), 0)),
            pl.BlockSpec((1, D_MODEL), lambda s: (0, 0)),
            pl.BlockSpec((1, D_MODEL), lambda s: (0, MOD_SHIFT)),
            pl.BlockSpec((1, D_MODEL), lambda s: (0, MOD_SCALE)),
            pl.BlockSpec(memory_space=pl.ANY),
        ],
        out_specs=pl.BlockSpec((PROJ_WIDTH // LANES, PROJ_TM, LANES), lambda s: (0, s, 0)),
        out_shape=jax.ShapeDtypeStruct((PROJ_WIDTH // LANES, SEQ, LANES), F32),
        scratch_shapes=[
            pltpu.VMEM((D_MODEL, PROJ_WIDTH), BF16),
            pltpu.VMEM((PROJ_W_SLOTS, D_MODEL, PROJ_W_CHUNK), F32),
            pltpu.VMEM((PROJ_TM, D_MODEL), BF16),
            pltpu.VMEM((PROJ_TM, D_MODEL), BF16),
            pltpu.SemaphoreType.DMA((PROJ_W_SLOTS,)),
        ],
        compiler_params=pltpu.CompilerParams(
            dimension_semantics=("arbitrary",), vmem_limit_bytes=PROJ_VMEM_LIMIT_BYTES),
        name="in_proj",
    )(x2d, x2d, gain, mod, mod, w_in)


def _rows(start, size, stride):
    return pl.ds(start, size) if stride == 1 else pl.ds(start, size, stride=stride)


def _attn_kernel(q_ref, k_ref, v_ref, g_ref, coef_ref, c_ref, wada_ref, bada_ref,
                 o_ref, gate_ref, kd, vd, tmp, bias_s, s_s, mx_s, m_s, l_s, acc_s):
    sb = pl.program_id(1)
    _ada_accumulate(jnp.logical_and(pl.program_id(0) == 0, sb == 0),
                    c_ref, wada_ref, bada_ref, gate_ref)
    lane = lax.broadcasted_iota(jnp.int32, (ATTN_QB, LANES), 1)
    is_h0 = lane < HEAD_DIM

    @pl.when(sb == 0)
    def _():
        for src_ref, dst in ((k_ref, kd), (v_ref, vd)):
            for p, (_, d) in enumerate(DILATED_PATTERNS):
                dst[p, pl.ds(0, ATTN_QB), :] = jnp.zeros((ATTN_QB, LANES), BF16)
                d_prev = DILATED_PATTERNS[p - 1][1] if p else 1
                ratio = d // d_prev
                assert p == 0 and d == 1 or ratio == DEINT_RATIO
                per_residue = SEQ // d // DEINT_ROWS
                keep_f32 = 0 < p < len(DILATED_PATTERNS) - 1
                from_ref = src_ref if p <= 1 else tmp

                def deint(c, carry, p=p, d_prev=d_prev, ratio=ratio, per_residue=per_residue,
                          keep_f32=keep_f32, from_ref=from_ref, dst=dst):
                    r = lax.shift_right_logical(c, per_residue.bit_length() - 1)
                    chunk = jnp.bitwise_and(c, per_residue - 1)
                    r_prev = jnp.bitwise_and(r, d_prev - 1)
                    j = lax.shift_right_logical(r, d_prev.bit_length() - 1)
                    start = r_prev * (SEQ // d_prev) + chunk * (DEINT_ROWS * ratio) + j
                    x = from_ref[_rows(start, DEINT_ROWS, ratio), :]
                    if keep_f32:
                        tmp[pl.ds(pl.multiple_of(c * DEINT_ROWS, DEINT_ROWS), DEINT_ROWS), :] = x
                    rows = pl.ds(pl.multiple_of(ATTN_QB + c * DEINT_ROWS, ATTN_QB), DEINT_ROWS)
                    dst[p, rows, :] = x.astype(BF16)
                    return carry

                lax.fori_loop(0, SEQ // DEINT_ROWS, deint, 0, unroll=4)

        qi = lax.broadcasted_iota(jnp.int32, (ATTN_QB, 2 * ATTN_QB), 0)
        ki = lax.broadcasted_iota(jnp.int32, (ATTN_QB, 2 * ATTN_QB), 1)
        dist = qi + ATTN_QB - ki
        in_band = jnp.logical_and(dist >= 0, dist <= ATTN_QB)
        in_band_cur = jnp.logical_and(in_band, ki >= ATTN_QB)
        dist_f = dist.astype(F32)
        for p in range(len(DILATED_PATTERNS)):
            for h in range(HEADS_PER_BLOCK):
                pen = -(coef_ref[pl.program_id(0), p * HEADS_PER_BLOCK + h] * dist_f)
                rows = pl.ds(h * ATTN_QB, ATTN_QB)
                bias_s[p, 0, rows, :] = jnp.where(in_band, pen, NEG_INF)
                bias_s[p, 1, rows, :] = jnp.where(in_band_cur, pen, NEG_INF)

    ones = jnp.ones((2 * ATTN_QB, LANES), BF16)

    def place(d, ti):
        r, n = ti % d, ti // d
        blk = sb * (ATTN_SB // (ATTN_QB * d)) + n
        q_rows = _rows(r + n * (ATTN_QB * d), ATTN_QB, d)
        window = pl.ds(pl.multiple_of(r * (SEQ // d) + blk * ATTN_QB, ATTN_QB), 2 * ATTN_QB)
        return q_rows, window, blk == 0

    def scores(p, d, g0, half):
        for i in range(ATTN_GROUP):
            q_rows, window, first = place(d, g0 + i)
            q = (q_ref[q_rows, :] * (ATTN_SCALE * LOG2E)).astype(BF16)
            zero = jnp.zeros_like(q)
            qs = jnp.concatenate([jnp.where(is_h0, q, zero), jnp.where(is_h0, zero, q)], axis=0)
            s = lax.dot_general(qs, kd[p, window, :], (((1,), (1,)), ((), ())),
                                preferred_element_type=F32)
            s = s + bias_s[p, first.astype(jnp.int32)]
            slot = half * ATTN_GROUP + i
            s_s[slot] = s
            mx_s[slot] = jnp.broadcast_to(jnp.max(s, axis=1, keepdims=True), (2 * ATTN_QB, LANES))

    def weighted(p, d, g0, half):
        for i in range(ATTN_GROUP):
            q_rows, window, _ = place(d, g0 + i)
            slot = half * ATTN_GROUP + i
            mx = mx_s[slot]
            e = jnp.concatenate(
                [jnp.exp2(s_s[slot, :, :LANES] - mx), jnp.exp2(s_s[slot, :, LANES:] - mx)],
                axis=1).astype(BF16)
            vc = jnp.concatenate([vd[p, window, :], ones], axis=1)
            pv = jnp.dot(e, vc, preferred_element_type=F32)
            acc_s[p, q_rows, :] = jnp.where(is_h0, pv[:ATTN_QB, :LANES], pv[ATTN_QB:, :LANES])
            l_s[p, q_rows, :] = jnp.where(is_h0, pv[:ATTN_QB, LANES:], pv[ATTN_QB:, LANES:])
            m_s[p, q_rows, :] = jnp.where(is_h0, mx[:ATTN_QB], mx[ATTN_QB:])

    groups = [(p, d, g0) for p, (_, d) in enumerate(DILATED_PATTERNS)
              for g0 in range(0, ATTN_SB // ATTN_QB, ATTN_GROUP)]
    scores(*groups[0], 0)
    for k, group in enumerate(groups):
        if k + 1 < len(groups):
            scores(*groups[k + 1], (k + 1) % 2)
        weighted(*group, k % 2)

    def combine(i, carry):
        rows = pl.ds(pl.multiple_of(i * COMBINE_ROWS, COMBINE_ROWS), COMBINE_ROWS)
        m0, m1, m2 = m_s[0, rows, :], m_s[1, rows, :], m_s[2, rows, :]
        mm = jnp.maximum(jnp.maximum(m0, m1), m2)
        e0, e1, e2 = jnp.exp2(m0 - mm), jnp.exp2(m1 - mm), jnp.exp2(m2 - mm)
        num = e0 * acc_s[0, rows, :] + e1 * acc_s[1, rows, :] + e2 * acc_s[2, rows, :]
        den = e0 * l_s[0, rows, :] + e1 * l_s[1, rows, :] + e2 * l_s[2, rows, :]
        o_ref[rows, :] = (num / den * _silu(g_ref[rows, :])).astype(BF16)
        return carry

    lax.fori_loop(0, ATTN_SB // COMBINE_ROWS, combine, 0, unroll=2)


def _alibi_coefs():
    slopes = 2.0 ** (-8.0 * np.arange(1, N_HEADS + 1, dtype=np.float64) / N_HEADS)
    dil = np.array([d for _, d in DILATED_PATTERNS], dtype=np.float64)
    coef = slopes.reshape(-1, HEADS_PER_BLOCK)[:, None, :] * dil[None, :, None] * LOG2E
    return jnp.asarray(coef.reshape(N_HEADS // HEADS_PER_BLOCK, -1), dtype=F32)


def _attention(proj, c_col, w_ada, b_ada):
    n_hb = ATTN_WIDTH // LANES
    n_sb = SEQ // ATTN_SB
    n_p = len(DILATED_PATTERNS)
    ada_rows = D_MODEL // (n_hb * n_sb)
    return pl.pallas_call(
        _attn_kernel,
        grid=(n_hb, n_sb),
        in_specs=[
            pl.BlockSpec((None, ATTN_SB, LANES), lambda h, s: (h, s, 0)),
            pl.BlockSpec((None, SEQ, LANES), lambda h, s: (n_hb + h, 0, 0)),
            pl.BlockSpec((None, SEQ, LANES), lambda h, s: (2 * n_hb + h, 0, 0)),
            pl.BlockSpec((None, ATTN_SB, LANES), lambda h, s: (3 * n_hb + h, s, 0)),
            pl.BlockSpec(memory_space=pltpu.SMEM),
            pl.BlockSpec((ada_rows, 1), lambda h, s: (h * n_sb + s, 0)),
            pl.BlockSpec((ada_rows, D_MODEL), lambda h, s: (h * n_sb + s, MOD_GATE)),
            pl.BlockSpec((1, D_MODEL), lambda h, s: (0, MOD_GATE)),
        ],
        out_specs=[
            pl.BlockSpec((ATTN_SB, LANES), lambda h, s: (s, h)),
            pl.BlockSpec((1, D_MODEL), lambda h, s: (0, 0)),
        ],
        out_shape=[
            jax.ShapeDtypeStruct((SEQ, ATTN_WIDTH), BF16),
            jax.ShapeDtypeStruct((1, D_MODEL), F32),
        ],
        scratch_shapes=[
            pltpu.VMEM((n_p, ATTN_QB + SEQ, LANES), BF16),
            pltpu.VMEM((n_p, ATTN_QB + SEQ, LANES), BF16),
            pltpu.VMEM((SEQ, LANES), F32),
            pltpu.VMEM((n_p, 2, HEADS_PER_BLOCK * ATTN_QB, 2 * ATTN_QB), F32),
            pltpu.VMEM((2 * ATTN_GROUP, 2 * ATTN_QB, 2 * ATTN_QB), F32),
            pltpu.VMEM((2 * ATTN_GROUP, 2 * ATTN_QB, LANES), F32),
        ] + [pltpu.VMEM((n_p, ATTN_SB, LANES), F32)] * 3,
        compiler_params=pltpu.CompilerParams(
            dimension_semantics=("arbitrary", "arbitrary"),
            vmem_limit_bytes=ATTN_VMEM_LIMIT_BYTES),
        name="dilated_attn",
    )(proj, proj, proj, proj, _alibi_coefs(), c_col, w_ada, b_ada)


def _lru_kernel(u_ref, g_ref, cw_ref, cb_ref, wg_ref, bg_ref, lam_ref, wout_ref,
                o_ref, wout_bf16_ref, ubuf, a_s, b_s, h_s, p_s):
    wout_bf16_ref[...] = wout_ref[...].astype(BF16)

    lam = lam_ref[...]
    log_sig = jnp.minimum(lam, 0.0) - jnp.log1p(jnp.exp(-jnp.abs(lam)))
    half_c_log_sig = (0.5 * LRU_C) * log_sig
    chunks_per_seg = LRU_SEG_LEN // LRU_CHUNK

    ubuf[pl.ds(0, SUBLANES), :] = jnp.zeros((SUBLANES, LANES), F32)

    def chunk(c, carry):
        rows = pl.ds(pl.multiple_of(c * LRU_CHUNK, LRU_CHUNK), LRU_CHUNK)
        ubuf[pl.ds(SUBLANES, LRU_CHUNK), :] = u_ref[rows, :]
        xc = cb_ref[...]
        for j in range(CONV_WIDTH):
            off = SUBLANES - (CONV_WIDTH - 1) + j
            xc = xc + ubuf[pl.ds(off, LRU_CHUNK), :] * cw_ref[pl.ds(j, 1), :]
        ubuf[pl.ds(0, SUBLANES), :] = ubuf[pl.ds(LRU_CHUNK, SUBLANES), :]

        z = jnp.dot(xc.astype(BF16), wg_ref[...], preferred_element_type=F32) + bg_ref[...]
        i = 0.5 * jnp.tanh(z[:, LANES:]) + 0.5
        log_a = half_c_log_sig * jnp.tanh(z[:, :LANES]) + half_c_log_sig
        a = jnp.exp(log_a)
        th = jnp.tanh(log_a)
        x2 = -2.0 * th / (1.0 - th)
        mult = jnp.where(x2 > 0.0, x2 * lax.rsqrt(x2), 0.0)
        b = mult * (i * xc)

        seg = lax.shift_right_logical(c, chunks_per_seg.bit_length() - 1)
        within = jnp.bitwise_and(c, chunks_per_seg - 1)
        dst = pl.ds(pl.multiple_of(seg * LRU_SEG_PITCH + within * LRU_CHUNK, SUBLANES), LRU_CHUNK)
        a_s[dst, :] = a
        b_s[dst, :] = b
        return carry

    lax.fori_loop(0, SEQ // LRU_CHUNK, chunk, 0, unroll=4)

    def step(t, carry):
        h, prod = carry
        rows = pl.ds(t, LRU_SEGMENTS, stride=LRU_SEG_PITCH)
        a = a_s[rows, :]
        h = a * h + b_s[rows, :]
        prod = prod * a
        h_s[rows, :] = h
        p_s[rows, :] = prod
        return h, prod

    h_end, p_end = lax.fori_loop(
        0, LRU_SEG_LEN, step,
        (jnp.zeros((LRU_SEGMENTS, LANES), F32), jnp.ones((LRU_SEGMENTS, LANES), F32)), unroll=8)

    seg_id = lax.broadcasted_iota(jnp.int32, (LRU_SEGMENTS, LANES), 0)
    c_in = jnp.zeros((LRU_SEGMENTS, LANES), F32)
    for _ in range(LRU_SEGMENTS - 1):
        c_in = jnp.where(seg_id == 0, 0.0, pltpu.roll(h_end + p_end * c_in, 1, 0))

    for s in range(LRU_SEGMENTS):
        c_s = c_in[s:s + 1, :]

        def gate(k, carry, s=s, c_s=c_s):
            off = pl.multiple_of(k * LRU_CHUNK, LRU_CHUNK)
            src = pl.ds(s * LRU_SEG_PITCH + off, LRU_CHUNK)
            rows = pl.ds(s * LRU_SEG_LEN + off, LRU_CHUNK)
            h = h_s[src, :] + p_s[src, :] * c_s
            o_ref[rows, :] = (h * _silu(g_ref[rows, :])).astype(BF16)
            return carry

        lax.fori_loop(0, chunks_per_seg, gate, 0)


def _lru(proj, conv_w, conv_b, w_gates, b_gates, lam, w_out):
    n_cb = LRU_WIDTH // LANES
    u_col0 = 4 * ATTN_WIDTH // LANES
    g_col0 = u_col0 + n_cb
    return pl.pallas_call(
        _lru_kernel,
        grid=(n_cb,),
        in_specs=[
            pl.BlockSpec((None, SEQ, LANES), lambda j: (u_col0 + j, 0, 0)),
            pl.BlockSpec((None, SEQ, LANES), lambda j: (g_col0 + j, 0, 0)),
            pl.BlockSpec((CONV_WIDTH, LANES), lambda j: (0, j)),
            pl.BlockSpec((1, LANES), lambda j: (0, j)),
            pl.BlockSpec((None, LANES, 2 * LANES), lambda j: (j, 0, 0)),
            pl.BlockSpec((None, 1, 2 * LANES), lambda j: (j, 0, 0)),
            pl.BlockSpec((1, LANES), lambda j: (0, j)),
            pl.BlockSpec((D_MODEL // n_cb, D_MODEL), lambda j: (j, 0)),
        ],
        out_specs=[
            pl.BlockSpec((SEQ, LANES), lambda j: (0, j)),
            pl.BlockSpec((D_MODEL // n_cb, D_MODEL), lambda j: (j, 0)),
        ],
        out_shape=[
            jax.ShapeDtypeStruct((SEQ, LRU_WIDTH), BF16),
            jax.ShapeDtypeStruct((D_MODEL, D_MODEL), BF16),
        ],
        scratch_shapes=[
            pltpu.VMEM((LRU_CHUNK + SUBLANES, LANES), F32),
        ] + [pltpu.VMEM((LRU_SEGMENTS * LRU_SEG_PITCH, LANES), F32)] * 4,
        compiler_params=pltpu.CompilerParams(
            dimension_semantics=("arbitrary",), vmem_limit_bytes=VMEM_LIMIT_BYTES),
        name="rglru",
    )(proj, proj, conv_w, conv_b, w_gates, b_gates, lam, w_out)


def _block_diag_gates(w_rgate, b_rgate, w_igate, b_igate):
    per = LANES // LRU_BLOCK_W
    n_cb = LRU_WIDTH // LANES

    def bd(w):
        w = w.reshape(n_cb, per, LRU_BLOCK_W, LRU_BLOCK_W)
        eye = jnp.eye(per, dtype=w.dtype)
        return jnp.einsum('cpkj,pq->cpkqj', w, eye).reshape(n_cb, LANES, LANES)

    w = (0.5 * jnp.concatenate([bd(w_rgate), bd(w_igate)], axis=-1)).astype(BF16)
    b = 0.5 * jnp.concatenate(
        [b_rgate.reshape(n_cb, 1, LANES), b_igate.reshape(n_cb, 1, LANES)], axis=-1)
    return w, b


def _out_kernel(ma_ref, ml_ref, wa_ref, wl_ref, x_ref, gate_ref, fg_ref, o_ref):
    mix = jnp.dot(ma_ref[...], wa_ref[...], preferred_element_type=F32)
    mix = mix + jnp.dot(ml_ref[...], wl_ref[...], preferred_element_type=F32)
    y = x_ref[...] + gate_ref[...] * mix
    var = jnp.mean(y * y, axis=-1, keepdims=True)
    o_ref[...] = y * lax.rsqrt(var + EPS) * fg_ref[...]


def _out_proj(mixed_attn, mixed_lru, w_out_bf16, x2d, gate, final_gain):
    return pl.pallas_call(
        _out_kernel,
        grid=(SEQ // OUT_TM,),
        in_specs=[
            pl.BlockSpec((OUT_TM, ATTN_WIDTH), lambda i: (i, 0)),
            pl.BlockSpec((OUT_TM, LRU_WIDTH), lambda i: (i, 0)),
            pl.BlockSpec((ATTN_WIDTH, D_MODEL), lambda i: (0, 0)),
            pl.BlockSpec((LRU_WIDTH, D_MODEL), lambda i: (1, 0)),
            pl.BlockSpec((OUT_TM, D_MODEL), lambda i: (i, 0)),
            pl.BlockSpec((1, D_MODEL), lambda i: (0, 0)),
            pl.BlockSpec((1, D_MODEL), lambda i: (0, 0)),
        ],
        out_specs=pl.BlockSpec((OUT_TM, D_MODEL), lambda i: (i, 0)),
        out_shape=jax.ShapeDtypeStruct((SEQ, D_MODEL), F32),
        compiler_params=pltpu.CompilerParams(
            dimension_semantics=("arbitrary",), vmem_limit_bytes=VMEM_LIMIT_BYTES),
        name="out_proj",
    )(mixed_attn, mixed_lru, w_out_bf16, w_out_bf16, x2d, gate, final_gain)


@jax.jit
def kernel(x, c, norm_gain, w_ada, b_ada, w_in, conv_w, conv_b, w_rgate, b_rgate,
           w_igate, b_igate, lru_lambda, w_out, final_gain):
    assert x.shape == (1, SEQ, D_MODEL) and norm_gain.shape[0] == 1
    x2d = x.reshape(SEQ, D_MODEL)
    c_col = c.reshape(D_MODEL, 1)
    mod = _ada_mod(c_col, w_ada[0], b_ada)
    proj = _in_proj(x2d, norm_gain, mod, w_in[0])
    mixed_attn, gate = _attention(proj, c_col, w_ada[0], b_ada)
    w_gates, b_gates = _block_diag_gates(w_rgate[0], b_rgate[0], w_igate[0], b_igate[0])
    mixed_lru, w_out_bf16 = _lru(proj, conv_w[0], conv_b, w_gates, b_gates, lru_lambda, w_out[0])
    y = _out_proj(mixed_attn, mixed_lru, w_out_bf16, x2d, gate,
                  final_gain.reshape(1, D_MODEL))
    return y.reshape(1, SEQ, D_MODEL)
```

```python
import functools
import math

import jax
import jax.numpy as jnp
import numpy as np
from jax import lax
from jax.experimental import pallas as pl
from jax.experimental.pallas import tpu as pltpu

D_MODEL = 2048
SEQ = 8192
ATTN_WIDTH = D_MODEL // 2
LRU_WIDTH = D_MODEL - ATTN_WIDTH
HEAD_DIM = 64
N_HEADS = ATTN_WIDTH // HEAD_DIM
LRU_BLOCKS = 16
LRU_BLOCK_W = LRU_WIDTH // LRU_BLOCKS
CONV_WIDTH = 4
LRU_C = 8.0
DILATED_PATTERNS = ((128, 1), (512, 4), (2048, 16))
ATTN_SCALE = 1.0 / math.sqrt(HEAD_DIM)
NEG_INF = -1e30
LOG2E = math.log2(math.e)
EPS = 1e-6
PROJ_WIDTH = 4 * ATTN_WIDTH + 2 * LRU_WIDTH
MOD_SHIFT, MOD_SCALE, MOD_GATE = 0, 1, 2

LANES = 128
SUBLANES = 8
VMEM_LIMIT_BYTES = 56 * 1024 * 1024

ADA_TK = 256
PROJ_TM = 256
PROJ_W_CHUNK = 512
PROJ_W_SLOTS = 2
PROJ_VMEM_LIMIT_BYTES = 60 * 1024 * 1024
NORM_ROWS = 64
ATTN_QB = 128
ATTN_SB = 2048
ATTN_LOOKAHEAD = 1
ATTN_SLOTS = ATTN_LOOKAHEAD + 1
DEINT_ROWS = 256
DEINT_RATIO = 4
ATTN_VMEM_LIMIT_BYTES = 60 * 1024 * 1024
HEADS_PER_BLOCK = LANES // HEAD_DIM
COMBINE_ROWS = 256
LRU_CHUNK = 256
LRU_SEGMENTS = SUBLANES
LRU_SEG_LEN = SEQ // LRU_SEGMENTS
LRU_SEG_PITCH = LRU_SEG_LEN + SUBLANES
OUT_TM = 512

F32 = jnp.float32
BF16 = jnp.bfloat16


def _silu(x):
    h = 0.5 * x
    return h + h * jnp.tanh(h)


def _ada_kernel(c_ref, w_ref, b_ref, o_ref):
    @pl.when(pl.program_id(0) == 0)
    def _():
        o_ref[...] = b_ref[...]

    c_act = _silu(c_ref[...])
    o_ref[...] += jnp.sum(w_ref[...] * c_act, axis=0, keepdims=True)


def _ada_mod(c_col, w_ada, b_ada):
    n = w_ada.shape[1]
    return pl.pallas_call(
        _ada_kernel,
        grid=(D_MODEL // ADA_TK,),
        in_specs=[
            pl.BlockSpec((ADA_TK, 1), lambda k: (k, 0)),
            pl.BlockSpec((ADA_TK, n), lambda k: (k, 0)),
            pl.BlockSpec((1, n), lambda k: (0, 0)),
        ],
        out_specs=pl.BlockSpec((1, n), lambda k: (0, 0)),
        out_shape=jax.ShapeDtypeStruct((1, n), F32),
        compiler_params=pltpu.CompilerParams(
            dimension_semantics=("arbitrary",), vmem_limit_bytes=VMEM_LIMIT_BYTES),
        name="ada_mod",
    )(c_col, w_ada, b_ada)


def _w_chunk_copy(w_hbm, stage, sem, c):
    slot = c % PROJ_W_SLOTS
    cols = pl.ds(c * PROJ_W_CHUNK, PROJ_W_CHUNK)
    return pltpu.make_async_copy(w_hbm.at[:, cols], stage.at[slot], sem.at[slot])


def _proj_kernel(x0_ref, xn_ref, gain_ref, shift_ref, scale_ref, w_hbm, o_ref,
                 w_res, stage, h_even, h_odd, sem):
    s = pl.program_id(0)
    n_chunks = PROJ_WIDTH // PROJ_W_CHUNK
    slabs_per_chunk = PROJ_W_CHUNK // LANES

    def normalise(x_ref, h_ref):
        mul = gain_ref[...] * (1.0 + scale_ref[...])
        shift = shift_ref[...]
        for r0 in range(0, PROJ_TM, NORM_ROWS):
            xv = x_ref[pl.ds(r0, NORM_ROWS), :]
            var = jnp.mean(xv * xv, axis=-1, keepdims=True)
            h = xv * lax.rsqrt(var + EPS) * mul + shift
            h_ref[pl.ds(r0, NORM_ROWS), :] = h.astype(BF16)

    def project(h_ref, first_slab, n_slabs):
        cols = pl.ds(first_slab * LANES, n_slabs * LANES)
        acc = jnp.dot(h_ref[...], w_res[:, cols], preferred_element_type=F32)
        for b in range(n_slabs):
            o_ref[first_slab + b] = acc[:, b * LANES:(b + 1) * LANES]

    @pl.when(s == 0)
    def _():
        for c in range(PROJ_W_SLOTS):
            _w_chunk_copy(w_hbm, stage, sem, c).start()
        normalise(x0_ref, h_even)
        normalise(xn_ref, h_odd)
        for c in range(n_chunks):
            _w_chunk_copy(w_hbm, stage, sem, c).wait()
            cols = pl.ds(c * PROJ_W_CHUNK, PROJ_W_CHUNK)
            w_res[:, cols] = stage[c % PROJ_W_SLOTS].astype(BF16)
            if c + PROJ_W_SLOTS < n_chunks:
                _w_chunk_copy(w_hbm, stage, sem, c + PROJ_W_SLOTS).start()
            project(h_even, c * slabs_per_chunk, slabs_per_chunk)

    is_even = jnp.bitwise_and(s, 1) == 0

    @pl.when(jnp.logical_and(s > 0, is_even))
    def _():
        normalise(xn_ref, h_odd)
        project(h_even, 0, PROJ_WIDTH // LANES)

    @pl.when(jnp.logical_not(is_even))
    def _():
        normalise(xn_ref, h_even)
        project(h_odd, 0, PROJ_WIDTH // LANES)


def _in_proj(x2d, gain, mod, w_in):
    n_tiles = SEQ // PROJ_TM
    return pl.pallas_call(
        _proj_kernel,
        grid=(n_tiles,),
        in_specs=[
            pl.BlockSpec((PROJ_TM, D_MODEL), lambda s: (0, 0), pipeline_mode=pl.Buffered(1)),
            pl.BlockSpec((PROJ_TM, D_MODEL), lambda s: (jnp.minimum(s + 1, n_tiles - 1), 0)),
            pl.BlockSpec((1, D_MODEL), lambda s: (0, 0)),
            pl.BlockSpec((1, D_MODEL), lambda s: (0, MOD_SHIFT)),
            pl.BlockSpec((1, D_MODEL), lambda s: (0, MOD_SCALE)),
            pl.BlockSpec(memory_space=pl.ANY),
        ],
        out_specs=pl.BlockSpec((PROJ_WIDTH // LANES, PROJ_TM, LANES), lambda s: (0, s, 0)),
        out_shape=jax.ShapeDtypeStruct((PROJ_WIDTH // LANES, SEQ, LANES), F32),
        scratch_shapes=[
            pltpu.VMEM((D_MODEL, PROJ_WIDTH), BF16),
            pltpu.VMEM((PROJ_W_SLOTS, D_MODEL, PROJ_W_CHUNK), F32),
            pltpu.VMEM((PROJ_TM, D_MODEL), BF16),
            pltpu.VMEM((PROJ_TM, D_MODEL), BF16),
            pltpu.SemaphoreType.DMA((PROJ_W_SLOTS,)),
        ],
        compiler_params=pltpu.CompilerParams(
            dimension_semantics=("arbitrary",), vmem_limit_bytes=PROJ_VMEM_LIMIT_BYTES),
        name="in_proj",
    )(x2d, x2d, gain, mod, mod, w_in)


def _rows(start, size, stride):
    return pl.ds(start, size) if stride == 1 else pl.ds(start, size, stride=stride)


def _attn_kernel(q_ref, k_ref, v_ref, g_ref, coef_ref, o_ref,
                 kd, vd, tmp, bias_s, s_s, mx_s, m_s, l_s, acc_s):
    sb = pl.program_id(1)
    lane = lax.broadcasted_iota(jnp.int32, (ATTN_QB, LANES), 1)
    is_h0 = lane < HEAD_DIM

    @pl.when(sb == 0)
    def _():
        for src_ref, dst in ((k_ref, kd), (v_ref, vd)):
            for p, (_, d) in enumerate(DILATED_PATTERNS):
                dst[p, pl.ds(0, ATTN_QB), :] = jnp.zeros((ATTN_QB, LANES), BF16)
                d_prev = DILATED_PATTERNS[p - 1][1] if p else 1
                ratio = d // d_prev
                assert p == 0 and d == 1 or ratio == DEINT_RATIO
                per_residue = SEQ // d // DEINT_ROWS
                keep_f32 = 0 < p < len(DILATED_PATTERNS) - 1
                from_ref = src_ref if p <= 1 else tmp

                def deint(c, carry, p=p, d_prev=d_prev, ratio=ratio, per_residue=per_residue,
                          keep_f32=keep_f32, from_ref=from_ref, dst=dst):
                    r = lax.shift_right_logical(c, per_residue.bit_length() - 1)
                    chunk = jnp.bitwise_and(c, per_residue - 1)
                    r_prev = jnp.bitwise_and(r, d_prev - 1)
                    j = lax.shift_right_logical(r, d_prev.bit_length() - 1)
                    start = r_prev * (SEQ // d_prev) + chunk * (DEINT_ROWS * ratio) + j
                    x = from_ref[_rows(start, DEINT_ROWS, ratio), :]
                    if keep_f32:
                        tmp[pl.ds(pl.multiple_of(c * DEINT_ROWS, DEINT_ROWS), DEINT_ROWS), :] = x
                    rows = pl.ds(pl.multiple_of(ATTN_QB + c * DEINT_ROWS, ATTN_QB), DEINT_ROWS)
                    dst[p, rows, :] = x.astype(BF16)
                    return carry

                lax.fori_loop(0, SEQ // DEINT_ROWS, deint, 0, unroll=4)

        qi = lax.broadcasted_iota(jnp.int32, (ATTN_QB, 2 * ATTN_QB), 0)
        ki = lax.broadcasted_iota(jnp.int32, (ATTN_QB, 2 * ATTN_QB), 1)
        dist = qi + ATTN_QB - ki
        in_band = jnp.logical_and(dist >= 0, dist <= ATTN_QB)
        in_band_cur = jnp.logical_and(in_band, ki >= ATTN_QB)
        dist_f = dist.astype(F32)
        for p in range(len(DILATED_PATTERNS)):
            for h in range(HEADS_PER_BLOCK):
                pen = -(coef_ref[pl.program_id(0), p * HEADS_PER_BLOCK + h] * dist_f)
                rows = pl.ds(h * ATTN_QB, ATTN_QB)
                bias_s[p, 0, rows, :] = jnp.where(in_band, pen, NEG_INF)
                bias_s[p, 1, rows, :] = jnp.where(in_band_cur, pen, NEG_INF)

    ones = jnp.ones((2 * ATTN_QB, LANES), BF16)

    def place(d, ti):
        r, n = ti % d, ti // d
        blk = sb * (ATTN_SB // (ATTN_QB * d)) + n
        q_rows = _rows(r + n * (ATTN_QB * d), ATTN_QB, d)
        window = pl.ds(pl.multiple_of(r * (SEQ // d) + blk * ATTN_QB, ATTN_QB), 2 * ATTN_QB)
        return q_rows, window, blk == 0

    def scores(p, d, ti, slot):
        q_rows, window, first = place(d, ti)
        q = (q_ref[q_rows, :] * (ATTN_SCALE * LOG2E)).astype(BF16)
        zero = jnp.zeros_like(q)
        qs = jnp.concatenate([jnp.where(is_h0, q, zero), jnp.where(is_h0, zero, q)], axis=0)
        s = lax.dot_general(qs, kd[p, window, :], (((1,), (1,)), ((), ())),
                            preferred_element_type=F32)
        s = s + bias_s[p, first.astype(jnp.int32)]
        s_s[slot] = s
        mx_s[slot] = jnp.broadcast_to(jnp.max(s, axis=1, keepdims=True), (2 * ATTN_QB, LANES))

    def weighted(p, d, ti, slot):
        q_rows, window, _ = place(d, ti)
        mx = mx_s[slot]
        e = jnp.concatenate(
            [jnp.exp2(s_s[slot, :, :LANES] - mx), jnp.exp2(s_s[slot, :, LANES:] - mx)],
            axis=1).astype(BF16)
        vc = jnp.concatenate([vd[p, window, :], ones], axis=1)
        pv = jnp.dot(e, vc, preferred_element_type=F32)
        acc_s[p, q_rows, :] = jnp.where(is_h0, pv[:ATTN_QB, :LANES], pv[ATTN_QB:, :LANES])
        l_s[p, q_rows, :] = jnp.where(is_h0, pv[:ATTN_QB, LANES:], pv[ATTN_QB:, LANES:])
        m_s[p, q_rows, :] = jnp.where(is_h0, mx[:ATTN_QB], mx[ATTN_QB:])

    tiles = [(p, d, ti) for p, (_, d) in enumerate(DILATED_PATTERNS)
             for ti in range(ATTN_SB // ATTN_QB)]
    for k in range(min(ATTN_LOOKAHEAD, len(tiles))):
        scores(*tiles[k], k % ATTN_SLOTS)
    for k, tile in enumerate(tiles):
        if k + ATTN_LOOKAHEAD < len(tiles):
            scores(*tiles[k + ATTN_LOOKAHEAD], (k + ATTN_LOOKAHEAD) % ATTN_SLOTS)
        weighted(*tile, k % ATTN_SLOTS)

    def combine(i, carry):
        rows = pl.ds(pl.multiple_of(i * COMBINE_ROWS, COMBINE_ROWS), COMBINE_ROWS)
        m0, m1, m2 = m_s[0, rows, :], m_s[1, rows, :], m_s[2, rows, :]
        mm = jnp.maximum(jnp.maximum(m0, m1), m2)
        e0, e1, e2 = jnp.exp2(m0 - mm), jnp.exp2(m1 - mm), jnp.exp2(m2 - mm)
        num = e0 * acc_s[0, rows, :] + e1 * acc_s[1, rows, :] + e2 * acc_s[2, rows, :]
        den = e0 * l_s[0, rows, :] + e1 * l_s[1, rows, :] + e2 * l_s[2, rows, :]
        o_ref[rows, :] = (num / den * _silu(g_ref[rows, :])).astype(BF16)
        return carry

    lax.fori_loop(0, ATTN_SB // COMBINE_ROWS, combine, 0, unroll=2)


def _alibi_coefs():
    slopes = 2.0 ** (-8.0 * np.arange(1, N_HEADS + 1, dtype=np.float64) / N_HEADS)
    dil = np.array([d for _, d in DILATED_PATTERNS], dtype=np.float64)
    coef = slopes.reshape(-1, HEADS_PER_BLOCK)[:, None, :] * dil[None, :, None] * LOG2E
    return jnp.asarray(coef.reshape(N_HEADS // HEADS_PER_BLOCK, -1), dtype=F32)


def _attention(proj):
    n_hb = ATTN_WIDTH // LANES
    n_p = len(DILATED_PATTERNS)
    return pl.pallas_call(
        _attn_kernel,
        grid=(n_hb, SEQ // ATTN_SB),
        in_specs=[
            pl.BlockSpec((None, ATTN_SB, LANES), lambda h, s: (h, s, 0)),
            pl.BlockSpec((None, SEQ, LANES), lambda h, s: (n_hb + h, 0, 0)),
            pl.BlockSpec((None, SEQ, LANES), lambda h, s: (2 * n_hb + h, 0, 0)),
            pl.BlockSpec((None, ATTN_SB, LANES), lambda h, s: (3 * n_hb + h, s, 0)),
            pl.BlockSpec(memory_space=pltpu.SMEM),
        ],
        out_specs=pl.BlockSpec((ATTN_SB, LANES), lambda h, s: (s, h)),
        out_shape=jax.ShapeDtypeStruct((SEQ, ATTN_WIDTH), BF16),
        scratch_shapes=[
            pltpu.VMEM((n_p, ATTN_QB + SEQ, LANES), BF16),
            pltpu.VMEM((n_p, ATTN_QB + SEQ, LANES), BF16),
            pltpu.VMEM((SEQ, LANES), F32),
            pltpu.VMEM((n_p, 2, HEADS_PER_BLOCK * ATTN_QB, 2 * ATTN_QB), F32),
            pltpu.VMEM((ATTN_SLOTS, 2 * ATTN_QB, 2 * ATTN_QB), F32),
            pltpu.VMEM((ATTN_SLOTS, 2 * ATTN_QB, LANES), F32),
        ] + [pltpu.VMEM((n_p, ATTN_SB, LANES), F32)] * 3,
        compiler_params=pltpu.CompilerParams(
            dimension_semantics=("arbitrary", "arbitrary"),
            vmem_limit_bytes=ATTN_VMEM_LIMIT_BYTES),
        name="dilated_attn",
    )(proj, proj, proj, proj, _alibi_coefs())


def _lru_kernel(u_ref, g_ref, cw_ref, cb_ref, wg_ref, bg_ref, lam_ref, wout_ref,
                o_ref, wout_bf16_ref, ubuf, a_s, b_s, h_s, p_s):
    wout_bf16_ref[...] = wout_ref[...].astype(BF16)

    lam = lam_ref[...]
    log_sig = jnp.minimum(lam, 0.0) - jnp.log1p(jnp.exp(-jnp.abs(lam)))
    half_c_log_sig = (0.5 * LRU_C) * log_sig
    chunks_per_seg = LRU_SEG_LEN // LRU_CHUNK

    ubuf[pl.ds(0, SUBLANES), :] = jnp.zeros((SUBLANES, LANES), F32)

    def chunk(c, carry):
        rows = pl.ds(pl.multiple_of(c * LRU_CHUNK, LRU_CHUNK), LRU_CHUNK)
        ubuf[pl.ds(SUBLANES, LRU_CHUNK), :] = u_ref[rows, :]
        xc = cb_ref[...]
        for j in range(CONV_WIDTH):
            off = SUBLANES - (CONV_WIDTH - 1) + j
            xc = xc + ubuf[pl.ds(off, LRU_CHUNK), :] * cw_ref[pl.ds(j, 1), :]
        ubuf[pl.ds(0, SUBLANES), :] = ubuf[pl.ds(LRU_CHUNK, SUBLANES), :]

        z = jnp.dot(xc.astype(BF16), wg_ref[...], preferred_element_type=F32) + bg_ref[...]
        i = 0.5 * jnp.tanh(z[:, LANES:]) + 0.5
        log_a = half_c_log_sig * jnp.tanh(z[:, :LANES]) + half_c_log_sig
        a = jnp.exp(log_a)
        th = jnp.tanh(log_a)
        x2 = -2.0 * th / (1.0 - th)
        mult = jnp.where(x2 > 0.0, x2 * lax.rsqrt(x2), 0.0)
        b = mult * (i * xc)

        seg = lax.shift_right_logical(c, chunks_per_seg.bit_length() - 1)
        within = jnp.bitwise_and(c, chunks_per_seg - 1)
        dst = pl.ds(pl.multiple_of(seg * LRU_SEG_PITCH + within * LRU_CHUNK, SUBLANES), LRU_CHUNK)
        a_s[dst, :] = a
        b_s[dst, :] = b
        return carry

    lax.fori_loop(0, SEQ // LRU_CHUNK, chunk, 0, unroll=4)

    def step(t, carry):
        h, prod = carry
        rows = pl.ds(t, LRU_SEGMENTS, stride=LRU_SEG_PITCH)
        a = a_s[rows, :]
        h = a * h + b_s[rows, :]
        prod = prod * a
        h_s[rows, :] = h
        p_s[rows, :] = prod
        return h, prod

    h_end, p_end = lax.fori_loop(
        0, LRU_SEG_LEN, step,
        (jnp.zeros((LRU_SEGMENTS, LANES), F32), jnp.ones((LRU_SEGMENTS, LANES), F32)), unroll=8)

    seg_id = lax.broadcasted_iota(jnp.int32, (LRU_SEGMENTS, LANES), 0)
    c_in = jnp.zeros((LRU_SEGMENTS, LANES), F32)
    for _ in range(LRU_SEGMENTS - 1):
        c_in = jnp.where(seg_id == 0, 0.0, pltpu.roll(h_end + p_end * c_in, 1, 0))

    for s in range(LRU_SEGMENTS):
        c_s = c_in[s:s + 1, :]

        def gate(k, carry, s=s, c_s=c_s):
            off = pl.multiple_of(k * LRU_CHUNK, LRU_CHUNK)
            src = pl.ds(s * LRU_SEG_PITCH + off, LRU_CHUNK)
            rows = pl.ds(s * LRU_SEG_LEN + off, LRU_CHUNK)
            h = h_s[src, :] + p_s[src, :] * c_s
            o_ref[rows, :] = (h * _silu(g_ref[rows, :])).astype(BF16)
            return carry

        lax.fori_loop(0, chunks_per_seg, gate, 0)


def _lru(proj, conv_w, conv_b, w_gates, b_gates, lam, w_out):
    n_cb = LRU_WIDTH // LANES
    u_col0 = 4 * ATTN_WIDTH // LANES
    g_col0 = u_col0 + n_cb
    return pl.pallas_call(
        _lru_kernel,
        grid=(n_cb,),
        in_specs=[
            pl.BlockSpec((None, SEQ, LANES), lambda j: (u_col0 + j, 0, 0)),
            pl.BlockSpec((None, SEQ, LANES), lambda j: (g_col0 + j, 0, 0)),
            pl.BlockSpec((CONV_WIDTH, LANES), lambda j: (0, j)),
            pl.BlockSpec((1, LANES), lambda j: (0, j)),
            pl.BlockSpec((None, LANES, 2 * LANES), lambda j: (j, 0, 0)),
            pl.BlockSpec((None, 1, 2 * LANES), lambda j: (j, 0, 0)),
            pl.BlockSpec((1, LANES), lambda j: (0, j)),
            pl.BlockSpec((D_MODEL // n_cb, D_MODEL), lambda j: (j, 0)),
        ],
        out_specs=[
            pl.BlockSpec((SEQ, LANES), lambda j: (0, j)),
            pl.BlockSpec((D_MODEL // n_cb, D_MODEL), lambda j: (j, 0)),
        ],
        out_shape=[
            jax.ShapeDtypeStruct((SEQ, LRU_WIDTH), BF16),
            jax.ShapeDtypeStruct((D_MODEL, D_MODEL), BF16),
        ],
        scratch_shapes=[
            pltpu.VMEM((LRU_CHUNK + SUBLANES, LANES), F32),
        ] + [pltpu.VMEM((LRU_SEGMENTS * LRU_SEG_PITCH, LANES), F32)] * 4,
        compiler_params=pltpu.CompilerParams(
            dimension_semantics=("arbitrary",), vmem_limit_bytes=VMEM_LIMIT_BYTES),
        name="rglru",
    )(proj, proj, conv_w, conv_b, w_gates, b_gates, lam, w_out)


def _block_diag_gates(w_rgate, b_rgate, w_igate, b_igate):
    per = LANES // LRU_BLOCK_W
    n_cb = LRU_WIDTH // LANES

    def bd(w):
        w = w.reshape(n_cb, per, LRU_BLOCK_W, LRU_BLOCK_W)
        eye = jnp.eye(per, dtype=w.dtype)
        return jnp.einsum('cpkj,pq->cpkqj', w, eye).reshape(n_cb, LANES, LANES)

    w = (0.5 * jnp.concatenate([bd(w_rgate), bd(w_igate)], axis=-1)).astype(BF16)
    b = 0.5 * jnp.concatenate(
        [b_rgate.reshape(n_cb, 1, LANES), b_igate.reshape(n_cb, 1, LANES)], axis=-1)
    return w, b


def _out_kernel(ma_ref, ml_ref, wa_ref, wl_ref, x_ref, gate_ref, fg_ref, o_ref):
    mix = jnp.dot(ma_ref[...], wa_ref[...], preferred_element_type=F32)
    mix = mix + jnp.dot(ml_ref[...], wl_ref[...], preferred_element_type=F32)
    y = x_ref[...] + gate_ref[...] * mix
    var = jnp.mean(y * y, axis=-1, keepdims=True)
    o_ref[...] = y * lax.rsqrt(var + EPS) * fg_ref[...]


def _out_proj(mixed_attn, mixed_lru, w_out_bf16, x2d, mod, final_gain):
    return pl.pallas_call(
        _out_kernel,
        grid=(SEQ // OUT_TM,),
        in_specs=[
            pl.BlockSpec((OUT_TM, ATTN_WIDTH), lambda i: (i, 0)),
            pl.BlockSpec((OUT_TM, LRU_WIDTH), lambda i: (i, 0)),
            pl.BlockSpec((ATTN_WIDTH, D_MODEL), lambda i: (0, 0)),
            pl.BlockSpec((LRU_WIDTH, D_MODEL), lambda i: (1, 0)),
            pl.BlockSpec((OUT_TM, D_MODEL), lambda i: (i, 0)),
            pl.BlockSpec((1, D_MODEL), lambda i: (0, MOD_GATE)),
            pl.BlockSpec((1, D_MODEL), lambda i: (0, 0)),
        ],
        out_specs=pl.BlockSpec((OUT_TM, D_MODEL), lambda i: (i, 0)),
        out_shape=jax.ShapeDtypeStruct((SEQ, D_MODEL), F32),
        compiler_params=pltpu.CompilerParams(
            dimension_semantics=("arbitrary",), vmem_limit_bytes=VMEM_LIMIT_BYTES),
        name="out_proj",
    )(mixed_attn, mixed_lru, w_out_bf16, w_out_bf16, x2d, mod, final_gain)


@jax.jit
def kernel(x, c, norm_gain, w_ada, b_ada, w_in, conv_w, conv_b, w_rgate, b_rgate,
           w_igate, b_igate, lru_lambda, w_out, final_gain):
    assert x.shape == (1, SEQ, D_MODEL) and norm_gain.shape[0] == 1
    x2d = x.reshape(SEQ, D_MODEL)
    mod = _ada_mod(c.reshape(D_MODEL, 1), w_ada[0], b_ada)
    proj = _in_proj(x2d, norm_gain, mod, w_in[0])
    mixed_attn = _attention(proj)
    w_gates, b_gates = _block_diag_gates(w_rgate[0], b_rgate[0], w_igate[0], b_igate[0])
    mixed_lru, w_out_bf16 = _lru(proj, conv_w[0], conv_b, w_gates, b_gates, lru_lambda, w_out[0])
    y = _out_proj(mixed_attn, mixed_lru, w_out_bf16, x2d, mod,
                  final_gain.reshape(1, D_MODEL))
    return y.reshape(1, SEQ, D_MODEL)
```

```python
import functools
import math

import jax
import jax.numpy as jnp
import numpy as np
from jax import lax
from jax.experimental import pallas as pl
from jax.experimental.pallas import tpu as pltpu

D_MODEL = 2048
SEQ = 8192
ATTN_WIDTH = D_MODEL // 2
LRU_WIDTH = D_MODEL - ATTN_WIDTH
HEAD_DIM = 64
N_HEADS = ATTN_WIDTH // HEAD_DIM
LRU_BLOCKS = 16
LRU_BLOCK_W = LRU_WIDTH // LRU_BLOCKS
CONV_WIDTH = 4
LRU_C = 8.0
DILATED_PATTERNS = ((128, 1), (512, 4), (2048, 16))
ATTN_SCALE = 1.0 / math.sqrt(HEAD_DIM)
NEG_INF = -1e30
LOG2E = math.log2(math.e)
EPS = 1e-6
PROJ_WIDTH = 4 * ATTN_WIDTH + 2 * LRU_WIDTH
MOD_SHIFT, MOD_SCALE, MOD_GATE = 0, 1, 2

LANES = 128
SUBLANES = 8
VMEM_LIMIT_BYTES = 56 * 1024 * 1024

ADA_TK = 256
PROJ_TM = 256
PROJ_W_CHUNK = 512
PROJ_W_SLOTS = 2
PROJ_VMEM_LIMIT_BYTES = 60 * 1024 * 1024
NORM_ROWS = 64
ATTN_QB = 128
ATTN_SB = 2048
ATTN_LOOKAHEAD = 1
ATTN_SLOTS = ATTN_LOOKAHEAD + 1
DEINT_ROWS = 256
DEINT_RATIO = 4
ATTN_VMEM_LIMIT_BYTES = 60 * 1024 * 1024
HEADS_PER_BLOCK = LANES // HEAD_DIM
COMBINE_ROWS = 256
LRU_CHUNK = 256
LRU_SEGMENTS = SUBLANES
LRU_SEG_LEN = SEQ // LRU_SEGMENTS
LRU_SEG_PITCH = LRU_SEG_LEN + SUBLANES
OUT_TM = 512

F32 = jnp.float32
BF16 = jnp.bfloat16


def _silu(x):
    h = 0.5 * x
    return h + h * jnp.tanh(h)


def _ada_kernel(c_ref, w_ref, b_ref, o_ref):
    @pl.when(pl.program_id(0) == 0)
    def _():
        o_ref[...] = b_ref[...]

    c_act = _silu(c_ref[...])
    o_ref[...] += jnp.sum(w_ref[...] * c_act, axis=0, keepdims=True)


def _ada_mod(c_col, w_ada, b_ada):
    n = w_ada.shape[1]
    return pl.pallas_call(
        _ada_kernel,
        grid=(D_MODEL // ADA_TK,),
        in_specs=[
            pl.BlockSpec((ADA_TK, 1), lambda k: (k, 0)),
            pl.BlockSpec((ADA_TK, n), lambda k: (k, 0)),
            pl.BlockSpec((1, n), lambda k: (0, 0)),
        ],
        out_specs=pl.BlockSpec((1, n), lambda k: (0, 0)),
        out_shape=jax.ShapeDtypeStruct((1, n), F32),
        compiler_params=pltpu.CompilerParams(
            dimension_semantics=("arbitrary",), vmem_limit_bytes=VMEM_LIMIT_BYTES),
        name="ada_mod",
    )(c_col, w_ada, b_ada)


def _w_chunk_copy(w_hbm, stage, sem, c):
    slot = c % PROJ_W_SLOTS
    cols = pl.ds(c * PROJ_W_CHUNK, PROJ_W_CHUNK)
    return pltpu.make_async_copy(w_hbm.at[:, cols], stage.at[slot], sem.at[slot])


def _proj_kernel(x0_ref, xn_ref, gain_ref, shift_ref, scale_ref, w_hbm, o_ref,
                 w_res, stage, h_even, h_odd, sem):
    s = pl.program_id(0)
    n_chunks = PROJ_WIDTH // PROJ_W_CHUNK
    slabs_per_chunk = PROJ_W_CHUNK // LANES

    def normalise(x_ref, h_ref):
        mul = gain_ref[...] * (1.0 + scale_ref[...])
        shift = shift_ref[...]
        for r0 in range(0, PROJ_TM, NORM_ROWS):
            xv = x_ref[pl.ds(r0, NORM_ROWS), :]
            var = jnp.mean(xv * xv, axis=-1, keepdims=True)
            h = xv * lax.rsqrt(var + EPS) * mul + shift
            h_ref[pl.ds(r0, NORM_ROWS), :] = h.astype(BF16)

    def project(h_ref, first_slab, n_slabs):
        cols = pl.ds(first_slab * LANES, n_slabs * LANES)
        acc = jnp.dot(h_ref[...], w_res[:, cols], preferred_element_type=F32)
        for b in range(n_slabs):
            o_ref[first_slab + b] = acc[:, b * LANES:(b + 1) * LANES]

    @pl.when(s == 0)
    def _():
        for c in range(PROJ_W_SLOTS):
            _w_chunk_copy(w_hbm, stage, sem, c).start()
        normalise(x0_ref, h_even)
        normalise(xn_ref, h_odd)
        for c in range(n_chunks):
            _w_chunk_copy(w_hbm, stage, sem, c).wait()
            cols = pl.ds(c * PROJ_W_CHUNK, PROJ_W_CHUNK)
            w_res[:, cols] = stage[c % PROJ_W_SLOTS].astype(BF16)
            if c + PROJ_W_SLOTS < n_chunks:
                _w_chunk_copy(w_hbm, stage, sem, c + PROJ_W_SLOTS).start()
            project(h_even, c * slabs_per_chunk, slabs_per_chunk)

    is_even = jnp.bitwise_and(s, 1) == 0

    @pl.when(jnp.logical_and(s > 0, is_even))
    def _():
        normalise(xn_ref, h_odd)
        project(h_even, 0, PROJ_WIDTH // LANES)

    @pl.when(jnp.logical_not(is_even))
    def _():
        normalise(xn_ref, h_even)
        project(h_odd, 0, PROJ_WIDTH // LANES)


def _in_proj(x2d, gain, mod, w_in):
    n_tiles = SEQ // PROJ_TM
    return pl.pallas_call(
        _proj_kernel,
        grid=(n_tiles,),
        in_specs=[
            pl.BlockSpec((PROJ_TM, D_MODEL), lambda s: (0, 0), pipeline_mode=pl.Buffered(1)),
            pl.BlockSpec((PROJ_TM, D_MODEL), lambda s: (jnp.minimum(s + 1, n_tiles - 1), 0)),
            pl.BlockSpec((1, D_MODEL), lambda s: (0, 0)),
            pl.BlockSpec((1, D_MODEL), lambda s: (0, MOD_SHIFT)),
            pl.BlockSpec((1, D_MODEL), lambda s: (0, MOD_SCALE)),
            pl.BlockSpec(memory_space=pl.ANY),
        ],
        out_specs=pl.BlockSpec((PROJ_WIDTH // LANES, PROJ_TM, LANES), lambda s: (0, s, 0)),
        out_shape=jax.ShapeDtypeStruct((PROJ_WIDTH // LANES, SEQ, LANES), F32),
        scratch_shapes=[
            pltpu.VMEM((D_MODEL, PROJ_WIDTH), BF16),
            pltpu.VMEM((PROJ_W_SLOTS, D_MODEL, PROJ_W_CHUNK), F32),
            pltpu.VMEM((PROJ_TM, D_MODEL), BF16),
            pltpu.VMEM((PROJ_TM, D_MODEL), BF16),
            pltpu.SemaphoreType.DMA((PROJ_W_SLOTS,)),
        ],
        compiler_params=pltpu.CompilerParams(
            dimension_semantics=("arbitrary",), vmem_limit_bytes=PROJ_VMEM_LIMIT_BYTES),
        name="in_proj",
    )(x2d, x2d, gain, mod, mod, w_in)


def _rows(start, size, stride):
    return pl.ds(start, size) if stride == 1 else pl.ds(start, size, stride=stride)


def _attn_kernel(q_ref, k_ref, v_ref, g_ref, coef_ref, o_ref,
                 kd, vd, tmp, bias_s, s_s, mx_s, m_s, l_s, acc_s):
    sb = pl.program_id(1)
    lane = lax.broadcasted_iota(jnp.int32, (ATTN_QB, LANES), 1)
    is_h0 = lane < HEAD_DIM

    @pl.when(sb == 0)
    def _():
        for src_ref, dst in ((k_ref, kd), (v_ref, vd)):
            for p, (_, d) in enumerate(DILATED_PATTERNS):
                dst[p, pl.ds(0, ATTN_QB), :] = jnp.zeros((ATTN_QB, LANES), BF16)
                d_prev = DILATED_PATTERNS[p - 1][1] if p else 1
                ratio = d // d_prev
                assert p == 0 and d == 1 or ratio == DEINT_RATIO
                per_residue = SEQ // d // DEINT_ROWS
                keep_f32 = 0 < p < len(DILATED_PATTERNS) - 1
                from_ref = src_ref if p <= 1 else tmp

                def deint(c, carry, p=p, d_prev=d_prev, ratio=ratio, per_residue=per_residue,
                          keep_f32=keep_f32, from_ref=from_ref, dst=dst):
                    r = lax.shift_right_logical(c, per_residue.bit_length() - 1)
                    chunk = jnp.bitwise_and(c, per_residue - 1)
                    r_prev = jnp.bitwise_and(r, d_prev - 1)
                    j = lax.shift_right_logical(r, d_prev.bit_length() - 1)
                    start = r_prev * (SEQ // d_prev) + chunk * (DEINT_ROWS * ratio) + j
                    x = from_ref[_rows(start, DEINT_ROWS, ratio), :]
                    if keep_f32:
                        tmp[pl.ds(pl.multiple_of(c * DEINT_ROWS, DEINT_ROWS), DEINT_ROWS), :] = x
                    rows = pl.ds(pl.multiple_of(ATTN_QB + c * DEINT_ROWS, ATTN_QB), DEINT_ROWS)
                    dst[p, rows, :] = x.astype(BF16)
                    return carry

                lax.fori_loop(0, SEQ // DEINT_ROWS, deint, 0, unroll=4)

        qi = lax.broadcasted_iota(jnp.int32, (ATTN_QB, 2 * ATTN_QB), 0)
        ki = lax.broadcasted_iota(jnp.int32, (ATTN_QB, 2 * ATTN_QB), 1)
        dist = qi + ATTN_QB - ki
        in_band = jnp.logical_and(dist >= 0, dist <= ATTN_QB)
        in_band_cur = jnp.logical_and(in_band, ki >= ATTN_QB)
        dist_f = dist.astype(F32)
        for p in range(len(DILATED_PATTERNS)):
            for h in range(HEADS_PER_BLOCK):
                pen = -(coef_ref[pl.program_id(0), p * HEADS_PER_BLOCK + h] * dist_f)
                rows = pl.ds(h * ATTN_QB, ATTN_QB)
                bias_s[p, 0, rows, :] = jnp.where(in_band, pen, NEG_INF)
                bias_s[p, 1, rows, :] = jnp.where(in_band_cur, pen, NEG_INF)

    ones = jnp.ones((2 * ATTN_QB, LANES), BF16)

    def place(d, ti):
        r, n = ti % d, ti // d
        blk = sb * (ATTN_SB // (ATTN_QB * d)) + n
        q_rows = _rows(r + n * (ATTN_QB * d), ATTN_QB, d)
        window = pl.ds(pl.multiple_of(r * (SEQ // d) + blk * ATTN_QB, ATTN_QB), 2 * ATTN_QB)
        return q_rows, window, blk == 0

    def scores(p, d, ti, slot):
        q_rows, window, first = place(d, ti)
        q = (q_ref[q_rows, :] * (ATTN_SCALE * LOG2E)).astype(BF16)
        zero = jnp.zeros_like(q)
        qs = jnp.concatenate([jnp.where(is_h0, q, zero), jnp.where(is_h0, zero, q)], axis=0)
        s = lax.dot_general(qs, kd[p, window, :], (((1,), (1,)), ((), ())),
                            preferred_element_type=F32)
        s = s + bias_s[p, first.astype(jnp.int32)]
        s_s[slot] = s
        mx_s[slot] = jnp.broadcast_to(jnp.max(s, axis=1, keepdims=True), (2 * ATTN_QB, LANES))

    def weighted(p, d, ti, slot):
        q_rows, window, _ = place(d, ti)
        mx = mx_s[slot]
        e = jnp.concatenate(
            [jnp.exp2(s_s[slot, :, :LANES] - mx), jnp.exp2(s_s[slot, :, LANES:] - mx)],
            axis=1).astype(BF16)
        vc = jnp.concatenate([vd[p, window, :], ones], axis=1)
        pv = jnp.dot(e, vc, preferred_element_type=F32)
        acc_s[p, q_rows, :] = jnp.where(is_h0, pv[:ATTN_QB, :LANES], pv[ATTN_QB:, :LANES])
        l_s[p, q_rows, :] = jnp.where(is_h0, pv[:ATTN_QB, LANES:], pv[ATTN_QB:, LANES:])
        m_s[p, q_rows, :] = jnp.where(is_h0, mx[:ATTN_QB], mx[ATTN_QB:])

    tiles = [(p, d, ti) for p, (_, d) in enumerate(DILATED_PATTERNS)
             for ti in range(ATTN_SB // ATTN_QB)]
    for k in range(min(ATTN_LOOKAHEAD, len(tiles))):
        scores(*tiles[k], k % ATTN_SLOTS)
    for k, tile in enumerate(tiles):
        if k + ATTN_LOOKAHEAD < len(tiles):
            scores(*tiles[k + ATTN_LOOKAHEAD], (k + ATTN_LOOKAHEAD) % ATTN_SLOTS)
        weighted(*tile, k % ATTN_SLOTS)

    def combine(i, carry):
        rows = pl.ds(pl.multiple_of(i * COMBINE_ROWS, COMBINE_ROWS), COMBINE_ROWS)
        m0, m1, m2 = m_s[0, rows, :], m_s[1, rows, :], m_s[2, rows, :]
        mm = jnp.maximum(jnp.maximum(m0, m1), m2)
        e0, e1, e2 = jnp.exp2(m0 - mm), jnp.exp2(m1 - mm), jnp.exp2(m2 - mm)
        num = e0 * acc_s[0, rows, :] + e1 * acc_s[1, rows, :] + e2 * acc_s[2, rows, :]
        den = e0 * l_s[0, rows, :] + e1 * l_s[1, rows, :] + e2 * l_s[2, rows, :]
        o_ref[rows, :] = (num / den * _silu(g_ref[rows, :])).astype(BF16)
        return carry

    lax.fori_loop(0, ATTN_SB // COMBINE_ROWS, combine, 0, unroll=2)


def _alibi_coefs():
    slopes = 2.0 ** (-8.0 * np.arange(1, N_HEADS + 1, dtype=np.float64) / N_HEADS)
    dil = np.array([d for _, d in DILATED_PATTERNS], dtype=np.float64)
    coef = slopes.reshape(-1, HEADS_PER_BLOCK)[:, None, :] * dil[None, :, None] * LOG2E
    return jnp.asarray(coef.reshape(N_HEADS // HEADS_PER_BLOCK, -1), dtype=F32)


def _attention(proj):
    n_hb = ATTN_WIDTH // LANES
    n_p = len(DILATED_PATTERNS)
    return pl.pallas_call(
        _attn_kernel,
        grid=(n_hb, SEQ // ATTN_SB),
        in_specs=[
            pl.BlockSpec((None, ATTN_SB, LANES), lambda h, s: (h, s, 0)),
            pl.BlockSpec((None, SEQ, LANES), lambda h, s: (n_hb + h, 0, 0)),
            pl.BlockSpec((None, SEQ, LANES), lambda h, s: (2 * n_hb + h, 0, 0)),
            pl.BlockSpec((None, ATTN_SB, LANES), lambda h, s: (3 * n_hb + h, s, 0)),
            pl.BlockSpec(memory_space=pltpu.SMEM),
        ],
        out_specs=pl.BlockSpec((ATTN_SB, LANES), lambda h, s: (s, h)),
        out_shape=jax.ShapeDtypeStruct((SEQ, ATTN_WIDTH), BF16),
        scratch_shapes=[
            pltpu.VMEM((n_p, ATTN_QB + SEQ, LANES), BF16),
            pltpu.VMEM((n_p, ATTN_QB + SEQ, LANES), BF16),
            pltpu.VMEM((SEQ, LANES), F32),
            pltpu.VMEM((n_p, 2, HEADS_PER_BLOCK * ATTN_QB, 2 * ATTN_QB), F32),
            pltpu.VMEM((ATTN_SLOTS, 2 * ATTN_QB, 2 * ATTN_QB), F32),
            pltpu.VMEM((ATTN_SLOTS, 2 * ATTN_QB, LANES), F32),
        ] + [pltpu.VMEM((n_p, ATTN_SB, LANES), F32)] * 3,
        compiler_params=pltpu.CompilerParams(
            dimension_semantics=("arbitrary", "arbitrary"),
            vmem_limit_bytes=ATTN_VMEM_LIMIT_BYTES),
        name="dilated_attn",
    )(proj, proj, proj, proj, _alibi_coefs())


def _lru_kernel(u_ref, g_ref, cw_ref, cb_ref, wg_ref, bg_ref, lam_ref, wout_ref,
                o_ref, wout_bf16_ref, ubuf, a_s, b_s, h_s, p_s):
    wout_bf16_ref[...] = wout_ref[...].astype(BF16)

    lam = lam_ref[...]
    log_sig = jnp.minimum(lam, 0.0) - jnp.log1p(jnp.exp(-jnp.abs(lam)))
    half_c_log_sig = (0.5 * LRU_C) * log_sig
    chunks_per_seg = LRU_SEG_LEN // LRU_CHUNK

    ubuf[pl.ds(0, SUBLANES), :] = jnp.zeros((SUBLANES, LANES), F32)

    def chunk(c, carry):
        rows = pl.ds(pl.multiple_of(c * LRU_CHUNK, LRU_CHUNK), LRU_CHUNK)
        ubuf[pl.ds(SUBLANES, LRU_CHUNK), :] = u_ref[rows, :]
        xc = cb_ref[...]
        for j in range(CONV_WIDTH):
            off = SUBLANES - (CONV_WIDTH - 1) + j
            xc = xc + ubuf[pl.ds(off, LRU_CHUNK), :] * cw_ref[pl.ds(j, 1), :]
        ubuf[pl.ds(0, SUBLANES), :] = ubuf[pl.ds(LRU_CHUNK, SUBLANES), :]

        z = jnp.dot(xc.astype(BF16), wg_ref[...], preferred_element_type=F32) + bg_ref[...]
        i = 0.5 * jnp.tanh(z[:, LANES:]) + 0.5
        log_a = half_c_log_sig * jnp.tanh(z[:, :LANES]) + half_c_log_sig
        a = jnp.exp(log_a)
        th = jnp.tanh(log_a)
        x2 = -2.0 * th / (1.0 - th)
        mult = jnp.where(x2 > 0.0, x2 * lax.rsqrt(x2), 0.0)
        b = mult * (i * xc)

        seg = lax.shift_right_logical(c, chunks_per_seg.bit_length() - 1)
        within = jnp.bitwise_and(c, chunks_per_seg - 1)
        dst = pl.ds(pl.multiple_of(seg * LRU_SEG_PITCH + within * LRU_CHUNK, SUBLANES), LRU_CHUNK)
        a_s[dst, :] = a
        b_s[dst, :] = b
        return carry

    lax.fori_loop(0, SEQ // LRU_CHUNK, chunk, 0, unroll=True)

    def step(t, carry):
        h, prod = carry
        rows = pl.ds(t, LRU_SEGMENTS, stride=LRU_SEG_PITCH)
        a = a_s[rows, :]
        h = a * h + b_s[rows, :]
        prod = prod * a
        h_s[rows, :] = h
        p_s[rows, :] = prod
        return h, prod

    h_end, p_end = lax.fori_loop(
        0, LRU_SEG_LEN, step,
        (jnp.zeros((LRU_SEGMENTS, LANES), F32), jnp.ones((LRU_SEGMENTS, LANES), F32)), unroll=16)

    seg_id = lax.broadcasted_iota(jnp.int32, (LRU_SEGMENTS, LANES), 0)
    c_in = jnp.zeros((LRU_SEGMENTS, LANES), F32)
    for _ in range(LRU_SEGMENTS - 1):
        c_in = jnp.where(seg_id == 0, 0.0, pltpu.roll(h_end + p_end * c_in, 1, 0))

    for s in range(LRU_SEGMENTS):
        c_s = c_in[s:s + 1, :]

        def gate(k, carry, s=s, c_s=c_s):
            off = pl.multiple_of(k * LRU_CHUNK, LRU_CHUNK)
            src = pl.ds(s * LRU_SEG_PITCH + off, LRU_CHUNK)
            rows = pl.ds(s * LRU_SEG_LEN + off, LRU_CHUNK)
            h = h_s[src, :] + p_s[src, :] * c_s
            o_ref[rows, :] = (h * _silu(g_ref[rows, :])).astype(BF16)
            return carry

        lax.fori_loop(0, chunks_per_seg, gate, 0, unroll=True)


def _lru(proj, conv_w, conv_b, w_gates, b_gates, lam, w_out):
    n_cb = LRU_WIDTH // LANES
    u_col0 = 4 * ATTN_WIDTH // LANES
    g_col0 = u_col0 + n_cb
    return pl.pallas_call(
        _lru_kernel,
        grid=(n_cb,),
        in_specs=[
            pl.BlockSpec((None, SEQ, LANES), lambda j: (u_col0 + j, 0, 0)),
            pl.BlockSpec((None, SEQ, LANES), lambda j: (g_col0 + j, 0, 0)),
            pl.BlockSpec((CONV_WIDTH, LANES), lambda j: (0, j)),
            pl.BlockSpec((1, LANES), lambda j: (0, j)),
            pl.BlockSpec((None, LANES, 2 * LANES), lambda j: (j, 0, 0)),
            pl.BlockSpec((None, 1, 2 * LANES), lambda j: (j, 0, 0)),
            pl.BlockSpec((1, LANES), lambda j: (0, j)),
            pl.BlockSpec((D_MODEL // n_cb, D_MODEL), lambda j: (j, 0)),
        ],
        out_specs=[
            pl.BlockSpec((SEQ, LANES), lambda j: (0, j)),
            pl.BlockSpec((D_MODEL // n_cb, D_MODEL), lambda j: (j, 0)),
        ],
        out_shape=[
            jax.ShapeDtypeStruct((SEQ, LRU_WIDTH), BF16),
            jax.ShapeDtypeStruct((D_MODEL, D_MODEL), BF16),
        ],
        scratch_shapes=[
            pltpu.VMEM((LRU_CHUNK + SUBLANES, LANES), F32),
        ] + [pltpu.VMEM((LRU_SEGMENTS * LRU_SEG_PITCH, LANES), F32)] * 4,
        compiler_params=pltpu.CompilerParams(
            dimension_semantics=("arbitrary",), vmem_limit_bytes=VMEM_LIMIT_BYTES),
        name="rglru",
    )(proj, proj, conv_w, conv_b, w_gates, b_gates, lam, w_out)


def _block_diag_gates(w_rgate, b_rgate, w_igate, b_igate):
    per = LANES // LRU_BLOCK_W
    n_cb = LRU_WIDTH // LANES

    def bd(w):
        w = w.reshape(n_cb, per, LRU_BLOCK_W, LRU_BLOCK_W)
        eye = jnp.eye(per, dtype=w.dtype)
        return jnp.einsum('cpkj,pq->cpkqj', w, eye).reshape(n_cb, LANES, LANES)

    w = (0.5 * jnp.concatenate([bd(w_rgate), bd(w_igate)], axis=-1)).astype(BF16)
    b = 0.5 * jnp.concatenate(
        [b_rgate.reshape(n_cb, 1, LANES), b_igate.reshape(n_cb, 1, LANES)], axis=-1)
    return w, b


def _out_kernel(ma_ref, ml_ref, wa_ref, wl_ref, x_ref, gate_ref, fg_ref, o_ref):
    mix = jnp.dot(ma_ref[...], wa_ref[...], preferred_element_type=F32)
    mix = mix + jnp.dot(ml_ref[...], wl_ref[...], preferred_element_type=F32)
    y = x_ref[...] + gate_ref[...] * mix
    var = jnp.mean(y * y, axis=-1, keepdims=True)
    o_ref[...] = y * lax.rsqrt(var + EPS) * fg_ref[...]


def _out_proj(mixed_attn, mixed_lru, w_out_bf16, x2d, mod, final_gain):
    return pl.pallas_call(
        _out_kernel,
        grid=(SEQ // OUT_TM,),
        in_specs=[
            pl.BlockSpec((OUT_TM, ATTN_WIDTH), lambda i: (i, 0)),
            pl.BlockSpec((OUT_TM, LRU_WIDTH), lambda i: (i, 0)),
            pl.BlockSpec((ATTN_WIDTH, D_MODEL), lambda i: (0, 0)),
            pl.BlockSpec((LRU_WIDTH, D_MODEL), lambda i: (1, 0)),
            pl.BlockSpec((OUT_TM, D_MODEL), lambda i: (i, 0)),
            pl.BlockSpec((1, D_MODEL), lambda i: (0, MOD_GATE)),
            pl.BlockSpec((1, D_MODEL), lambda i: (0, 0)),
        ],
        out_specs=pl.BlockSpec((OUT_TM, D_MODEL), lambda i: (i, 0)),
        out_shape=jax.ShapeDtypeStruct((SEQ, D_MODEL), F32),
        compiler_params=pltpu.CompilerParams(
            dimension_semantics=("arbitrary",), vmem_limit_bytes=VMEM_LIMIT_BYTES),
        name="out_proj",
    )(mixed_attn, mixed_lru, w_out_bf16, w_out_bf16, x2d, mod, final_gain)


@jax.jit
def kernel(x, c, norm_gain, w_ada, b_ada, w_in, conv_w, conv_b, w_rgate, b_rgate,
           w_igate, b_igate, lru_lambda, w_out, final_gain):
    assert x.shape == (1, SEQ, D_MODEL) and norm_gain.shape[0] == 1
    x2d = x.reshape(SEQ, D_MODEL)
    mod = _ada_mod(c.reshape(D_MODEL, 1), w_ada[0], b_ada)
    proj = _in_proj(x2d, norm_gain, mod, w_in[0])
    mixed_attn = _attention(proj)
    w_gates, b_gates = _block_diag_gates(w_rgate[0], b_rgate[0], w_igate[0], b_igate[0])
    mixed_lru, w_out_bf16 = _lru(proj, conv_w[0], conv_b, w_gates, b_gates, lru_lambda, w_out[0])
    y = _out_proj(mixed_attn, mixed_lru, w_out_bf16, x2d, mod,
                  final_gain.reshape(1, D_MODEL))
    return y.reshape(1, SEQ, D_MODEL)
```

```python
import functools
import math

import jax
import jax.numpy as jnp
import numpy as np
from jax import lax
from jax.experimental import pallas as pl
from jax.experimental.pallas import tpu as pltpu

D_MODEL = 2048
SEQ = 8192
ATTN_WIDTH = D_MODEL // 2
LRU_WIDTH = D_MODEL - ATTN_WIDTH
HEAD_DIM = 64
N_HEADS = ATTN_WIDTH // HEAD_DIM
LRU_BLOCKS = 16
LRU_BLOCK_W = LRU_WIDTH // LRU_BLOCKS
CONV_WIDTH = 4
LRU_C = 8.0
DILATED_PATTERNS = ((128, 1), (512, 4), (2048, 16))
ATTN_SCALE = 1.0 / math.sqrt(HEAD_DIM)
NEG_INF = -1e30
LOG2E = math.log2(math.e)
EPS = 1e-6
PROJ_WIDTH = 4 * ATTN_WIDTH + 2 * LRU_WIDTH
MOD_SHIFT, MOD_SCALE, MOD_GATE = 0, 1, 2

LANES = 128
SUBLANES = 8
VMEM_LIMIT_BYTES = 56 * 1024 * 1024

ADA_TK = 256
PROJ_TM = 256
PROJ_W_CHUNK = 512
PROJ_W_SLOTS = 2
PROJ_VMEM_LIMIT_BYTES = 60 * 1024 * 1024
NORM_ROWS = 64
ATTN_QB = 128
ATTN_SB = 2048
ATTN_LOOKAHEAD = 1
ATTN_SLOTS = ATTN_LOOKAHEAD + 1
DEINT_ROWS = 256
DEINT_RATIO = 4
ATTN_VMEM_LIMIT_BYTES = 60 * 1024 * 1024
HEADS_PER_BLOCK = LANES // HEAD_DIM
COMBINE_ROWS = 256
LRU_CHUNK = 1024
LRU_SEGMENTS = SUBLANES
LRU_SEG_LEN = SEQ // LRU_SEGMENTS
LRU_SEG_PITCH = LRU_SEG_LEN + SUBLANES
OUT_TM = 512

F32 = jnp.float32
BF16 = jnp.bfloat16


def _silu(x):
    h = 0.5 * x
    return h + h * jnp.tanh(h)


def _ada_kernel(c_ref, w_ref, b_ref, o_ref):
    @pl.when(pl.program_id(0) == 0)
    def _():
        o_ref[...] = b_ref[...]

    c_act = _silu(c_ref[...])
    o_ref[...] += jnp.sum(w_ref[...] * c_act, axis=0, keepdims=True)


def _ada_mod(c_col, w_ada, b_ada):
    n = w_ada.shape[1]
    return pl.pallas_call(
        _ada_kernel,
        grid=(D_MODEL // ADA_TK,),
        in_specs=[
            pl.BlockSpec((ADA_TK, 1), lambda k: (k, 0)),
            pl.BlockSpec((ADA_TK, n), lambda k: (k, 0)),
            pl.BlockSpec((1, n), lambda k: (0, 0)),
        ],
        out_specs=pl.BlockSpec((1, n), lambda k: (0, 0)),
        out_shape=jax.ShapeDtypeStruct((1, n), F32),
        compiler_params=pltpu.CompilerParams(
            dimension_semantics=("arbitrary",), vmem_limit_bytes=VMEM_LIMIT_BYTES),
        name="ada_mod",
    )(c_col, w_ada, b_ada)


def _w_chunk_copy(w_hbm, stage, sem, c):
    slot = c % PROJ_W_SLOTS
    cols = pl.ds(c * PROJ_W_CHUNK, PROJ_W_CHUNK)
    return pltpu.make_async_copy(w_hbm.at[:, cols], stage.at[slot], sem.at[slot])


def _proj_kernel(x0_ref, xn_ref, gain_ref, shift_ref, scale_ref, w_hbm, o_ref,
                 w_res, stage, h_even, h_odd, sem):
    s = pl.program_id(0)
    n_chunks = PROJ_WIDTH // PROJ_W_CHUNK
    slabs_per_chunk = PROJ_W_CHUNK // LANES

    def normalise(x_ref, h_ref):
        mul = gain_ref[...] * (1.0 + scale_ref[...])
        shift = shift_ref[...]
        for r0 in range(0, PROJ_TM, NORM_ROWS):
            xv = x_ref[pl.ds(r0, NORM_ROWS), :]
            var = jnp.mean(xv * xv, axis=-1, keepdims=True)
            h = xv * lax.rsqrt(var + EPS) * mul + shift
            h_ref[pl.ds(r0, NORM_ROWS), :] = h.astype(BF16)

    def project(h_ref, first_slab, n_slabs):
        cols = pl.ds(first_slab * LANES, n_slabs * LANES)
        acc = jnp.dot(h_ref[...], w_res[:, cols], preferred_element_type=F32)
        for b in range(n_slabs):
            o_ref[first_slab + b] = acc[:, b * LANES:(b + 1) * LANES]

    @pl.when(s == 0)
    def _():
        for c in range(PROJ_W_SLOTS):
            _w_chunk_copy(w_hbm, stage, sem, c).start()
        normalise(x0_ref, h_even)
        normalise(xn_ref, h_odd)
        for c in range(n_chunks):
            _w_chunk_copy(w_hbm, stage, sem, c).wait()
            cols = pl.ds(c * PROJ_W_CHUNK, PROJ_W_CHUNK)
            w_res[:, cols] = stage[c % PROJ_W_SLOTS].astype(BF16)
            if c + PROJ_W_SLOTS < n_chunks:
                _w_chunk_copy(w_hbm, stage, sem, c + PROJ_W_SLOTS).start()
            project(h_even, c * slabs_per_chunk, slabs_per_chunk)

    is_even = jnp.bitwise_and(s, 1) == 0

    @pl.when(jnp.logical_and(s > 0, is_even))
    def _():
        normalise(xn_ref, h_odd)
        project(h_even, 0, PROJ_WIDTH // LANES)

    @pl.when(jnp.logical_not(is_even))
    def _():
        normalise(xn_ref, h_even)
        project(h_odd, 0, PROJ_WIDTH // LANES)


def _in_proj(x2d, gain, mod, w_in):
    n_tiles = SEQ // PROJ_TM
    return pl.pallas_call(
        _proj_kernel,
        grid=(n_tiles,),
        in_specs=[
            pl.BlockSpec((PROJ_TM, D_MODEL), lambda s: (0, 0), pipeline_mode=pl.Buffered(1)),
            pl.BlockSpec((PROJ_TM, D_MODEL), lambda s: (jnp.minimum(s + 1, n_tiles - 1), 0)),
            pl.BlockSpec((1, D_MODEL), lambda s: (0, 0)),
            pl.BlockSpec((1, D_MODEL), lambda s: (0, MOD_SHIFT)),
            pl.BlockSpec((1, D_MODEL), lambda s: (0, MOD_SCALE)),
            pl.BlockSpec(memory_space=pl.ANY),
        ],
        out_specs=pl.BlockSpec((PROJ_WIDTH // LANES, PROJ_TM, LANES), lambda s: (0, s, 0)),
        out_shape=jax.ShapeDtypeStruct((PROJ_WIDTH // LANES, SEQ, LANES), F32),
        scratch_shapes=[
            pltpu.VMEM((D_MODEL, PROJ_WIDTH), BF16),
            pltpu.VMEM((PROJ_W_SLOTS, D_MODEL, PROJ_W_CHUNK), F32),
            pltpu.VMEM((PROJ_TM, D_MODEL), BF16),
            pltpu.VMEM((PROJ_TM, D_MODEL), BF16),
            pltpu.SemaphoreType.DMA((PROJ_W_SLOTS,)),
        ],
        compiler_params=pltpu.CompilerParams(
            dimension_semantics=("arbitrary",), vmem_limit_bytes=PROJ_VMEM_LIMIT_BYTES),
        name="in_proj",
    )(x2d, x2d, gain, mod, mod, w_in)


def _rows(start, size, stride):
    return pl.ds(start, size) if stride == 1 else pl.ds(start, size, stride=stride)


def _attn_kernel(q_ref, k_ref, v_ref, g_ref, coef_ref, o_ref,
                 kd, vd, tmp, bias_s, s_s, mx_s, m_s, l_s, acc_s):
    sb = pl.program_id(1)
    lane = lax.broadcasted_iota(jnp.int32, (ATTN_QB, LANES), 1)
    is_h0 = lane < HEAD_DIM

    @pl.when(sb == 0)
    def _():
        for src_ref, dst in ((k_ref, kd), (v_ref, vd)):
            for p, (_, d) in enumerate(DILATED_PATTERNS):
                dst[p, pl.ds(0, ATTN_QB), :] = jnp.zeros((ATTN_QB, LANES), BF16)
                d_prev = DILATED_PATTERNS[p - 1][1] if p else 1
                ratio = d // d_prev
                assert p == 0 and d == 1 or ratio == DEINT_RATIO
                per_residue = SEQ // d // DEINT_ROWS
                keep_f32 = 0 < p < len(DILATED_PATTERNS) - 1
                from_ref = src_ref if p <= 1 else tmp

                def deint(c, carry, p=p, d_prev=d_prev, ratio=ratio, per_residue=per_residue,
                          keep_f32=keep_f32, from_ref=from_ref, dst=dst):
                    r = lax.shift_right_logical(c, per_residue.bit_length() - 1)
                    chunk = jnp.bitwise_and(c, per_residue - 1)
                    r_prev = jnp.bitwise_and(r, d_prev - 1)
                    j = lax.shift_right_logical(r, d_prev.bit_length() - 1)
                    start = r_prev * (SEQ // d_prev) + chunk * (DEINT_ROWS * ratio) + j
                    x = from_ref[_rows(start, DEINT_ROWS, ratio), :]
                    if keep_f32:
                        tmp[pl.ds(pl.multiple_of(c * DEINT_ROWS, DEINT_ROWS), DEINT_ROWS), :] = x
                    rows = pl.ds(pl.multiple_of(ATTN_QB + c * DEINT_ROWS, ATTN_QB), DEINT_ROWS)
                    dst[p, rows, :] = x.astype(BF16)
                    return carry

                lax.fori_loop(0, SEQ // DEINT_ROWS, deint, 0, unroll=4)

        qi = lax.broadcasted_iota(jnp.int32, (ATTN_QB, 2 * ATTN_QB), 0)
        ki = lax.broadcasted_iota(jnp.int32, (ATTN_QB, 2 * ATTN_QB), 1)
        dist = qi + ATTN_QB - ki
        in_band = jnp.logical_and(dist >= 0, dist <= ATTN_QB)
        in_band_cur = jnp.logical_and(in_band, ki >= ATTN_QB)
        dist_f = dist.astype(F32)
        for p in range(len(DILATED_PATTERNS)):
            for h in range(HEADS_PER_BLOCK):
                pen = -(coef_ref[pl.program_id(0), p * HEADS_PER_BLOCK + h] * dist_f)
                rows = pl.ds(h * ATTN_QB, ATTN_QB)
                bias_s[p, 0, rows, :] = jnp.where(in_band, pen, NEG_INF)
                bias_s[p, 1, rows, :] = jnp.where(in_band_cur, pen, NEG_INF)

    ones = jnp.ones((2 * ATTN_QB, LANES), BF16)

    def place(d, ti):
        r, n = ti % d, ti // d
        blk = sb * (ATTN_SB // (ATTN_QB * d)) + n
        q_rows = _rows(r + n * (ATTN_QB * d), ATTN_QB, d)
        window = pl.ds(pl.multiple_of(r * (SEQ // d) + blk * ATTN_QB, ATTN_QB), 2 * ATTN_QB)
        return q_rows, window, blk == 0

    def scores(p, d, ti, slot):
        q_rows, window, first = place(d, ti)
        q = (q_ref[q_rows, :] * (ATTN_SCALE * LOG2E)).astype(BF16)
        zero = jnp.zeros_like(q)
        qs = jnp.concatenate([jnp.where(is_h0, q, zero), jnp.where(is_h0, zero, q)], axis=0)
        s = lax.dot_general(qs, kd[p, window, :], (((1,), (1,)), ((), ())),
                            preferred_element_type=F32)
        s = s + bias_s[p, first.astype(jnp.int32)]
        s_s[slot] = s
        mx_s[slot] = jnp.broadcast_to(jnp.max(s, axis=1, keepdims=True), (2 * ATTN_QB, LANES))

    def weighted(p, d, ti, slot):
        q_rows, window, _ = place(d, ti)
        mx = mx_s[slot]
        e = jnp.concatenate(
            [jnp.exp2(s_s[slot, :, :LANES] - mx), jnp.exp2(s_s[slot, :, LANES:] - mx)],
            axis=1).astype(BF16)
        vc = jnp.concatenate([vd[p, window, :], ones], axis=1)
        pv = jnp.dot(e, vc, preferred_element_type=F32)
        acc_s[p, q_rows, :] = jnp.where(is_h0, pv[:ATTN_QB, :LANES], pv[ATTN_QB:, :LANES])
        l_s[p, q_rows, :] = jnp.where(is_h0, pv[:ATTN_QB, LANES:], pv[ATTN_QB:, LANES:])
        m_s[p, q_rows, :] = jnp.where(is_h0, mx[:ATTN_QB], mx[ATTN_QB:])

    tiles = [(p, d, ti) for p, (_, d) in enumerate(DILATED_PATTERNS)
             for ti in range(ATTN_SB // ATTN_QB)]
    for k in range(min(ATTN_LOOKAHEAD, len(tiles))):
        scores(*tiles[k], k % ATTN_SLOTS)
    for k, tile in enumerate(tiles):
        if k + ATTN_LOOKAHEAD < len(tiles):
            scores(*tiles[k + ATTN_LOOKAHEAD], (k + ATTN_LOOKAHEAD) % ATTN_SLOTS)
        weighted(*tile, k % ATTN_SLOTS)

    def combine(i, carry):
        rows = pl.ds(pl.multiple_of(i * COMBINE_ROWS, COMBINE_ROWS), COMBINE_ROWS)
        m0, m1, m2 = m_s[0, rows, :], m_s[1, rows, :], m_s[2, rows, :]
        mm = jnp.maximum(jnp.maximum(m0, m1), m2)
        e0, e1, e2 = jnp.exp2(m0 - mm), jnp.exp2(m1 - mm), jnp.exp2(m2 - mm)
        num = e0 * acc_s[0, rows, :] + e1 * acc_s[1, rows, :] + e2 * acc_s[2, rows, :]
        den = e0 * l_s[0, rows, :] + e1 * l_s[1, rows, :] + e2 * l_s[2, rows, :]
        o_ref[rows, :] = (num / den * _silu(g_ref[rows, :])).astype(BF16)
        return carry

    lax.fori_loop(0, ATTN_SB // COMBINE_ROWS, combine, 0, unroll=2)


def _alibi_coefs():
    slopes = 2.0 ** (-8.0 * np.arange(1, N_HEADS + 1, dtype=np.float64) / N_HEADS)
    dil = np.array([d for _, d in DILATED_PATTERNS], dtype=np.float64)
    coef = slopes.reshape(-1, HEADS_PER_BLOCK)[:, None, :] * dil[None, :, None] * LOG2E
    return jnp.asarray(coef.reshape(N_HEADS // HEADS_PER_BLOCK, -1), dtype=F32)


def _attention(proj):
    n_hb = ATTN_WIDTH // LANES
    n_p = len(DILATED_PATTERNS)
    return pl.pallas_call(
        _attn_kernel,
        grid=(n_hb, SEQ // ATTN_SB),
        in_specs=[
            pl.BlockSpec((None, ATTN_SB, LANES), lambda h, s: (h, s, 0)),
            pl.BlockSpec((None, SEQ, LANES), lambda h, s: (n_hb + h, 0, 0)),
            pl.BlockSpec((None, SEQ, LANES), lambda h, s: (2 * n_hb + h, 0, 0)),
            pl.BlockSpec((None, ATTN_SB, LANES), lambda h, s: (3 * n_hb + h, s, 0)),
            pl.BlockSpec(memory_space=pltpu.SMEM),
        ],
        out_specs=pl.BlockSpec((ATTN_SB, LANES), lambda h, s: (s, h)),
        out_shape=jax.ShapeDtypeStruct((SEQ, ATTN_WIDTH), BF16),
        scratch_shapes=[
            pltpu.VMEM((n_p, ATTN_QB + SEQ, LANES), BF16),
            pltpu.VMEM((n_p, ATTN_QB + SEQ, LANES), BF16),
            pltpu.VMEM((SEQ, LANES), F32),
            pltpu.VMEM((n_p, 2, HEADS_PER_BLOCK * ATTN_QB, 2 * ATTN_QB), F32),
            pltpu.VMEM((ATTN_SLOTS, 2 * ATTN_QB, 2 * ATTN_QB), F32),
            pltpu.VMEM((ATTN_SLOTS, 2 * ATTN_QB, LANES), F32),
        ] + [pltpu.VMEM((n_p, ATTN_SB, LANES), F32)] * 3,
        compiler_params=pltpu.CompilerParams(
            dimension_semantics=("arbitrary", "arbitrary"),
            vmem_limit_bytes=ATTN_VMEM_LIMIT_BYTES),
        name="dilated_attn",
    )(proj, proj, proj, proj, _alibi_coefs())


def _lru_kernel(u_ref, g_ref, cw_ref, cb_ref, wg_ref, bg_ref, lam_ref, wout_ref,
                o_ref, wout_bf16_ref, ubuf, a_s, b_s, h_s, p_s):
    wout_bf16_ref[...] = wout_ref[...].astype(BF16)

    lam = lam_ref[...]
    log_sig = jnp.minimum(lam, 0.0) - jnp.log1p(jnp.exp(-jnp.abs(lam)))
    half_c_log_sig = (0.5 * LRU_C) * log_sig
    chunks_per_seg = LRU_SEG_LEN // LRU_CHUNK

    ubuf[pl.ds(0, SUBLANES), :] = jnp.zeros((SUBLANES, LANES), F32)

    def chunk(c, carry):
        rows = pl.ds(pl.multiple_of(c * LRU_CHUNK, LRU_CHUNK), LRU_CHUNK)
        ubuf[pl.ds(SUBLANES, LRU_CHUNK), :] = u_ref[rows, :]
        xc = cb_ref[...]
        for j in range(CONV_WIDTH):
            off = SUBLANES - (CONV_WIDTH - 1) + j
            xc = xc + ubuf[pl.ds(off, LRU_CHUNK), :] * cw_ref[pl.ds(j, 1), :]
        ubuf[pl.ds(0, SUBLANES), :] = ubuf[pl.ds(LRU_CHUNK, SUBLANES), :]

        z = jnp.dot(xc.astype(BF16), wg_ref[...], preferred_element_type=F32) + bg_ref[...]
        two_i = jnp.tanh(z[:, LANES:]) + 1.0
        log_a = half_c_log_sig * jnp.tanh(z[:, :LANES]) + half_c_log_sig
        a = jnp.exp(log_a)
        th = jnp.tanh(log_a)
        x4 = -0.5 * th / (1.0 - th)
        half_mult = jnp.where(x4 > 0.0, x4 * lax.rsqrt(x4), 0.0)
        b = (half_mult * xc) * two_i

        seg = lax.shift_right_logical(c, chunks_per_seg.bit_length() - 1)
        within = jnp.bitwise_and(c, chunks_per_seg - 1)
        dst = pl.ds(pl.multiple_of(seg * LRU_SEG_PITCH + within * LRU_CHUNK, SUBLANES), LRU_CHUNK)
        a_s[dst, :] = a
        b_s[dst, :] = b
        return carry

    lax.fori_loop(0, SEQ // LRU_CHUNK, chunk, 0, unroll=True)

    def step(t, carry):
        h, prod = carry
        rows = pl.ds(t, LRU_SEGMENTS, stride=LRU_SEG_PITCH)
        a = a_s[rows, :]
        h = a * h + b_s[rows, :]
        prod = prod * a
        h_s[rows, :] = h
        p_s[rows, :] = prod
        return h, prod

    h_end, p_end = lax.fori_loop(
        0, LRU_SEG_LEN, step,
        (jnp.zeros((LRU_SEGMENTS, LANES), F32), jnp.ones((LRU_SEGMENTS, LANES), F32)), unroll=16)

    seg_id = lax.broadcasted_iota(jnp.int32, (LRU_SEGMENTS, LANES), 0)
    c_in = jnp.zeros((LRU_SEGMENTS, LANES), F32)
    for _ in range(LRU_SEGMENTS - 1):
        c_in = jnp.where(seg_id == 0, 0.0, pltpu.roll(h_end + p_end * c_in, 1, 0))

    for s in range(LRU_SEGMENTS):
        c_s = c_in[s:s + 1, :]

        def gate(k, carry, s=s, c_s=c_s):
            off = pl.multiple_of(k * LRU_CHUNK, LRU_CHUNK)
            src = pl.ds(s * LRU_SEG_PITCH + off, LRU_CHUNK)
            rows = pl.ds(s * LRU_SEG_LEN + off, LRU_CHUNK)
            h = h_s[src, :] + p_s[src, :] * c_s
            o_ref[rows, :] = (h * _silu(g_ref[rows, :])).astype(BF16)
            return carry

        lax.fori_loop(0, chunks_per_seg, gate, 0, unroll=True)


def _lru(proj, conv_w, conv_b, w_gates, b_gates, lam, w_out):
    n_cb = LRU_WIDTH // LANES
    u_col0 = 4 * ATTN_WIDTH // LANES
    g_col0 = u_col0 + n_cb
    return pl.pallas_call(
        _lru_kernel,
        grid=(n_cb,),
        in_specs=[
            pl.BlockSpec((None, SEQ, LANES), lambda j: (u_col0 + j, 0, 0)),
            pl.BlockSpec((None, SEQ, LANES), lambda j: (g_col0 + j, 0, 0)),
            pl.BlockSpec((CONV_WIDTH, LANES), lambda j: (0, j)),
            pl.BlockSpec((1, LANES), lambda j: (0, j)),
            pl.BlockSpec((None, LANES, 2 * LANES), lambda j: (j, 0, 0)),
            pl.BlockSpec((None, 1, 2 * LANES), lambda j: (j, 0, 0)),
            pl.BlockSpec((1, LANES), lambda j: (0, j)),
            pl.BlockSpec((D_MODEL // n_cb, D_MODEL), lambda j: (j, 0)),
        ],
        out_specs=[
            pl.BlockSpec((SEQ, LANES), lambda j: (0, j)),
            pl.BlockSpec((D_MODEL // n_cb, D_MODEL), lambda j: (j, 0)),
        ],
        out_shape=[
            jax.ShapeDtypeStruct((SEQ, LRU_WIDTH), BF16),
            jax.ShapeDtypeStruct((D_MODEL, D_MODEL), BF16),
        ],
        scratch_shapes=[
            pltpu.VMEM((LRU_CHUNK + SUBLANES, LANES), F32),
        ] + [pltpu.VMEM((LRU_SEGMENTS * LRU_SEG_PITCH, LANES), F32)] * 4,
        compiler_params=pltpu.CompilerParams(
            dimension_semantics=("arbitrary",), vmem_limit_bytes=VMEM_LIMIT_BYTES),
        name="rglru",
    )(proj, proj, conv_w, conv_b, w_gates, b_gates, lam, w_out)


def _block_diag_gates(w_rgate, b_rgate, w_igate, b_igate):
    per = LANES // LRU_BLOCK_W
    n_cb = LRU_WIDTH // LANES

    def bd(w):
        w = w.reshape(n_cb, per, LRU_BLOCK_W, LRU_BLOCK_W)
        eye = jnp.eye(per, dtype=w.dtype)
        return jnp.einsum('cpkj,pq->cpkqj', w, eye).reshape(n_cb, LANES, LANES)

    w = (0.5 * jnp.concatenate([bd(w_rgate), bd(w_igate)], axis=-1)).astype(BF16)
    b = 0.5 * jnp.concatenate(
        [b_rgate.reshape(n_cb, 1, LANES), b_igate.reshape(n_cb, 1, LANES)], axis=-1)
    return w, b


def _out_kernel(ma_ref, ml_ref, w_ref, x_ref, gate_ref, fg_ref, o_ref):
    mixed = jnp.concatenate([ma_ref[...], ml_ref[...]], axis=1)
    mix = jnp.dot(mixed, w_ref[...], preferred_element_type=F32)
    y = x_ref[...] + gate_ref[...] * mix
    var = jnp.mean(y * y, axis=-1, keepdims=True)
    o_ref[...] = y * lax.rsqrt(var + EPS) * fg_ref[...]


def _out_proj(mixed_attn, mixed_lru, w_out_bf16, x2d, mod, final_gain):
    return pl.pallas_call(
        _out_kernel,
        grid=(SEQ // OUT_TM,),
        in_specs=[
            pl.BlockSpec((OUT_TM, ATTN_WIDTH), lambda i: (i, 0)),
            pl.BlockSpec((OUT_TM, LRU_WIDTH), lambda i: (i, 0)),
            pl.BlockSpec((D_MODEL, D_MODEL), lambda i: (0, 0)),
            pl.BlockSpec((OUT_TM, D_MODEL), lambda i: (i, 0)),
            pl.BlockSpec((1, D_MODEL), lambda i: (0, MOD_GATE)),
            pl.BlockSpec((1, D_MODEL), lambda i: (0, 0)),
        ],
        out_specs=pl.BlockSpec((OUT_TM, D_MODEL), lambda i: (i, 0)),
        out_shape=jax.ShapeDtypeStruct((SEQ, D_MODEL), F32),
        compiler_params=pltpu.CompilerParams(
            dimension_semantics=("arbitrary",), vmem_limit_bytes=VMEM_LIMIT_BYTES),
        name="out_proj",
    )(mixed_attn, mixed_lru, w_out_bf16, x2d, mod, final_gain)


@jax.jit
def kernel(x, c, norm_gain, w_ada, b_ada, w_in, conv_w, conv_b, w_rgate, b_rgate,
           w_igate, b_igate, lru_lambda, w_out, final_gain):
    assert x.shape == (1, SEQ, D_MODEL) and norm_gain.shape[0] == 1
    x2d = x.reshape(SEQ, D_MODEL)
    mod = _ada_mod(c.reshape(D_MODEL, 1), w_ada[0], b_ada)
    proj = _in_proj(x2d, norm_gain, mod, w_in[0])
    mixed_attn = _attention(proj)
    w_gates, b_gates = _block_diag_gates(w_rgate[0], b_rgate[0], w_igate[0], b_igate[0])
    mixed_lru, w_out_bf16 = _lru(proj, conv_w[0], conv_b, w_gates, b_gates, lru_lambda, w_out[0])
    y = _out_proj(mixed_attn, mixed_lru, w_out_bf16, x2d, mod,
                  final_gain.reshape(1, D_MODEL))
    return y.reshape(1, SEQ, D_MODEL)
```

```python
import functools
import math

import jax
import jax.numpy as jnp
import numpy as np
from jax import lax
from jax.experimental import pallas as pl
from jax.experimental.pallas import tpu as pltpu

D_MODEL = 2048
SEQ = 8192
ATTN_WIDTH = D_MODEL // 2
LRU_WIDTH = D_MODEL - ATTN_WIDTH
HEAD_DIM = 64
N_HEADS = ATTN_WIDTH // HEAD_DIM
LRU_BLOCKS = 16
LRU_BLOCK_W = LRU_WIDTH // LRU_BLOCKS
CONV_WIDTH = 4
LRU_C = 8.0
DILATED_PATTERNS = ((128, 1), (512, 4), (2048, 16))
ATTN_SCALE = 1.0 / math.sqrt(HEAD_DIM)
NEG_INF = -1e30
LOG2E = math.log2(math.e)
EPS = 1e-6
PROJ_WIDTH = 4 * ATTN_WIDTH + 2 * LRU_WIDTH
MOD_SHIFT, MOD_SCALE, MOD_GATE = 0, 1, 2

LANES = 128
SUBLANES = 8
VMEM_LIMIT_BYTES = 56 * 1024 * 1024

ADA_TK = 256
PROJ_TM = 256
PROJ_W_CHUNK = 512
PROJ_W_SLOTS = 2
PROJ_VMEM_LIMIT_BYTES = 60 * 1024 * 1024
NORM_ROWS = 64
ATTN_QB = 128
ATTN_SB = 2048
ATTN_LOOKAHEAD = 1
ATTN_SLOTS = ATTN_LOOKAHEAD + 1
DEINT_ROWS = 256
DEINT_RATIO = 4
ATTN_VMEM_LIMIT_BYTES = 60 * 1024 * 1024
HEADS_PER_BLOCK = LANES // HEAD_DIM
COMBINE_ROWS = 256
STATE_PAD = SUBLANES
LRU_CHUNK = 1024
LRU_SEGMENTS = SUBLANES
LRU_SEG_LEN = SEQ // LRU_SEGMENTS
LRU_SEG_PITCH = LRU_SEG_LEN + SUBLANES
OUT_TM = 512

F32 = jnp.float32
BF16 = jnp.bfloat16


def _silu(x):
    h = 0.5 * x
    return h + h * jnp.tanh(h)


def _ada_kernel(c_ref, w_ref, b_ref, o_ref):
    @pl.when(pl.program_id(0) == 0)
    def _():
        o_ref[...] = b_ref[...]

    c_act = _silu(c_ref[...])
    o_ref[...] += jnp.sum(w_ref[...] * c_act, axis=0, keepdims=True)


def _ada_mod(c_col, w_ada, b_ada):
    n = w_ada.shape[1]
    return pl.pallas_call(
        _ada_kernel,
        grid=(D_MODEL // ADA_TK,),
        in_specs=[
            pl.BlockSpec((ADA_TK, 1), lambda k: (k, 0)),
            pl.BlockSpec((ADA_TK, n), lambda k: (k, 0)),
            pl.BlockSpec((1, n), lambda k: (0, 0)),
        ],
        out_specs=pl.BlockSpec((1, n), lambda k: (0, 0)),
        out_shape=jax.ShapeDtypeStruct((1, n), F32),
        compiler_params=pltpu.CompilerParams(
            dimension_semantics=("arbitrary",), vmem_limit_bytes=VMEM_LIMIT_BYTES),
        name="ada_mod",
    )(c_col, w_ada, b_ada)


def _w_chunk_copy(w_hbm, stage, sem, c):
    slot = c % PROJ_W_SLOTS
    cols = pl.ds(c * PROJ_W_CHUNK, PROJ_W_CHUNK)
    return pltpu.make_async_copy(w_hbm.at[:, cols], stage.at[slot], sem.at[slot])


def _proj_kernel(x0_ref, xn_ref, gain_ref, shift_ref, scale_ref, w_hbm, o_ref,
                 w_res, stage, h_even, h_odd, sem):
    s = pl.program_id(0)
    n_chunks = PROJ_WIDTH // PROJ_W_CHUNK
    slabs_per_chunk = PROJ_W_CHUNK // LANES

    def normalise(x_ref, h_ref):
        mul = gain_ref[...] * (1.0 + scale_ref[...])
        shift = shift_ref[...]
        for r0 in range(0, PROJ_TM, NORM_ROWS):
            xv = x_ref[pl.ds(r0, NORM_ROWS), :]
            var = jnp.mean(xv * xv, axis=-1, keepdims=True)
            h = xv * lax.rsqrt(var + EPS) * mul + shift
            h_ref[pl.ds(r0, NORM_ROWS), :] = h.astype(BF16)

    def project(h_ref, first_slab, n_slabs):
        cols = pl.ds(first_slab * LANES, n_slabs * LANES)
        acc = jnp.dot(h_ref[...], w_res[:, cols], preferred_element_type=F32)
        for b in range(n_slabs):
            o_ref[first_slab + b] = acc[:, b * LANES:(b + 1) * LANES]

    @pl.when(s == 0)
    def _():
        for c in range(PROJ_W_SLOTS):
            _w_chunk_copy(w_hbm, stage, sem, c).start()
        normalise(x0_ref, h_even)
        normalise(xn_ref, h_odd)
        for c in range(n_chunks):
            _w_chunk_copy(w_hbm, stage, sem, c).wait()
            cols = pl.ds(c * PROJ_W_CHUNK, PROJ_W_CHUNK)
            w_res[:, cols] = stage[c % PROJ_W_SLOTS].astype(BF16)
            if c + PROJ_W_SLOTS < n_chunks:
                _w_chunk_copy(w_hbm, stage, sem, c + PROJ_W_SLOTS).start()
            project(h_even, c * slabs_per_chunk, slabs_per_chunk)

    is_even = jnp.bitwise_and(s, 1) == 0

    @pl.when(jnp.logical_and(s > 0, is_even))
    def _():
        normalise(xn_ref, h_odd)
        project(h_even, 0, PROJ_WIDTH // LANES)

    @pl.when(jnp.logical_not(is_even))
    def _():
        normalise(xn_ref, h_even)
        project(h_odd, 0, PROJ_WIDTH // LANES)


def _in_proj(x2d, gain, mod, w_in):
    n_tiles = SEQ // PROJ_TM
    return pl.pallas_call(
        _proj_kernel,
        grid=(n_tiles,),
        in_specs=[
            pl.BlockSpec((PROJ_TM, D_MODEL), lambda s: (0, 0), pipeline_mode=pl.Buffered(1)),
            pl.BlockSpec((PROJ_TM, D_MODEL), lambda s: (jnp.minimum(s + 1, n_tiles - 1), 0)),
            pl.BlockSpec((1, D_MODEL), lambda s: (0, 0)),
            pl.BlockSpec((1, D_MODEL), lambda s: (0, MOD_SHIFT)),
            pl.BlockSpec((1, D_MODEL), lambda s: (0, MOD_SCALE)),
            pl.BlockSpec(memory_space=pl.ANY),
        ],
        out_specs=pl.BlockSpec((PROJ_WIDTH // LANES, PROJ_TM, LANES), lambda s: (0, s, 0)),
        out_shape=jax.ShapeDtypeStruct((PROJ_WIDTH // LANES, SEQ, LANES), F32),
        scratch_shapes=[
            pltpu.VMEM((D_MODEL, PROJ_WIDTH), BF16),
            pltpu.VMEM((PROJ_W_SLOTS, D_MODEL, PROJ_W_CHUNK), F32),
            pltpu.VMEM((PROJ_TM, D_MODEL), BF16),
            pltpu.VMEM((PROJ_TM, D_MODEL), BF16),
            pltpu.SemaphoreType.DMA((PROJ_W_SLOTS,)),
        ],
        compiler_params=pltpu.CompilerParams(
            dimension_semantics=("arbitrary",), vmem_limit_bytes=PROJ_VMEM_LIMIT_BYTES),
        name="in_proj",
    )(x2d, x2d, gain, mod, mod, w_in)


def _rows(start, size, stride):
    return pl.ds(start, size) if stride == 1 else pl.ds(start, size, stride=stride)


def _attn_kernel(q_ref, k_ref, v_ref, g_ref, coef_ref, o_ref,
                 kd, vd, tmp, bias_s, s_s, mx_s, state):
    sb = pl.program_id(1)
    n_p = len(DILATED_PATTERNS)
    lse_s = [state.at[p] for p in range(n_p)]
    out_s = [state.at[n_p + p] for p in range(n_p)]
    lane = lax.broadcasted_iota(jnp.int32, (ATTN_QB, LANES), 1)
    is_h0 = lane < HEAD_DIM

    @pl.when(sb == 0)
    def _():
        for src_ref, dst in ((k_ref, kd), (v_ref, vd)):
            for p, (_, d) in enumerate(DILATED_PATTERNS):
                dst[p, pl.ds(0, ATTN_QB), :] = jnp.zeros((ATTN_QB, LANES), BF16)
                d_prev = DILATED_PATTERNS[p - 1][1] if p else 1
                ratio = d // d_prev
                assert p == 0 and d == 1 or ratio == DEINT_RATIO
                per_residue = SEQ // d // DEINT_ROWS
                keep_f32 = 0 < p < len(DILATED_PATTERNS) - 1
                from_ref = src_ref if p <= 1 else tmp

                def deint(c, carry, p=p, d_prev=d_prev, ratio=ratio, per_residue=per_residue,
                          keep_f32=keep_f32, from_ref=from_ref, dst=dst):
                    r = lax.shift_right_logical(c, per_residue.bit_length() - 1)
                    chunk = jnp.bitwise_and(c, per_residue - 1)
                    r_prev = jnp.bitwise_and(r, d_prev - 1)
                    j = lax.shift_right_logical(r, d_prev.bit_length() - 1)
                    start = r_prev * (SEQ // d_prev) + chunk * (DEINT_ROWS * ratio) + j
                    x = from_ref[_rows(start, DEINT_ROWS, ratio), :]
                    if keep_f32:
                        tmp[pl.ds(pl.multiple_of(c * DEINT_ROWS, DEINT_ROWS), DEINT_ROWS), :] = x
                    rows = pl.ds(pl.multiple_of(ATTN_QB + c * DEINT_ROWS, ATTN_QB), DEINT_ROWS)
                    dst[p, rows, :] = x.astype(BF16)
                    return carry

                lax.fori_loop(0, SEQ // DEINT_ROWS, deint, 0, unroll=4)

        qi = lax.broadcasted_iota(jnp.int32, (ATTN_QB, 2 * ATTN_QB), 0)
        ki = lax.broadcasted_iota(jnp.int32, (ATTN_QB, 2 * ATTN_QB), 1)
        dist = qi + ATTN_QB - ki
        in_band = jnp.logical_and(dist >= 0, dist <= ATTN_QB)
        in_band_cur = jnp.logical_and(in_band, ki >= ATTN_QB)
        dist_f = dist.astype(F32)
        for p in range(len(DILATED_PATTERNS)):
            for h in range(HEADS_PER_BLOCK):
                pen = -(coef_ref[pl.program_id(0), p * HEADS_PER_BLOCK + h] * dist_f)
                rows = pl.ds(h * ATTN_QB, ATTN_QB)
                bias_s[p, 0, rows, :] = jnp.where(in_band, pen, NEG_INF)
                bias_s[p, 1, rows, :] = jnp.where(in_band_cur, pen, NEG_INF)

    ones = jnp.ones((2 * ATTN_QB, LANES), BF16)

    def place(d, ti):
        r, n = ti % d, ti // d
        blk = sb * (ATTN_SB // (ATTN_QB * d)) + n
        q_rows = _rows(r + n * (ATTN_QB * d), ATTN_QB, d)
        window = pl.ds(pl.multiple_of(r * (SEQ // d) + blk * ATTN_QB, ATTN_QB), 2 * ATTN_QB)
        return q_rows, window, blk == 0

    def scores(p, d, ti, slot):
        q_rows, window, first = place(d, ti)
        q = (q_ref[q_rows, :] * (ATTN_SCALE * LOG2E)).astype(BF16)
        zero = jnp.zeros_like(q)
        qs = jnp.concatenate([jnp.where(is_h0, q, zero), jnp.where(is_h0, zero, q)], axis=0)
        s = lax.dot_general(qs, kd[p, window, :], (((1,), (1,)), ((), ())),
                            preferred_element_type=F32)
        s = s + bias_s[p, first.astype(jnp.int32)]
        s_s[slot] = s
        mx_s[slot] = jnp.broadcast_to(jnp.max(s, axis=1, keepdims=True), (2 * ATTN_QB, LANES))

    def weighted(p, d, ti, slot):
        q_rows, window, _ = place(d, ti)
        mx = mx_s[slot]
        e = jnp.concatenate(
            [jnp.exp2(s_s[slot, :, :LANES] - mx), jnp.exp2(s_s[slot, :, LANES:] - mx)],
            axis=1).astype(BF16)
        vc = jnp.concatenate([vd[p, window, :], ones], axis=1)
        pv = jnp.dot(e, vc, preferred_element_type=F32)
        acc = jnp.where(is_h0, pv[:ATTN_QB, :LANES], pv[ATTN_QB:, :LANES])
        l = jnp.where(is_h0, pv[:ATTN_QB, LANES:], pv[ATTN_QB:, LANES:])
        m = jnp.where(is_h0, mx[:ATTN_QB], mx[ATTN_QB:])
        out_s[p][q_rows, :] = acc / l
        lse_s[p][q_rows, :] = m + jnp.log2(l)

    tiles = [(p, d, ti) for p, (_, d) in enumerate(DILATED_PATTERNS)
             for ti in range(ATTN_SB // ATTN_QB)]
    for k in range(min(ATTN_LOOKAHEAD, len(tiles))):
        scores(*tiles[k], k % ATTN_SLOTS)
    for k, tile in enumerate(tiles):
        if k + ATTN_LOOKAHEAD < len(tiles):
            scores(*tiles[k + ATTN_LOOKAHEAD], (k + ATTN_LOOKAHEAD) % ATTN_SLOTS)
        weighted(*tile, k % ATTN_SLOTS)

    def combine(i, carry):
        rows = pl.ds(pl.multiple_of(i * COMBINE_ROWS, COMBINE_ROWS), COMBINE_ROWS)
        lse = [lse_s[p][rows, :] for p in range(n_p)]
        top = functools.reduce(jnp.maximum, lse)
        w = [jnp.exp2(x - top) for x in lse]
        num = functools.reduce(jnp.add, [w[p] * out_s[p][rows, :] for p in range(n_p)])
        den = functools.reduce(jnp.add, w)
        o_ref[rows, :] = (num / den * _silu(g_ref[rows, :])).astype(BF16)
        return carry

    lax.fori_loop(0, ATTN_SB // COMBINE_ROWS, combine, 0, unroll=2)


def _alibi_coefs():
    slopes = 2.0 ** (-8.0 * np.arange(1, N_HEADS + 1, dtype=np.float64) / N_HEADS)
    dil = np.array([d for _, d in DILATED_PATTERNS], dtype=np.float64)
    coef = slopes.reshape(-1, HEADS_PER_BLOCK)[:, None, :] * dil[None, :, None] * LOG2E
    return jnp.asarray(coef.reshape(N_HEADS // HEADS_PER_BLOCK, -1), dtype=F32)


def _attention(proj):
    n_hb = ATTN_WIDTH // LANES
    n_p = len(DILATED_PATTERNS)
    return pl.pallas_call(
        _attn_kernel,
        grid=(n_hb, SEQ // ATTN_SB),
        in_specs=[
            pl.BlockSpec((None, ATTN_SB, LANES), lambda h, s: (h, s, 0)),
            pl.BlockSpec((None, SEQ, LANES), lambda h, s: (n_hb + h, 0, 0)),
            pl.BlockSpec((None, SEQ, LANES), lambda h, s: (2 * n_hb + h, 0, 0)),
            pl.BlockSpec((None, ATTN_SB, LANES), lambda h, s: (3 * n_hb + h, s, 0)),
            pl.BlockSpec(memory_space=pltpu.SMEM),
        ],
        out_specs=pl.BlockSpec((ATTN_SB, LANES), lambda h, s: (s, h)),
        out_shape=jax.ShapeDtypeStruct((SEQ, ATTN_WIDTH), BF16),
        scratch_shapes=[
            pltpu.VMEM((n_p, ATTN_QB + SEQ, LANES), BF16),
            pltpu.VMEM((n_p, ATTN_QB + SEQ, LANES), BF16),
            pltpu.VMEM((SEQ, LANES), F32),
            pltpu.VMEM((n_p, 2, HEADS_PER_BLOCK * ATTN_QB, 2 * ATTN_QB), F32),
            pltpu.VMEM((ATTN_SLOTS, 2 * ATTN_QB, 2 * ATTN_QB), F32),
            pltpu.VMEM((ATTN_SLOTS, 2 * ATTN_QB, LANES), F32),
            pltpu.VMEM((2 * n_p, ATTN_SB + STATE_PAD, LANES), F32),
        ],
        compiler_params=pltpu.CompilerParams(
            dimension_semantics=("arbitrary", "arbitrary"),
            vmem_limit_bytes=ATTN_VMEM_LIMIT_BYTES),
        name="dilated_attn",
    )(proj, proj, proj, proj, _alibi_coefs())


def _lru_kernel(u_ref, g_ref, cw_ref, cb_ref, wg_ref, bg_ref, lam_ref, wout_ref,
                o_ref, wout_bf16_ref, ubuf, a_s, b_s, h_s, p_s):
    wout_bf16_ref[...] = wout_ref[...].astype(BF16)

    lam = lam_ref[...]
    log_sig = jnp.minimum(lam, 0.0) - jnp.log1p(jnp.exp(-jnp.abs(lam)))
    half_c_log_sig = (0.5 * LRU_C) * log_sig
    chunks_per_seg = LRU_SEG_LEN // LRU_CHUNK

    ubuf[pl.ds(0, SUBLANES), :] = jnp.zeros((SUBLANES, LANES), F32)

    def chunk(c, carry):
        rows = pl.ds(pl.multiple_of(c * LRU_CHUNK, LRU_CHUNK), LRU_CHUNK)
        ubuf[pl.ds(SUBLANES, LRU_CHUNK), :] = u_ref[rows, :]
        xc = cb_ref[...]
        for j in range(CONV_WIDTH):
            off = SUBLANES - (CONV_WIDTH - 1) + j
            xc = xc + ubuf[pl.ds(off, LRU_CHUNK), :] * cw_ref[pl.ds(j, 1), :]
        ubuf[pl.ds(0, SUBLANES), :] = ubuf[pl.ds(LRU_CHUNK, SUBLANES), :]

        z = jnp.dot(xc.astype(BF16), wg_ref[...], preferred_element_type=F32) + bg_ref[...]
        two_i = jnp.tanh(z[:, LANES:]) + 1.0
        log_a = half_c_log_sig * jnp.tanh(z[:, :LANES]) + half_c_log_sig
        a = jnp.exp(log_a)
        th = jnp.tanh(log_a)
        x4 = -0.5 * th / (1.0 - th)
        half_mult = jnp.where(x4 > 0.0, x4 * lax.rsqrt(x4), 0.0)
        b = (half_mult * xc) * two_i

        seg = lax.shift_right_logical(c, chunks_per_seg.bit_length() - 1)
        within = jnp.bitwise_and(c, chunks_per_seg - 1)
        dst = pl.ds(pl.multiple_of(seg * LRU_SEG_PITCH + within * LRU_CHUNK, SUBLANES), LRU_CHUNK)
        a_s[dst, :] = a
        b_s[dst, :] = b
        return carry

    lax.fori_loop(0, SEQ // LRU_CHUNK, chunk, 0, unroll=True)

    def step(t, carry):
        h, prod = carry
        rows = pl.ds(t, LRU_SEGMENTS, stride=LRU_SEG_PITCH)
        a = a_s[rows, :]
        h = a * h + b_s[rows, :]
        prod = prod * a
        h_s[rows, :] = h
        p_s[rows, :] = prod
        return h, prod

    h_end, p_end = lax.fori_loop(
        0, LRU_SEG_LEN, step,
        (jnp.zeros((LRU_SEGMENTS, LANES), F32), jnp.ones((LRU_SEGMENTS, LANES), F32)), unroll=16)

    seg_id = lax.broadcasted_iota(jnp.int32, (LRU_SEGMENTS, LANES), 0)
    c_in = jnp.zeros((LRU_SEGMENTS, LANES), F32)
    for _ in range(LRU_SEGMENTS - 1):
        c_in = jnp.where(seg_id == 0, 0.0, pltpu.roll(h_end + p_end * c_in, 1, 0))

    for s in range(LRU_SEGMENTS):
        c_s = c_in[s:s + 1, :]

        def gate(k, carry, s=s, c_s=c_s):
            off = pl.multiple_of(k * LRU_CHUNK, LRU_CHUNK)
            src = pl.ds(s * LRU_SEG_PITCH + off, LRU_CHUNK)
            rows = pl.ds(s * LRU_SEG_LEN + off, LRU_CHUNK)
            h = h_s[src, :] + p_s[src, :] * c_s
            o_ref[rows, :] = (h * _silu(g_ref[rows, :])).astype(BF16)
            return carry

        lax.fori_loop(0, chunks_per_seg, gate, 0, unroll=True)


def _lru(proj, conv_w, conv_b, w_gates, b_gates, lam, w_out):
    n_cb = LRU_WIDTH // LANES
    u_col0 = 4 * ATTN_WIDTH // LANES
    g_col0 = u_col0 + n_cb
    return pl.pallas_call(
        _lru_kernel,
        grid=(n_cb,),
        in_specs=[
            pl.BlockSpec((None, SEQ, LANES), lambda j: (u_col0 + j, 0, 0)),
            pl.BlockSpec((None, SEQ, LANES), lambda j: (g_col0 + j, 0, 0)),
            pl.BlockSpec((CONV_WIDTH, LANES), lambda j: (0, j)),
            pl.BlockSpec((1, LANES), lambda j: (0, j)),
            pl.BlockSpec((None, LANES, 2 * LANES), lambda j: (j, 0, 0)),
            pl.BlockSpec((None, 1, 2 * LANES), lambda j: (j, 0, 0)),
            pl.BlockSpec((1, LANES), lambda j: (0, j)),
            pl.BlockSpec((D_MODEL // n_cb, D_MODEL), lambda j: (j, 0)),
        ],
        out_specs=[
            pl.BlockSpec((SEQ, LANES), lambda j: (0, j)),
            pl.BlockSpec((D_MODEL // n_cb, D_MODEL), lambda j: (j, 0)),
        ],
        out_shape=[
            jax.ShapeDtypeStruct((SEQ, LRU_WIDTH), BF16),
            jax.ShapeDtypeStruct((D_MODEL, D_MODEL), BF16),
        ],
        scratch_shapes=[
            pltpu.VMEM((LRU_CHUNK + SUBLANES, LANES), F32),
        ] + [pltpu.VMEM((LRU_SEGMENTS * LRU_SEG_PITCH, LANES), F32)] * 4,
        compiler_params=pltpu.CompilerParams(
            dimension_semantics=("arbitrary",), vmem_limit_bytes=VMEM_LIMIT_BYTES),
        name="rglru",
    )(proj, proj, conv_w, conv_b, w_gates, b_gates, lam, w_out)


def _block_diag_gates(w_rgate, b_rgate, w_igate, b_igate):
    per = LANES // LRU_BLOCK_W
    n_cb = LRU_WIDTH // LANES

    def bd(w):
        w = w.reshape(n_cb, per, LRU_BLOCK_W, LRU_BLOCK_W)
        eye = jnp.eye(per, dtype=w.dtype)
        return jnp.einsum('cpkj,pq->cpkqj', w, eye).reshape(n_cb, LANES, LANES)

    w = (0.5 * jnp.concatenate([bd(w_rgate), bd(w_igate)], axis=-1)).astype(BF16)
    b = 0.5 * jnp.concatenate(
        [b_rgate.reshape(n_cb, 1, LANES), b_igate.reshape(n_cb, 1, LANES)], axis=-1)
    return w, b


def _out_kernel(ma_ref, ml_ref, w_ref, x_ref, gate_ref, fg_ref, o_ref):
    mixed = jnp.concatenate([ma_ref[...], ml_ref[...]], axis=1)
    mix = jnp.dot(mixed, w_ref[...], preferred_element_type=F32)
    y = x_ref[...] + gate_ref[...] * mix
    var = jnp.mean(y * y, axis=-1, keepdims=True)
    o_ref[...] = y * lax.rsqrt(var + EPS) * fg_ref[...]


def _out_proj(mixed_attn, mixed_lru, w_out_bf16, x2d, mod, final_gain):
    return pl.pallas_call(
        _out_kernel,
        grid=(SEQ // OUT_TM,),
        in_specs=[
            pl.BlockSpec((OUT_TM, ATTN_WIDTH), lambda i: (i, 0)),
            pl.BlockSpec((OUT_TM, LRU_WIDTH), lambda i: (i, 0)),
            pl.BlockSpec((D_MODEL, D_MODEL), lambda i: (0, 0)),
            pl.BlockSpec((OUT_TM, D_MODEL), lambda i: (i, 0)),
            pl.BlockSpec((1, D_MODEL), lambda i: (0, MOD_GATE)),
            pl.BlockSpec((1, D_MODEL), lambda i: (0, 0)),
        ],
        out_specs=pl.BlockSpec((OUT_TM, D_MODEL), lambda i: (i, 0)),
        out_shape=jax.ShapeDtypeStruct((SEQ, D_MODEL), F32),
        compiler_params=pltpu.CompilerParams(
            dimension_semantics=("arbitrary",), vmem_limit_bytes=VMEM_LIMIT_BYTES),
        name="out_proj",
    )(mixed_attn, mixed_lru, w_out_bf16, x2d, mod, final_gain)


@jax.jit
def kernel(x, c, norm_gain, w_ada, b_ada, w_in, conv_w, conv_b, w_rgate, b_rgate,
           w_igate, b_igate, lru_lambda, w_out, final_gain):
    assert x.shape == (1, SEQ, D_MODEL) and norm_gain.shape[0] == 1
    x2d = x.reshape(SEQ, D_MODEL)
    mod = _ada_mod(c.reshape(D_MODEL, 1), w_ada[0], b_ada)
    proj = _in_proj(x2d, norm_gain, mod, w_in[0])
    mixed_attn = _attention(proj)
    w_gates, b_gates = _block_diag_gates(w_rgate[0], b_rgate[0], w_igate[0], b_igate[0])
    mixed_lru, w_out_bf16 = _lru(proj, conv_w[0], conv_b, w_gates, b_gates, lru_lambda, w_out[0])
    y = _out_proj(mixed_attn, mixed_lru, w_out_bf16, x2d, mod,
                  final_gain.reshape(1, D_MODEL))
    return y.reshape(1, SEQ, D_MODEL)
```

```python
import functools
import math

import jax
import jax.numpy as jnp
import numpy as np
from jax import lax
from jax.experimental import pallas as pl
from jax.experimental.pallas import tpu as pltpu

D_MODEL = 2048
SEQ = 8192
ATTN_WIDTH = D_MODEL // 2
LRU_WIDTH = D_MODEL - ATTN_WIDTH
HEAD_DIM = 64
N_HEADS = ATTN_WIDTH // HEAD_DIM
LRU_BLOCKS = 16
LRU_BLOCK_W = LRU_WIDTH // LRU_BLOCKS
CONV_WIDTH = 4
LRU_C = 8.0
DILATED_PATTERNS = ((128, 1), (512, 4), (2048, 16))
ATTN_SCALE = 1.0 / math.sqrt(HEAD_DIM)
NEG_INF = -1e30
LOG2E = math.log2(math.e)
EPS = 1e-6
PROJ_WIDTH = 4 * ATTN_WIDTH + 2 * LRU_WIDTH
MOD_SHIFT, MOD_SCALE, MOD_GATE = 0, 1, 2

LANES = 128
SUBLANES = 8
VMEM_LIMIT_BYTES = 56 * 1024 * 1024

ADA_TK = 256
PROJ_TM = 256
PROJ_W_CHUNK = 512
PROJ_W_SLOTS = 2
PROJ_VMEM_LIMIT_BYTES = 60 * 1024 * 1024
NORM_ROWS = 64
ATTN_QB = 128
ATTN_SB = 2048
ATTN_LOOKAHEAD = 1
ATTN_SLOTS = ATTN_LOOKAHEAD + 1
DEINT_ROWS = 256
DEINT_RATIO = 4
ATTN_VMEM_LIMIT_BYTES = 60 * 1024 * 1024
HEADS_PER_BLOCK = LANES // HEAD_DIM
COMBINE_ROWS = 256
STATE_PAD = SUBLANES
LRU_CHUNK = 1024
LRU_SEGMENTS = SUBLANES
LRU_SEG_LEN = SEQ // LRU_SEGMENTS
LRU_SEG_PITCH = LRU_SEG_LEN + SUBLANES
OUT_TM = 512

F32 = jnp.float32
BF16 = jnp.bfloat16


def _silu(x):
    h = 0.5 * x
    return h + h * jnp.tanh(h)


def _ada_kernel(c_ref, w_ref, b_ref, o_ref):
    @pl.when(pl.program_id(0) == 0)
    def _():
        o_ref[...] = b_ref[...]

    c_act = _silu(c_ref[...])
    o_ref[...] += jnp.sum(w_ref[...] * c_act, axis=0, keepdims=True)


def _ada_mod(c_col, w_ada, b_ada):
    n = w_ada.shape[1]
    return pl.pallas_call(
        _ada_kernel,
        grid=(D_MODEL // ADA_TK,),
        in_specs=[
            pl.BlockSpec((ADA_TK, 1), lambda k: (k, 0)),
            pl.BlockSpec((ADA_TK, n), lambda k: (k, 0)),
            pl.BlockSpec((1, n), lambda k: (0, 0)),
        ],
        out_specs=pl.BlockSpec((1, n), lambda k: (0, 0)),
        out_shape=jax.ShapeDtypeStruct((1, n), F32),
        compiler_params=pltpu.CompilerParams(
            dimension_semantics=("arbitrary",), vmem_limit_bytes=VMEM_LIMIT_BYTES),
        name="ada_mod",
    )(c_col, w_ada, b_ada)


def _w_chunk_copy(w_hbm, stage, sem, c):
    slot = c % PROJ_W_SLOTS
    cols = pl.ds(c * PROJ_W_CHUNK, PROJ_W_CHUNK)
    return pltpu.make_async_copy(w_hbm.at[:, cols], stage.at[slot], sem.at[slot])


def _proj_kernel(x0_ref, xn_ref, gain_ref, shift_ref, scale_ref, w_hbm, o_ref,
                 w_res, stage, h_even, h_odd, sem):
    s = pl.program_id(0)
    n_chunks = PROJ_WIDTH // PROJ_W_CHUNK
    slabs_per_chunk = PROJ_W_CHUNK // LANES

    def normalise(x_ref, h_ref):
        mul = gain_ref[...] * (1.0 + scale_ref[...])
        shift = shift_ref[...]
        for r0 in range(0, PROJ_TM, NORM_ROWS):
            xv = x_ref[pl.ds(r0, NORM_ROWS), :]
            var = jnp.mean(xv * xv, axis=-1, keepdims=True)
            h = xv * lax.rsqrt(var + EPS) * mul + shift
            h_ref[pl.ds(r0, NORM_ROWS), :] = h.astype(BF16)

    def project(h_ref, first_slab, n_slabs):
        cols = pl.ds(first_slab * LANES, n_slabs * LANES)
        acc = jnp.dot(h_ref[...], w_res[:, cols], preferred_element_type=F32)
        for b in range(n_slabs):
            o_ref[first_slab + b] = acc[:, b * LANES:(b + 1) * LANES]

    @pl.when(s == 0)
    def _():
        for c in range(PROJ_W_SLOTS):
            _w_chunk_copy(w_hbm, stage, sem, c).start()
        normalise(x0_ref, h_even)
        normalise(xn_ref, h_odd)
        for c in range(n_chunks):
            _w_chunk_copy(w_hbm, stage, sem, c).wait()
            cols = pl.ds(c * PROJ_W_CHUNK, PROJ_W_CHUNK)
            w_res[:, cols] = stage[c % PROJ_W_SLOTS].astype(BF16)
            if c + PROJ_W_SLOTS < n_chunks:
                _w_chunk_copy(w_hbm, stage, sem, c + PROJ_W_SLOTS).start()
            project(h_even, c * slabs_per_chunk, slabs_per_chunk)

    is_even = jnp.bitwise_and(s, 1) == 0

    @pl.when(jnp.logical_and(s > 0, is_even))
    def _():
        normalise(xn_ref, h_odd)
        project(h_even, 0, PROJ_WIDTH // LANES)

    @pl.when(jnp.logical_not(is_even))
    def _():
        normalise(xn_ref, h_even)
        project(h_odd, 0, PROJ_WIDTH // LANES)


def _in_proj(x2d, gain, mod, w_in):
    n_tiles = SEQ // PROJ_TM
    return pl.pallas_call(
        _proj_kernel,
        grid=(n_tiles,),
        in_specs=[
            pl.BlockSpec((PROJ_TM, D_MODEL), lambda s: (0, 0), pipeline_mode=pl.Buffered(1)),
            pl.BlockSpec((PROJ_TM, D_MODEL), lambda s: (jnp.minimum(s + 1, n_tiles - 1), 0)),
            pl.BlockSpec((1, D_MODEL), lambda s: (0, 0)),
            pl.BlockSpec((1, D_MODEL), lambda s: (0, MOD_SHIFT)),
            pl.BlockSpec((1, D_MODEL), lambda s: (0, MOD_SCALE)),
            pl.BlockSpec(memory_space=pl.ANY),
        ],
        out_specs=pl.BlockSpec((PROJ_WIDTH // LANES, PROJ_TM, LANES), lambda s: (0, s, 0)),
        out_shape=jax.ShapeDtypeStruct((PROJ_WIDTH // LANES, SEQ, LANES), F32),
        scratch_shapes=[
            pltpu.VMEM((D_MODEL, PROJ_WIDTH), BF16),
            pltpu.VMEM((PROJ_W_SLOTS, D_MODEL, PROJ_W_CHUNK), F32),
            pltpu.VMEM((PROJ_TM, D_MODEL), BF16),
            pltpu.VMEM((PROJ_TM, D_MODEL), BF16),
            pltpu.SemaphoreType.DMA((PROJ_W_SLOTS,)),
        ],
        compiler_params=pltpu.CompilerParams(
            dimension_semantics=("arbitrary",), vmem_limit_bytes=PROJ_VMEM_LIMIT_BYTES),
        name="in_proj",
    )(x2d, x2d, gain, mod, mod, w_in)


def _rows(start, size, stride):
    return pl.ds(start, size) if stride == 1 else pl.ds(start, size, stride=stride)


def _attn_kernel(q_ref, k_ref, v_ref, g_ref, coef_ref, o_ref,
                 kd, vd, tmp, bias_s, s_s, mx_s, state):
    sb = pl.program_id(1)
    n_p = len(DILATED_PATTERNS)
    lse_s = [state.at[p] for p in range(n_p)]
    out_s = [state.at[n_p + p] for p in range(n_p)]
    lane = lax.broadcasted_iota(jnp.int32, (ATTN_QB, LANES), 1)
    is_h0 = lane < HEAD_DIM

    @pl.when(sb == 0)
    def _():
        for src_ref, dst in ((k_ref, kd), (v_ref, vd)):
            for p, (_, d) in enumerate(DILATED_PATTERNS):
                dst[p, pl.ds(0, ATTN_QB), :] = jnp.zeros((ATTN_QB, LANES), BF16)
                d_prev = DILATED_PATTERNS[p - 1][1] if p else 1
                ratio = d // d_prev
                assert p == 0 and d == 1 or ratio == DEINT_RATIO
                per_residue = SEQ // d // DEINT_ROWS
                keep_f32 = 0 < p < len(DILATED_PATTERNS) - 1
                from_ref = src_ref if p <= 1 else tmp

                def deint(c, carry, p=p, d_prev=d_prev, ratio=ratio, per_residue=per_residue,
                          keep_f32=keep_f32, from_ref=from_ref, dst=dst):
                    r = lax.shift_right_logical(c, per_residue.bit_length() - 1)
                    chunk = jnp.bitwise_and(c, per_residue - 1)
                    r_prev = jnp.bitwise_and(r, d_prev - 1)
                    j = lax.shift_right_logical(r, d_prev.bit_length() - 1)
                    start = r_prev * (SEQ // d_prev) + chunk * (DEINT_ROWS * ratio) + j
                    x = from_ref[_rows(start, DEINT_ROWS, ratio), :]
                    if keep_f32:
                        tmp[pl.ds(pl.multiple_of(c * DEINT_ROWS, DEINT_ROWS), DEINT_ROWS), :] = x
                    rows = pl.ds(pl.multiple_of(ATTN_QB + c * DEINT_ROWS, ATTN_QB), DEINT_ROWS)
                    dst[p, rows, :] = x.astype(BF16)
                    return carry

                lax.fori_loop(0, SEQ // DEINT_ROWS, deint, 0, unroll=4)

        qi = lax.broadcasted_iota(jnp.int32, (ATTN_QB, 2 * ATTN_QB), 0)
        ki = lax.broadcasted_iota(jnp.int32, (ATTN_QB, 2 * ATTN_QB), 1)
        dist = qi + ATTN_QB - ki
        in_band = jnp.logical_and(dist >= 0, dist <= ATTN_QB)
        in_band_cur = jnp.logical_and(in_band, ki >= ATTN_QB)
        dist_f = dist.astype(F32)
        for p in range(len(DILATED_PATTERNS)):
            for h in range(HEADS_PER_BLOCK):
                pen = -(coef_ref[pl.program_id(0), p * HEADS_PER_BLOCK + h] * dist_f)
                rows = pl.ds(h * ATTN_QB, ATTN_QB)
                bias_s[p, 0, rows, :] = jnp.where(in_band, pen, NEG_INF)
                bias_s[p, 1, rows, :] = jnp.where(in_band_cur, pen, NEG_INF)

    ones = jnp.ones((2 * ATTN_QB, LANES), BF16)

    def place(d, ti):
        r, n = ti % d, ti // d
        blk = sb * (ATTN_SB // (ATTN_QB * d)) + n
        q_rows = _rows(r + n * (ATTN_QB * d), ATTN_QB, d)
        window = pl.ds(pl.multiple_of(r * (SEQ // d) + blk * ATTN_QB, ATTN_QB), 2 * ATTN_QB)
        return q_rows, window, blk == 0

    def scores(p, d, ti, slot):
        q_rows, window, first = place(d, ti)
        q = (q_ref[q_rows, :] * (ATTN_SCALE * LOG2E)).astype(BF16)
        zero = jnp.zeros_like(q)
        qs = jnp.concatenate([jnp.where(is_h0, q, zero), jnp.where(is_h0, zero, q)], axis=0)
        s = lax.dot_general(qs, kd[p, window, :], (((1,), (1,)), ((), ())),
                            preferred_element_type=F32)
        s = s + bias_s[p, first.astype(jnp.int32)]
        s_s[slot] = s
        mx_s[slot] = jnp.broadcast_to(jnp.max(s, axis=1, keepdims=True), (2 * ATTN_QB, LANES))

    def weighted(p, d, ti, slot):
        q_rows, window, _ = place(d, ti)
        mx = mx_s[slot]
        e = jnp.concatenate(
            [jnp.exp2(s_s[slot, :, :LANES] - mx), jnp.exp2(s_s[slot, :, LANES:] - mx)],
            axis=1).astype(BF16)
        vc = jnp.concatenate([vd[p, window, :], ones], axis=1)
        pv = jnp.dot(e, vc, preferred_element_type=F32)
        acc = jnp.where(is_h0, pv[:ATTN_QB, :LANES], pv[ATTN_QB:, :LANES])
        l = jnp.where(is_h0, pv[:ATTN_QB, LANES:], pv[ATTN_QB:, LANES:])
        m = jnp.where(is_h0, mx[:ATTN_QB], mx[ATTN_QB:])
        out_s[p][q_rows, :] = acc / l
        lse_s[p][q_rows, :] = m + jnp.log2(l)

    tiles = [(p, d, ti) for p, (_, d) in enumerate(DILATED_PATTERNS)
             for ti in range(ATTN_SB // ATTN_QB)]
    for k in range(min(ATTN_LOOKAHEAD, len(tiles))):
        scores(*tiles[k], k % ATTN_SLOTS)
    for k, tile in enumerate(tiles):
        if k + ATTN_LOOKAHEAD < len(tiles):
            scores(*tiles[k + ATTN_LOOKAHEAD], (k + ATTN_LOOKAHEAD) % ATTN_SLOTS)
        weighted(*tile, k % ATTN_SLOTS)

    def combine(r0):
        rows = pl.ds(r0, COMBINE_ROWS)
        lse = [lse_s[p][rows, :] for p in range(n_p)]
        top = functools.reduce(jnp.maximum, lse)
        w = [jnp.exp2(x - top) for x in lse]
        num = functools.reduce(jnp.add, [w[p] * out_s[p][rows, :] for p in range(n_p)])
        den = functools.reduce(jnp.add, w)
        o_ref[rows, :] = (num / den * _silu(g_ref[rows, :])).astype(BF16)

    @pl.when(sb >= 0)
    def _():
        for r0 in range(0, ATTN_SB, COMBINE_ROWS):
            combine(r0)


def _alibi_coefs():
    slopes = 2.0 ** (-8.0 * np.arange(1, N_HEADS + 1, dtype=np.float64) / N_HEADS)
    dil = np.array([d for _, d in DILATED_PATTERNS], dtype=np.float64)
    coef = slopes.reshape(-1, HEADS_PER_BLOCK)[:, None, :] * dil[None, :, None] * LOG2E
    return jnp.asarray(coef.reshape(N_HEADS // HEADS_PER_BLOCK, -1), dtype=F32)


def _attention(proj):
    n_hb = ATTN_WIDTH // LANES
    n_p = len(DILATED_PATTERNS)
    return pl.pallas_call(
        _attn_kernel,
        grid=(n_hb, SEQ // ATTN_SB),
        in_specs=[
            pl.BlockSpec((None, ATTN_SB, LANES), lambda h, s: (h, s, 0)),
            pl.BlockSpec((None, SEQ, LANES), lambda h, s: (n_hb + h, 0, 0)),
            pl.BlockSpec((None, SEQ, LANES), lambda h, s: (2 * n_hb + h, 0, 0)),
            pl.BlockSpec((None, ATTN_SB, LANES), lambda h, s: (3 * n_hb + h, s, 0)),
            pl.BlockSpec(memory_space=pltpu.SMEM),
        ],
        out_specs=pl.BlockSpec((ATTN_SB, LANES), lambda h, s: (s, h)),
        out_shape=jax.ShapeDtypeStruct((SEQ, ATTN_WIDTH), BF16),
        scratch_shapes=[
            pltpu.VMEM((n_p, ATTN_QB + SEQ, LANES), BF16),
            pltpu.VMEM((n_p, ATTN_QB + SEQ, LANES), BF16),
            pltpu.VMEM((SEQ, LANES), F32),
            pltpu.VMEM((n_p, 2, HEADS_PER_BLOCK * ATTN_QB, 2 * ATTN_QB), F32),
            pltpu.VMEM((ATTN_SLOTS, 2 * ATTN_QB, 2 * ATTN_QB), F32),
            pltpu.VMEM((ATTN_SLOTS, 2 * ATTN_QB, LANES), F32),
            pltpu.VMEM((2 * n_p, ATTN_SB + STATE_PAD, LANES), F32),
        ],
        compiler_params=pltpu.CompilerParams(
            dimension_semantics=("arbitrary", "arbitrary"),
            vmem_limit_bytes=ATTN_VMEM_LIMIT_BYTES),
        name="dilated_attn",
    )(proj, proj, proj, proj, _alibi_coefs())


def _lru_kernel(u_ref, g_ref, cw_ref, cb_ref, wg_ref, bg_ref, lam_ref, wout_ref,
                o_ref, wout_bf16_ref, ubuf, a_s, b_s, h_s, p_s):
    wout_bf16_ref[...] = wout_ref[...].astype(BF16)

    lam = lam_ref[...]
    log_sig = jnp.minimum(lam, 0.0) - jnp.log1p(jnp.exp(-jnp.abs(lam)))
    half_c_log_sig = (0.5 * LRU_C) * log_sig
    chunks_per_seg = LRU_SEG_LEN // LRU_CHUNK

    ubuf[pl.ds(0, SUBLANES), :] = jnp.zeros((SUBLANES, LANES), F32)

    def chunk(c, carry):
        rows = pl.ds(pl.multiple_of(c * LRU_CHUNK, LRU_CHUNK), LRU_CHUNK)
        ubuf[pl.ds(SUBLANES, LRU_CHUNK), :] = u_ref[rows, :]
        xc = cb_ref[...]
        for j in range(CONV_WIDTH):
            off = SUBLANES - (CONV_WIDTH - 1) + j
            xc = xc + ubuf[pl.ds(off, LRU_CHUNK), :] * cw_ref[pl.ds(j, 1), :]
        ubuf[pl.ds(0, SUBLANES), :] = ubuf[pl.ds(LRU_CHUNK, SUBLANES), :]

        z = jnp.dot(xc.astype(BF16), wg_ref[...], preferred_element_type=F32) + bg_ref[...]
        two_i = jnp.tanh(z[:, LANES:]) + 1.0
        log_a = half_c_log_sig * jnp.tanh(z[:, :LANES]) + half_c_log_sig
        a = jnp.exp(log_a)
        th = jnp.tanh(log_a)
        x4 = -0.5 * th / (1.0 - th)
        half_mult = jnp.where(x4 > 0.0, x4 * lax.rsqrt(x4), 0.0)
        b = (half_mult * xc) * two_i

        seg = lax.shift_right_logical(c, chunks_per_seg.bit_length() - 1)
        within = jnp.bitwise_and(c, chunks_per_seg - 1)
        dst = pl.ds(pl.multiple_of(seg * LRU_SEG_PITCH + within * LRU_CHUNK, SUBLANES), LRU_CHUNK)
        a_s[dst, :] = a
        b_s[dst, :] = b
        return carry

    lax.fori_loop(0, SEQ // LRU_CHUNK, chunk, 0, unroll=True)

    def step(t, carry):
        h, prod = carry
        rows = pl.ds(t, LRU_SEGMENTS, stride=LRU_SEG_PITCH)
        a = a_s[rows, :]
        h = a * h + b_s[rows, :]
        prod = prod * a
        h_s[rows, :] = h
        p_s[rows, :] = prod
        return h, prod

    h_end, p_end = lax.fori_loop(
        0, LRU_SEG_LEN, step,
        (jnp.zeros((LRU_SEGMENTS, LANES), F32), jnp.ones((LRU_SEGMENTS, LANES), F32)), unroll=16)

    seg_id = lax.broadcasted_iota(jnp.int32, (LRU_SEGMENTS, LANES), 0)
    c_in = jnp.zeros((LRU_SEGMENTS, LANES), F32)
    for _ in range(LRU_SEGMENTS - 1):
        c_in = jnp.where(seg_id == 0, 0.0, pltpu.roll(h_end + p_end * c_in, 1, 0))

    for s in range(LRU_SEGMENTS):
        c_s = c_in[s:s + 1, :]

        def gate(k, carry, s=s, c_s=c_s):
            off = pl.multiple_of(k * LRU_CHUNK, LRU_CHUNK)
            src = pl.ds(s * LRU_SEG_PITCH + off, LRU_CHUNK)
            rows = pl.ds(s * LRU_SEG_LEN + off, LRU_CHUNK)
            h = h_s[src, :] + p_s[src, :] * c_s
            o_ref[rows, :] = (h * _silu(g_ref[rows, :])).astype(BF16)
            return carry

        lax.fori_loop(0, chunks_per_seg, gate, 0, unroll=True)


def _lru(proj, conv_w, conv_b, w_gates, b_gates, lam, w_out):
    n_cb = LRU_WIDTH // LANES
    u_col0 = 4 * ATTN_WIDTH // LANES
    g_col0 = u_col0 + n_cb
    return pl.pallas_call(
        _lru_kernel,
        grid=(n_cb,),
        in_specs=[
            pl.BlockSpec((None, SEQ, LANES), lambda j: (u_col0 + j, 0, 0)),
            pl.BlockSpec((None, SEQ, LANES), lambda j: (g_col0 + j, 0, 0)),
            pl.BlockSpec((CONV_WIDTH, LANES), lambda j: (0, j)),
            pl.BlockSpec((1, LANES), lambda j: (0, j)),
            pl.BlockSpec((None, LANES, 2 * LANES), lambda j: (j, 0, 0)),
            pl.BlockSpec((None, 1, 2 * LANES), lambda j: (j, 0, 0)),
            pl.BlockSpec((1, LANES), lambda j: (0, j)),
            pl.BlockSpec((D_MODEL // n_cb, D_MODEL), lambda j: (j, 0)),
        ],
        out_specs=[
            pl.BlockSpec((SEQ, LANES), lambda j: (0, j)),
            pl.BlockSpec((D_MODEL // n_cb, D_MODEL), lambda j: (j, 0)),
        ],
        out_shape=[
            jax.ShapeDtypeStruct((SEQ, LRU_WIDTH), BF16),
            jax.ShapeDtypeStruct((D_MODEL, D_MODEL), BF16),
        ],
        scratch_shapes=[
            pltpu.VMEM((LRU_CHUNK + SUBLANES, LANES), F32),
        ] + [pltpu.VMEM((LRU_SEGMENTS * LRU_SEG_PITCH, LANES), F32)] * 4,
        compiler_params=pltpu.CompilerParams(
            dimension_semantics=("arbitrary",), vmem_limit_bytes=VMEM_LIMIT_BYTES),
        name="rglru",
    )(proj, proj, conv_w, conv_b, w_gates, b_gates, lam, w_out)


def _block_diag_gates(w_rgate, b_rgate, w_igate, b_igate):
    per = LANES // LRU_BLOCK_W
    n_cb = LRU_WIDTH // LANES

    def bd(w):
        w = w.reshape(n_cb, per, LRU_BLOCK_W, LRU_BLOCK_W)
        eye = jnp.eye(per, dtype=w.dtype)
        return jnp.einsum('cpkj,pq->cpkqj', w, eye).reshape(n_cb, LANES, LANES)

    w = (0.5 * jnp.concatenate([bd(w_rgate), bd(w_igate)], axis=-1)).astype(BF16)
    b = 0.5 * jnp.concatenate(
        [b_rgate.reshape(n_cb, 1, LANES), b_igate.reshape(n_cb, 1, LANES)], axis=-1)
    return w, b


def _out_kernel(ma_ref, ml_ref, w_ref, x_ref, gate_ref, fg_ref, o_ref):
    mixed = jnp.concatenate([ma_ref[...], ml_ref[...]], axis=1)
    mix = jnp.dot(mixed, w_ref[...], preferred_element_type=F32)
    y = x_ref[...] + gate_ref[...] * mix
    var = jnp.mean(y * y, axis=-1, keepdims=True)
    o_ref[...] = y * lax.rsqrt(var + EPS) * fg_ref[...]


def _out_proj(mixed_attn, mixed_lru, w_out_bf16, x2d, mod, final_gain):
    return pl.pallas_call(
        _out_kernel,
        grid=(SEQ // OUT_TM,),
        in_specs=[
            pl.BlockSpec((OUT_TM, ATTN_WIDTH), lambda i: (i, 0)),
            pl.BlockSpec((OUT_TM, LRU_WIDTH), lambda i: (i, 0)),
            pl.BlockSpec((D_MODEL, D_MODEL), lambda i: (0, 0)),
            pl.BlockSpec((OUT_TM, D_MODEL), lambda i: (i, 0)),
            pl.BlockSpec((1, D_MODEL), lambda i: (0, MOD_GATE)),
            pl.BlockSpec((1, D_MODEL), lambda i: (0, 0)),
        ],
        out_specs=pl.BlockSpec((OUT_TM, D_MODEL), lambda i: (i, 0)),
        out_shape=jax.ShapeDtypeStruct((SEQ, D_MODEL), F32),
        compiler_params=pltpu.CompilerParams(
            dimension_semantics=("arbitrary",), vmem_limit_bytes=VMEM_LIMIT_BYTES),
        name="out_proj",
    )(mixed_attn, mixed_lru, w_out_bf16, x2d, mod, final_gain)


@jax.jit
def kernel(x, c, norm_gain, w_ada, b_ada, w_in, conv_w, conv_b, w_rgate, b_rgate,
           w_igate, b_igate, lru_lambda, w_out, final_gain):
    assert x.shape == (1, SEQ, D_MODEL) and norm_gain.shape[0] == 1
    x2d = x.reshape(SEQ, D_MODEL)
    mod = _ada_mod(c.reshape(D_MODEL, 1), w_ada[0], b_ada)
    proj = _in_proj(x2d, norm_gain, mod, w_in[0])
    mixed_attn = _attention(proj)
    w_gates, b_gates = _block_diag_gates(w_rgate[0], b_rgate[0], w_igate[0], b_igate[0])
    mixed_lru, w_out_bf16 = _lru(proj, conv_w[0], conv_b, w_gates, b_gates, lru_lambda, w_out[0])
    y = _out_proj(mixed_attn, mixed_lru, w_out_bf16, x2d, mod,
                  final_gain.reshape(1, D_MODEL))
    return y.reshape(1, SEQ, D_MODEL)
```

```python
import functools
import math

import jax
import jax.numpy as jnp
import numpy as np
from jax import lax
from jax.experimental import pallas as pl
from jax.experimental.pallas import tpu as pltpu

D_MODEL = 2048
SEQ = 8192
ATTN_WIDTH = D_MODEL // 2
LRU_WIDTH = D_MODEL - ATTN_WIDTH
HEAD_DIM = 64
N_HEADS = ATTN_WIDTH // HEAD_DIM
LRU_BLOCKS = 16
LRU_BLOCK_W = LRU_WIDTH // LRU_BLOCKS
CONV_WIDTH = 4
LRU_C = 8.0
DILATED_PATTERNS = ((128, 1), (512, 4), (2048, 16))
ATTN_SCALE = 1.0 / math.sqrt(HEAD_DIM)
NEG_INF = -1e30
LOG2E = math.log2(math.e)
EPS = 1e-6
PROJ_WIDTH = 4 * ATTN_WIDTH + 2 * LRU_WIDTH
MOD_SHIFT, MOD_SCALE, MOD_GATE = 0, 1, 2

LANES = 128
SUBLANES = 8
VMEM_LIMIT_BYTES = 56 * 1024 * 1024

ADA_TK = 256
PROJ_TM = 256
PROJ_W_CHUNK = 512
PROJ_W_SLOTS = 2
PROJ_VMEM_LIMIT_BYTES = 60 * 1024 * 1024
NORM_ROWS = 64
ATTN_QB = 128
ATTN_SB = 2048
DEINT_ROWS = 256
DEINT_RATIO = 4
ATTN_VMEM_LIMIT_BYTES = 60 * 1024 * 1024
HEADS_PER_BLOCK = LANES // HEAD_DIM
STATE_PAD = SUBLANES
LRU_CHUNK = 1024
LRU_SEGMENTS = SUBLANES
LRU_SEG_LEN = SEQ // LRU_SEGMENTS
LRU_SEG_PITCH = LRU_SEG_LEN + SUBLANES
OUT_TM = 512

F32 = jnp.float32
BF16 = jnp.bfloat16


def _silu(x):
    h = 0.5 * x
    return h + h * jnp.tanh(h)


def _ada_kernel(c_ref, w_ref, b_ref, o_ref):
    @pl.when(pl.program_id(0) == 0)
    def _():
        o_ref[...] = b_ref[...]

    c_act = _silu(c_ref[...])
    o_ref[...] += jnp.sum(w_ref[...] * c_act, axis=0, keepdims=True)


def _ada_mod(c_col, w_ada, b_ada):
    n = w_ada.shape[1]
    return pl.pallas_call(
        _ada_kernel,
        grid=(D_MODEL // ADA_TK,),
        in_specs=[
            pl.BlockSpec((ADA_TK, 1), lambda k: (k, 0)),
            pl.BlockSpec((ADA_TK, n), lambda k: (k, 0)),
            pl.BlockSpec((1, n), lambda k: (0, 0)),
        ],
        out_specs=pl.BlockSpec((1, n), lambda k: (0, 0)),
        out_shape=jax.ShapeDtypeStruct((1, n), F32),
        compiler_params=pltpu.CompilerParams(
            dimension_semantics=("arbitrary",), vmem_limit_bytes=VMEM_LIMIT_BYTES),
        name="ada_mod",
    )(c_col, w_ada, b_ada)


def _w_chunk_copy(w_hbm, stage, sem, c):
    slot = c % PROJ_W_SLOTS
    cols = pl.ds(c * PROJ_W_CHUNK, PROJ_W_CHUNK)
    return pltpu.make_async_copy(w_hbm.at[:, cols], stage.at[slot], sem.at[slot])


def _proj_kernel(x0_ref, xn_ref, gain_ref, shift_ref, scale_ref, w_hbm, o_ref,
                 w_res, stage, h_even, h_odd, sem):
    s = pl.program_id(0)
    n_chunks = PROJ_WIDTH // PROJ_W_CHUNK
    slabs_per_chunk = PROJ_W_CHUNK // LANES

    def normalise(x_ref, h_ref):
        mul = gain_ref[...] * (1.0 + scale_ref[...])
        shift = shift_ref[...]
        for r0 in range(0, PROJ_TM, NORM_ROWS):
            xv = x_ref[pl.ds(r0, NORM_ROWS), :]
            var = jnp.mean(xv * xv, axis=-1, keepdims=True)
            h = xv * lax.rsqrt(var + EPS) * mul + shift
            h_ref[pl.ds(r0, NORM_ROWS), :] = h.astype(BF16)

    def project(h_ref, first_slab, n_slabs):
        cols = pl.ds(first_slab * LANES, n_slabs * LANES)
        acc = jnp.dot(h_ref[...], w_res[:, cols], preferred_element_type=F32)
        for b in range(n_slabs):
            o_ref[first_slab + b] = acc[:, b * LANES:(b + 1) * LANES]

    @pl.when(s == 0)
    def _():
        for c in range(PROJ_W_SLOTS):
            _w_chunk_copy(w_hbm, stage, sem, c).start()
        normalise(x0_ref, h_even)
        normalise(xn_ref, h_odd)
        for c in range(n_chunks):
            _w_chunk_copy(w_hbm, stage, sem, c).wait()
            cols = pl.ds(c * PROJ_W_CHUNK, PROJ_W_CHUNK)
            w_res[:, cols] = stage[c % PROJ_W_SLOTS].astype(BF16)
            if c + PROJ_W_SLOTS < n_chunks:
                _w_chunk_copy(w_hbm, stage, sem, c + PROJ_W_SLOTS).start()
            project(h_even, c * slabs_per_chunk, slabs_per_chunk)

    is_even = jnp.bitwise_and(s, 1) == 0

    @pl.when(jnp.logical_and(s > 0, is_even))
    def _():
        normalise(xn_ref, h_odd)
        project(h_even, 0, PROJ_WIDTH // LANES)

    @pl.when(jnp.logical_not(is_even))
    def _():
        normalise(xn_ref, h_even)
        project(h_odd, 0, PROJ_WIDTH // LANES)


def _in_proj(x2d, gain, mod, w_in):
    n_tiles = SEQ // PROJ_TM
    return pl.pallas_call(
        _proj_kernel,
        grid=(n_tiles,),
        in_specs=[
            pl.BlockSpec((PROJ_TM, D_MODEL), lambda s: (0, 0), pipeline_mode=pl.Buffered(1)),
            pl.BlockSpec((PROJ_TM, D_MODEL), lambda s: (jnp.minimum(s + 1, n_tiles - 1), 0)),
            pl.BlockSpec((1, D_MODEL), lambda s: (0, 0)),
            pl.BlockSpec((1, D_MODEL), lambda s: (0, MOD_SHIFT)),
            pl.BlockSpec((1, D_MODEL), lambda s: (0, MOD_SCALE)),
            pl.BlockSpec(memory_space=pl.ANY),
        ],
        out_specs=pl.BlockSpec((PROJ_WIDTH // LANES, PROJ_TM, LANES), lambda s: (0, s, 0)),
        out_shape=jax.ShapeDtypeStruct((PROJ_WIDTH // LANES, SEQ, LANES), F32),
        scratch_shapes=[
            pltpu.VMEM((D_MODEL, PROJ_WIDTH), BF16),
            pltpu.VMEM((PROJ_W_SLOTS, D_MODEL, PROJ_W_CHUNK), F32),
            pltpu.VMEM((PROJ_TM, D_MODEL), BF16),
            pltpu.VMEM((PROJ_TM, D_MODEL), BF16),
            pltpu.SemaphoreType.DMA((PROJ_W_SLOTS,)),
        ],
        compiler_params=pltpu.CompilerParams(
            dimension_semantics=("arbitrary",), vmem_limit_bytes=PROJ_VMEM_LIMIT_BYTES),
        name="in_proj",
    )(x2d, x2d, gain, mod, mod, w_in)


def _rows(start, size, stride):
    return pl.ds(start, size) if stride == 1 else pl.ds(start, size, stride=stride)


def _attn_kernel(q_ref, k_ref, v_ref, g_ref, coef_ref, o_ref,
                 kd, vd, tmp, bias_s, state):
    sb = pl.program_id(1)
    n_p = len(DILATED_PATTERNS)
    assert DILATED_PATTERNS[0][1] == 1
    lse_s = [None] + [state.at[p - 1] for p in range(1, n_p)]
    out_s = [None] + [state.at[n_p - 1 + p - 1] for p in range(1, n_p)]
    lane = lax.broadcasted_iota(jnp.int32, (ATTN_QB, LANES), 1)
    is_h0 = lane < HEAD_DIM

    @pl.when(sb == 0)
    def _():
        for src_ref, dst in ((k_ref, kd), (v_ref, vd)):
            for p, (_, d) in enumerate(DILATED_PATTERNS):
                dst[p, pl.ds(0, ATTN_QB), :] = jnp.zeros((ATTN_QB, LANES), BF16)
                d_prev = DILATED_PATTERNS[p - 1][1] if p else 1
                ratio = d // d_prev
                assert p == 0 and d == 1 or ratio == DEINT_RATIO
                per_residue = SEQ // d // DEINT_ROWS
                keep_f32 = 0 < p < len(DILATED_PATTERNS) - 1
                from_ref = src_ref if p <= 1 else tmp

                def deint(c, carry, p=p, d_prev=d_prev, ratio=ratio, per_residue=per_residue,
                          keep_f32=keep_f32, from_ref=from_ref, dst=dst):
                    r = lax.shift_right_logical(c, per_residue.bit_length() - 1)
                    chunk = jnp.bitwise_and(c, per_residue - 1)
                    r_prev = jnp.bitwise_and(r, d_prev - 1)
                    j = lax.shift_right_logical(r, d_prev.bit_length() - 1)
                    start = r_prev * (SEQ // d_prev) + chunk * (DEINT_ROWS * ratio) + j
                    x = from_ref[_rows(start, DEINT_ROWS, ratio), :]
                    if keep_f32:
                        tmp[pl.ds(pl.multiple_of(c * DEINT_ROWS, DEINT_ROWS), DEINT_ROWS), :] = x
                    rows = pl.ds(pl.multiple_of(ATTN_QB + c * DEINT_ROWS, ATTN_QB), DEINT_ROWS)
                    dst[p, rows, :] = x.astype(BF16)
                    return carry

                lax.fori_loop(0, SEQ // DEINT_ROWS, deint, 0, unroll=4)

        qi = lax.broadcasted_iota(jnp.int32, (ATTN_QB, 2 * ATTN_QB), 0)
        ki = lax.broadcasted_iota(jnp.int32, (ATTN_QB, 2 * ATTN_QB), 1)
        dist = qi + ATTN_QB - ki
        in_band = jnp.logical_and(dist >= 0, dist <= ATTN_QB)
        in_band_cur = jnp.logical_and(in_band, ki >= ATTN_QB)
        dist_f = dist.astype(F32)
        for p in range(len(DILATED_PATTERNS)):
            for h in range(HEADS_PER_BLOCK):
                pen = -(coef_ref[pl.program_id(0), p * HEADS_PER_BLOCK + h] * dist_f)
                rows = pl.ds(h * ATTN_QB, ATTN_QB)
                bias_s[p, 0, rows, :] = jnp.where(in_band, pen, NEG_INF)
                bias_s[p, 1, rows, :] = jnp.where(in_band_cur, pen, NEG_INF)

    ones = jnp.ones((2 * ATTN_QB, LANES), BF16)

    def place(d, ti):
        r, n = ti % d, ti // d
        blk = sb * (ATTN_SB // (ATTN_QB * d)) + n
        q_rows = _rows(r + n * (ATTN_QB * d), ATTN_QB, d)
        window = pl.ds(pl.multiple_of(r * (SEQ // d) + blk * ATTN_QB, ATTN_QB), 2 * ATTN_QB)
        return q_rows, window, blk == 0

    def tile(p, d, ti):
        q_rows, window, first = place(d, ti)
        q = (q_ref[q_rows, :] * (ATTN_SCALE * LOG2E)).astype(BF16)
        zero = jnp.zeros_like(q)
        qs = jnp.concatenate([jnp.where(is_h0, q, zero), jnp.where(is_h0, zero, q)], axis=0)
        s = lax.dot_general(qs, kd[p, window, :], (((1,), (1,)), ((), ())),
                            preferred_element_type=F32)
        s = s + bias_s[p, first.astype(jnp.int32)]
        mx = jnp.broadcast_to(jnp.max(s, axis=1, keepdims=True), (2 * ATTN_QB, LANES))
        e = jnp.concatenate(
            [jnp.exp2(s[:, :LANES] - mx), jnp.exp2(s[:, LANES:] - mx)], axis=1).astype(BF16)
        vc = jnp.concatenate([vd[p, window, :], ones], axis=1)
        pv = jnp.dot(e, vc, preferred_element_type=F32)
        acc = jnp.where(is_h0, pv[:ATTN_QB, :LANES], pv[ATTN_QB:, :LANES])
        l = jnp.where(is_h0, pv[:ATTN_QB, LANES:], pv[ATTN_QB:, LANES:])
        m = jnp.where(is_h0, mx[:ATTN_QB], mx[ATTN_QB:])
        out_t = acc / l
        lse_t = m + jnp.log2(l)
        if p > 0:
            out_s[p][q_rows, :] = out_t
            lse_s[p][q_rows, :] = lse_t
        else:
            lse = [lse_t] + [lse_s[j][q_rows, :] for j in range(1, n_p)]
            outs = [out_t] + [out_s[j][q_rows, :] for j in range(1, n_p)]
            top = functools.reduce(jnp.maximum, lse)
            w = [jnp.exp2(x - top) for x in lse]
            num = functools.reduce(jnp.add, [w[j] * outs[j] for j in range(n_p)])
            den = functools.reduce(jnp.add, w)
            o_ref[q_rows, :] = (num / den * _silu(g_ref[q_rows, :])).astype(BF16)

    for p in reversed(range(n_p)):
        for ti in range(ATTN_SB // ATTN_QB):
            tile(p, DILATED_PATTERNS[p][1], ti)


def _alibi_coefs():
    slopes = 2.0 ** (-8.0 * np.arange(1, N_HEADS + 1, dtype=np.float64) / N_HEADS)
    dil = np.array([d for _, d in DILATED_PATTERNS], dtype=np.float64)
    coef = slopes.reshape(-1, HEADS_PER_BLOCK)[:, None, :] * dil[None, :, None] * LOG2E
    return jnp.asarray(coef.reshape(N_HEADS // HEADS_PER_BLOCK, -1), dtype=F32)


def _attention(proj):
    n_hb = ATTN_WIDTH // LANES
    n_p = len(DILATED_PATTERNS)
    return pl.pallas_call(
        _attn_kernel,
        grid=(n_hb, SEQ // ATTN_SB),
        in_specs=[
            pl.BlockSpec((None, ATTN_SB, LANES), lambda h, s: (h, s, 0)),
            pl.BlockSpec((None, SEQ, LANES), lambda h, s: (n_hb + h, 0, 0)),
            pl.BlockSpec((None, SEQ, LANES), lambda h, s: (2 * n_hb + h, 0, 0)),
            pl.BlockSpec((None, ATTN_SB, LANES), lambda h, s: (3 * n_hb + h, s, 0)),
            pl.BlockSpec(memory_space=pltpu.SMEM),
        ],
        out_specs=pl.BlockSpec((ATTN_SB, LANES), lambda h, s: (s, h)),
        out_shape=jax.ShapeDtypeStruct((SEQ, ATTN_WIDTH), BF16),
        scratch_shapes=[
            pltpu.VMEM((n_p, ATTN_QB + SEQ, LANES), BF16),
            pltpu.VMEM((n_p, ATTN_QB + SEQ, LANES), BF16),
            pltpu.VMEM((SEQ, LANES), F32),
            pltpu.VMEM((n_p, 2, HEADS_PER_BLOCK * ATTN_QB, 2 * ATTN_QB), F32),
            pltpu.VMEM((2 * (n_p - 1), ATTN_SB + STATE_PAD, LANES), F32),
        ],
        compiler_params=pltpu.CompilerParams(
            dimension_semantics=("arbitrary", "arbitrary"),
            vmem_limit_bytes=ATTN_VMEM_LIMIT_BYTES),
        name="dilated_attn",
    )(proj, proj, proj, proj, _alibi_coefs())


def _lru_kernel(u_ref, g_ref, cw_ref, cb_ref, wg_ref, bg_ref, lam_ref, wout_ref,
                o_ref, wout_bf16_ref, ubuf, a_s, b_s, h_s, p_s):
    wout_bf16_ref[...] = wout_ref[...].astype(BF16)

    lam = lam_ref[...]
    log_sig = jnp.minimum(lam, 0.0) - jnp.log1p(jnp.exp(-jnp.abs(lam)))
    half_c_log_sig = (0.5 * LRU_C) * log_sig
    chunks_per_seg = LRU_SEG_LEN // LRU_CHUNK

    ubuf[pl.ds(0, SUBLANES), :] = jnp.zeros((SUBLANES, LANES), F32)

    def chunk(c, carry):
        rows = pl.ds(pl.multiple_of(c * LRU_CHUNK, LRU_CHUNK), LRU_CHUNK)
        ubuf[pl.ds(SUBLANES, LRU_CHUNK), :] = u_ref[rows, :]
        xc = cb_ref[...]
        for j in range(CONV_WIDTH):
            off = SUBLANES - (CONV_WIDTH - 1) + j
            xc = xc + ubuf[pl.ds(off, LRU_CHUNK), :] * cw_ref[pl.ds(j, 1), :]
        ubuf[pl.ds(0, SUBLANES), :] = ubuf[pl.ds(LRU_CHUNK, SUBLANES), :]

        z = jnp.dot(xc.astype(BF16), wg_ref[...], preferred_element_type=F32) + bg_ref[...]
        two_i = jnp.tanh(z[:, LANES:]) + 1.0
        log_a = half_c_log_sig * jnp.tanh(z[:, :LANES]) + half_c_log_sig
        a = jnp.exp(log_a)
        th = jnp.tanh(log_a)
        x4 = -0.5 * th / (1.0 - th)
        half_mult = jnp.where(x4 > 0.0, x4 * lax.rsqrt(x4), 0.0)
        b = (half_mult * xc) * two_i

        seg = lax.shift_right_logical(c, chunks_per_seg.bit_length() - 1)
        within = jnp.bitwise_and(c, chunks_per_seg - 1)
        dst = pl.ds(pl.multiple_of(seg * LRU_SEG_PITCH + within * LRU_CHUNK, SUBLANES), LRU_CHUNK)
        a_s[dst, :] = a
        b_s[dst, :] = b
        return carry

    lax.fori_loop(0, SEQ // LRU_CHUNK, chunk, 0, unroll=True)

    def step(t, carry):
        h, prod = carry
        rows = pl.ds(t, LRU_SEGMENTS, stride=LRU_SEG_PITCH)
        a = a_s[rows, :]
        h = a * h + b_s[rows, :]
        prod = prod * a
        h_s[rows, :] = h
        p_s[rows, :] = prod
        return h, prod

    h_end, p_end = lax.fori_loop(
        0, LRU_SEG_LEN, step,
        (jnp.zeros((LRU_SEGMENTS, LANES), F32), jnp.ones((LRU_SEGMENTS, LANES), F32)), unroll=128)

    seg_id = lax.broadcasted_iota(jnp.int32, (LRU_SEGMENTS, LANES), 0)
    c_in = jnp.zeros((LRU_SEGMENTS, LANES), F32)
    for _ in range(LRU_SEGMENTS - 1):
        c_in = jnp.where(seg_id == 0, 0.0, pltpu.roll(h_end + p_end * c_in, 1, 0))

    for s in range(LRU_SEGMENTS):
        c_s = c_in[s:s + 1, :]

        def gate(k, carry, s=s, c_s=c_s):
            off = pl.multiple_of(k * LRU_CHUNK, LRU_CHUNK)
            src = pl.ds(s * LRU_SEG_PITCH + off, LRU_CHUNK)
            rows = pl.ds(s * LRU_SEG_LEN + off, LRU_CHUNK)
            h = h_s[src, :] + p_s[src, :] * c_s
            o_ref[rows, :] = (h * _silu(g_ref[rows, :])).astype(BF16)
            return carry

        lax.fori_loop(0, chunks_per_seg, gate, 0, unroll=True)


def _lru(proj, conv_w, conv_b, w_gates, b_gates, lam, w_out):
    n_cb = LRU_WIDTH // LANES
    u_col0 = 4 * ATTN_WIDTH // LANES
    g_col0 = u_col0 + n_cb
    return pl.pallas_call(
        _lru_kernel,
        grid=(n_cb,),
        in_specs=[
            pl.BlockSpec((None, SEQ, LANES), lambda j: (u_col0 + j, 0, 0)),
            pl.BlockSpec((None, SEQ, LANES), lambda j: (g_col0 + j, 0, 0)),
            pl.BlockSpec((CONV_WIDTH, LANES), lambda j: (0, j)),
            pl.BlockSpec((1, LANES), lambda j: (0, j)),
            pl.BlockSpec((None, LANES, 2 * LANES), lambda j: (j, 0, 0)),
            pl.BlockSpec((None, 1, 2 * LANES), lambda j: (j, 0, 0)),
            pl.BlockSpec((1, LANES), lambda j: (0, j)),
            pl.BlockSpec((D_MODEL // n_cb, D_MODEL), lambda j: (j, 0)),
        ],
        out_specs=[
            pl.BlockSpec((SEQ, LANES), lambda j: (0, j)),
            pl.BlockSpec((D_MODEL // n_cb, D_MODEL), lambda j: (j, 0)),
        ],
        out_shape=[
            jax.ShapeDtypeStruct((SEQ, LRU_WIDTH), BF16),
            jax.ShapeDtypeStruct((D_MODEL, D_MODEL), BF16),
        ],
        scratch_shapes=[
            pltpu.VMEM((LRU_CHUNK + SUBLANES, LANES), F32),
        ] + [pltpu.VMEM((LRU_SEGMENTS * LRU_SEG_PITCH, LANES), F32)] * 4,
        compiler_params=pltpu.CompilerParams(
            dimension_semantics=("arbitrary",), vmem_limit_bytes=VMEM_LIMIT_BYTES),
        name="rglru",
    )(proj, proj, conv_w, conv_b, w_gates, b_gates, lam, w_out)


def _block_diag_gates(w_rgate, b_rgate, w_igate, b_igate):
    per = LANES // LRU_BLOCK_W
    n_cb = LRU_WIDTH // LANES

    def bd(w):
        w = w.reshape(n_cb, per, LRU_BLOCK_W, LRU_BLOCK_W)
        eye = jnp.eye(per, dtype=w.dtype)
        return jnp.einsum('cpkj,pq->cpkqj', w, eye).reshape(n_cb, LANES, LANES)

    w = (0.5 * jnp.concatenate([bd(w_rgate), bd(w_igate)], axis=-1)).astype(BF16)
    b = 0.5 * jnp.concatenate(
        [b_rgate.reshape(n_cb, 1, LANES), b_igate.reshape(n_cb, 1, LANES)], axis=-1)
    return w, b


def _out_kernel(ma_ref, ml_ref, w_ref, x_ref, gate_ref, fg_ref, o_ref):
    mixed = jnp.concatenate([ma_ref[...], ml_ref[...]], axis=1)
    mix = jnp.dot(mixed, w_ref[...], preferred_element_type=F32)
    y = x_ref[...] + gate_ref[...] * mix
    var = jnp.mean(y * y, axis=-1, keepdims=True)
    o_ref[...] = y * lax.rsqrt(var + EPS) * fg_ref[...]


def _out_proj(mixed_attn, mixed_lru, w_out_bf16, x2d, mod, final_gain):
    return pl.pallas_call(
        _out_kernel,
        grid=(SEQ // OUT_TM,),
        in_specs=[
            pl.BlockSpec((OUT_TM, ATTN_WIDTH), lambda i: (i, 0)),
            pl.BlockSpec((OUT_TM, LRU_WIDTH), lambda i: (i, 0)),
            pl.BlockSpec((D_MODEL, D_MODEL), lambda i: (0, 0)),
            pl.BlockSpec((OUT_TM, D_MODEL), lambda i: (i, 0)),
            pl.BlockSpec((1, D_MODEL), lambda i: (0, MOD_GATE)),
            pl.BlockSpec((1, D_MODEL), lambda i: (0, 0)),
        ],
        out_specs=pl.BlockSpec((OUT_TM, D_MODEL), lambda i: (i, 0)),
        out_shape=jax.ShapeDtypeStruct((SEQ, D_MODEL), F32),
        compiler_params=pltpu.CompilerParams(
            dimension_semantics=("arbitrary",), vmem_limit_bytes=VMEM_LIMIT_BYTES),
        name="out_proj",
    )(mixed_attn, mixed_lru, w_out_bf16, x2d, mod, final_gain)


@jax.jit
def kernel(x, c, norm_gain, w_ada, b_ada, w_in, conv_w, conv_b, w_rgate, b_rgate,
           w_igate, b_igate, lru_lambda, w_out, final_gain):
    assert x.shape == (1, SEQ, D_MODEL) and norm_gain.shape[0] == 1
    x2d = x.reshape(SEQ, D_MODEL)
    mod = _ada_mod(c.reshape(D_MODEL, 1), w_ada[0], b_ada)
    proj = _in_proj(x2d, norm_gain, mod, w_in[0])
    mixed_attn = _attention(proj)
    w_gates, b_gates = _block_diag_gates(w_rgate[0], b_rgate[0], w_igate[0], b_igate[0])
    mixed_lru, w_out_bf16 = _lru(proj, conv_w[0], conv_b, w_gates, b_gates, lru_lambda, w_out[0])
    y = _out_proj(mixed_attn, mixed_lru, w_out_bf16, x2d, mod,
                  final_gain.reshape(1, D_MODEL))
    return y.reshape(1, SEQ, D_MODEL)
```

```python
import functools
import math

import jax
import jax.numpy as jnp
import numpy as np
from jax import lax
from jax.experimental import pallas as pl
from jax.experimental.pallas import tpu as pltpu

D_MODEL = 2048
SEQ = 8192
ATTN_WIDTH = D_MODEL // 2
LRU_WIDTH = D_MODEL - ATTN_WIDTH
HEAD_DIM = 64
N_HEADS = ATTN_WIDTH // HEAD_DIM
LRU_BLOCKS = 16
LRU_BLOCK_W = LRU_WIDTH // LRU_BLOCKS
CONV_WIDTH = 4
LRU_C = 8.0
DILATED_PATTERNS = ((128, 1), (512, 4), (2048, 16))
ATTN_SCALE = 1.0 / math.sqrt(HEAD_DIM)
NEG_INF = -1e30
LOG2E = math.log2(math.e)
EPS = 1e-6
PROJ_WIDTH = 4 * ATTN_WIDTH + 2 * LRU_WIDTH
MOD_SHIFT, MOD_SCALE, MOD_GATE = 0, 1, 2

LANES = 128
SUBLANES = 8
VMEM_LIMIT_BYTES = 56 * 1024 * 1024

ADA_TK = 256
PROJ_TM = 256
PROJ_W_CHUNK = 512
PROJ_W_SLOTS = 2
PROJ_VMEM_LIMIT_BYTES = 60 * 1024 * 1024
NORM_ROWS = 64
ATTN_QB = 128
ATTN_SB = 2048
DEINT_ROWS = 256
DEINT_RATIO = 4
ATTN_VMEM_LIMIT_BYTES = 60 * 1024 * 1024
HEADS_PER_BLOCK = LANES // HEAD_DIM
STATE_PAD = SUBLANES
LRU_CHUNK = 1024
LRU_SEGMENTS = SUBLANES
LRU_SEG_LEN = SEQ // LRU_SEGMENTS
LRU_SEG_PITCH = LRU_SEG_LEN + SUBLANES
OUT_TM = 512

F32 = jnp.float32
BF16 = jnp.bfloat16


def _silu(x):
    h = 0.5 * x
    return h + h * jnp.tanh(h)


def _ada_kernel(c_ref, w_ref, b_ref, o_ref):
    @pl.when(pl.program_id(0) == 0)
    def _():
        o_ref[...] = b_ref[...]

    c_act = _silu(c_ref[...])
    o_ref[...] += jnp.sum(w_ref[...] * c_act, axis=0, keepdims=True)


def _ada_mod(c_col, w_ada, b_ada):
    n = w_ada.shape[1]
    return pl.pallas_call(
        _ada_kernel,
        grid=(D_MODEL // ADA_TK,),
        in_specs=[
            pl.BlockSpec((ADA_TK, 1), lambda k: (k, 0)),
            pl.BlockSpec((ADA_TK, n), lambda k: (k, 0)),
            pl.BlockSpec((1, n), lambda k: (0, 0)),
        ],
        out_specs=pl.BlockSpec((1, n), lambda k: (0, 0)),
        out_shape=jax.ShapeDtypeStruct((1, n), F32),
        compiler_params=pltpu.CompilerParams(
            dimension_semantics=("arbitrary",), vmem_limit_bytes=VMEM_LIMIT_BYTES),
        name="ada_mod",
    )(c_col, w_ada, b_ada)


def _w_chunk_copy(w_hbm, stage, sem, c):
    slot = c % PROJ_W_SLOTS
    cols = pl.ds(c * PROJ_W_CHUNK, PROJ_W_CHUNK)
    return pltpu.make_async_copy(w_hbm.at[:, cols], stage.at[slot], sem.at[slot])


def _proj_kernel(x0_ref, xn_ref, gain_ref, shift_ref, scale_ref, w_hbm, o_ref,
                 w_res, stage, h_even, h_odd, sem):
    s = pl.program_id(0)
    n_chunks = PROJ_WIDTH // PROJ_W_CHUNK
    slabs_per_chunk = PROJ_W_CHUNK // LANES

    def normalise(x_ref, h_ref):
        mul = gain_ref[...] * (1.0 + scale_ref[...])
        shift = shift_ref[...]
        for r0 in range(0, PROJ_TM, NORM_ROWS):
            xv = x_ref[pl.ds(r0, NORM_ROWS), :]
            var = jnp.mean(xv * xv, axis=-1, keepdims=True)
            h = xv * lax.rsqrt(var + EPS) * mul + shift
            h_ref[pl.ds(r0, NORM_ROWS), :] = h.astype(BF16)

    def project(h_ref, first_slab, n_slabs):
        cols = pl.ds(first_slab * LANES, n_slabs * LANES)
        acc = jnp.dot(h_ref[...], w_res[:, cols], preferred_element_type=F32)
        for b in range(n_slabs):
            o_ref[first_slab + b] = acc[:, b * LANES:(b + 1) * LANES]

    @pl.when(s == 0)
    def _():
        for c in range(PROJ_W_SLOTS):
            _w_chunk_copy(w_hbm, stage, sem, c).start()
        normalise(x0_ref, h_even)
        normalise(xn_ref, h_odd)
        for c in range(n_chunks):
            _w_chunk_copy(w_hbm, stage, sem, c).wait()
            cols = pl.ds(c * PROJ_W_CHUNK, PROJ_W_CHUNK)
            w_res[:, cols] = stage[c % PROJ_W_SLOTS].astype(BF16)
            if c + PROJ_W_SLOTS < n_chunks:
                _w_chunk_copy(w_hbm, stage, sem, c + PROJ_W_SLOTS).start()
            project(h_even, c * slabs_per_chunk, slabs_per_chunk)

    is_even = jnp.bitwise_and(s, 1) == 0

    @pl.when(jnp.logical_and(s > 0, is_even))
    def _():
        normalise(xn_ref, h_odd)
        project(h_even, 0, PROJ_WIDTH // LANES)

    @pl.when(jnp.logical_not(is_even))
    def _():
        normalise(xn_ref, h_even)
        project(h_odd, 0, PROJ_WIDTH // LANES)


def _in_proj(x2d, gain, mod, w_in):
    n_tiles = SEQ // PROJ_TM
    return pl.pallas_call(
        _proj_kernel,
        grid=(n_tiles,),
        in_specs=[
            pl.BlockSpec((PROJ_TM, D_MODEL), lambda s: (0, 0), pipeline_mode=pl.Buffered(1)),
            pl.BlockSpec((PROJ_TM, D_MODEL), lambda s: (jnp.minimum(s + 1, n_tiles - 1), 0)),
            pl.BlockSpec((1, D_MODEL), lambda s: (0, 0)),
            pl.BlockSpec((1, D_MODEL), lambda s: (0, MOD_SHIFT)),
            pl.BlockSpec((1, D_MODEL), lambda s: (0, MOD_SCALE)),
            pl.BlockSpec(memory_space=pl.ANY),
        ],
        out_specs=pl.BlockSpec((PROJ_WIDTH // LANES, PROJ_TM, LANES), lambda s: (0, s, 0)),
        out_shape=jax.ShapeDtypeStruct((PROJ_WIDTH // LANES, SEQ, LANES), F32),
        scratch_shapes=[
            pltpu.VMEM((D_MODEL, PROJ_WIDTH), BF16),
            pltpu.VMEM((PROJ_W_SLOTS, D_MODEL, PROJ_W_CHUNK), F32),
            pltpu.VMEM((PROJ_TM, D_MODEL), BF16),
            pltpu.VMEM((PROJ_TM, D_MODEL), BF16),
            pltpu.SemaphoreType.DMA((PROJ_W_SLOTS,)),
        ],
        compiler_params=pltpu.CompilerParams(
            dimension_semantics=("arbitrary",), vmem_limit_bytes=PROJ_VMEM_LIMIT_BYTES),
        name="in_proj",
    )(x2d, x2d, gain, mod, mod, w_in)


def _rows(start, size, stride):
    return pl.ds(start, size) if stride == 1 else pl.ds(start, size, stride=stride)


def _attn_kernel(q_ref, k_ref, v_ref, g_ref, coef_ref, o_ref,
                 kd, vd, tmp, bias_s, state):
    sb = pl.program_id(1)
    n_p = len(DILATED_PATTERNS)
    assert DILATED_PATTERNS[0][1] == 1
    lse_s = [None] + [state.at[p] for p in range(1, n_p)]
    out_s = [None] + [state.at[n_p + p] for p in range(1, n_p)]
    lane = lax.broadcasted_iota(jnp.int32, (ATTN_QB, LANES), 1)
    is_h0 = lane < HEAD_DIM

    @pl.when(sb == 0)
    def _():
        for src_ref, dst in ((k_ref, kd), (v_ref, vd)):
            for p, (_, d) in enumerate(DILATED_PATTERNS):
                dst[p, pl.ds(0, ATTN_QB), :] = jnp.zeros((ATTN_QB, LANES), BF16)
                d_prev = DILATED_PATTERNS[p - 1][1] if p else 1
                ratio = d // d_prev
                assert p == 0 and d == 1 or ratio == DEINT_RATIO
                per_residue = SEQ // d // DEINT_ROWS
                keep_f32 = 0 < p < len(DILATED_PATTERNS) - 1
                from_ref = src_ref if p <= 1 else tmp

                def deint(c, carry, p=p, d_prev=d_prev, ratio=ratio, per_residue=per_residue,
                          keep_f32=keep_f32, from_ref=from_ref, dst=dst):
                    r = lax.shift_right_logical(c, per_residue.bit_length() - 1)
                    chunk = jnp.bitwise_and(c, per_residue - 1)
                    r_prev = jnp.bitwise_and(r, d_prev - 1)
                    j = lax.shift_right_logical(r, d_prev.bit_length() - 1)
                    start = r_prev * (SEQ // d_prev) + chunk * (DEINT_ROWS * ratio) + j
                    x = from_ref[_rows(start, DEINT_ROWS, ratio), :]
                    if keep_f32:
                        tmp[pl.ds(pl.multiple_of(c * DEINT_ROWS, DEINT_ROWS), DEINT_ROWS), :] = x
                    rows = pl.ds(pl.multiple_of(ATTN_QB + c * DEINT_ROWS, ATTN_QB), DEINT_ROWS)
                    dst[p, rows, :] = x.astype(BF16)
                    return carry

                lax.fori_loop(0, SEQ // DEINT_ROWS, deint, 0, unroll=4)

        qi = lax.broadcasted_iota(jnp.int32, (ATTN_QB, 2 * ATTN_QB), 0)
        ki = lax.broadcasted_iota(jnp.int32, (ATTN_QB, 2 * ATTN_QB), 1)
        dist = qi + ATTN_QB - ki
        in_band = jnp.logical_and(dist >= 0, dist <= ATTN_QB)
        in_band_cur = jnp.logical_and(in_band, ki >= ATTN_QB)
        dist_f = dist.astype(F32)
        for p in range(len(DILATED_PATTERNS)):
            for h in range(HEADS_PER_BLOCK):
                pen = -(coef_ref[pl.program_id(0), p * HEADS_PER_BLOCK + h] * dist_f)
                rows = pl.ds(h * ATTN_QB, ATTN_QB)
                bias_s[p, 0, rows, :] = jnp.where(in_band, pen, NEG_INF)
                bias_s[p, 1, rows, :] = jnp.where(in_band_cur, pen, NEG_INF)

    ones = jnp.ones((2 * ATTN_QB, LANES), BF16)

    def place(d, ti):
        r, n = ti % d, ti // d
        blk = sb * (ATTN_SB // (ATTN_QB * d)) + n
        q_rows = _rows(r + n * (ATTN_QB * d), ATTN_QB, d)
        window = pl.ds(pl.multiple_of(r * (SEQ // d) + blk * ATTN_QB, ATTN_QB), 2 * ATTN_QB)
        return q_rows, window, blk == 0

    def tile(p, d, ti):
        q_rows, window, first = place(d, ti)
        q = (q_ref[q_rows, :] * (ATTN_SCALE * LOG2E)).astype(BF16)
        zero = jnp.zeros_like(q)
        qs = jnp.concatenate([jnp.where(is_h0, q, zero), jnp.where(is_h0, zero, q)], axis=0)
        s = lax.dot_general(qs, kd[p, window, :], (((1,), (1,)), ((), ())),
                            preferred_element_type=F32)
        s = s + bias_s[p, first.astype(jnp.int32)]
        mx = jnp.broadcast_to(jnp.max(s, axis=1, keepdims=True), (2 * ATTN_QB, LANES))
        e = jnp.concatenate(
            [jnp.exp2(s[:, :LANES] - mx), jnp.exp2(s[:, LANES:] - mx)], axis=1).astype(BF16)
        vc = jnp.concatenate([vd[p, window, :], ones], axis=1)
        pv = jnp.dot(e, vc, preferred_element_type=F32)
        acc = jnp.where(is_h0, pv[:ATTN_QB, :LANES], pv[ATTN_QB:, :LANES])
        l = jnp.where(is_h0, pv[:ATTN_QB, LANES:], pv[ATTN_QB:, LANES:])
        m = jnp.where(is_h0, mx[:ATTN_QB], mx[ATTN_QB:])
        out_t = acc / l
        lse_t = m + jnp.log2(l)
        if p > 0:
            out_s[p][q_rows, :] = out_t
            lse_s[p][q_rows, :] = lse_t
        else:
            lse = [lse_t] + [lse_s[j][q_rows, :] for j in range(1, n_p)]
            outs = [out_t] + [out_s[j][q_rows, :] for j in range(1, n_p)]
            top = functools.reduce(jnp.maximum, lse)
            w = [jnp.exp2(x - top) for x in lse]
            num = functools.reduce(jnp.add, [w[j] * outs[j] for j in range(n_p)])
            den = functools.reduce(jnp.add, w)
            o_ref[q_rows, :] = (num / den * _silu(g_ref[q_rows, :])).astype(BF16)

    for p in reversed(range(n_p)):
        for ti in range(ATTN_SB // ATTN_QB):
            tile(p, DILATED_PATTERNS[p][1], ti)


def _alibi_coefs():
    slopes = 2.0 ** (-8.0 * np.arange(1, N_HEADS + 1, dtype=np.float64) / N_HEADS)
    dil = np.array([d for _, d in DILATED_PATTERNS], dtype=np.float64)
    coef = slopes.reshape(-1, HEADS_PER_BLOCK)[:, None, :] * dil[None, :, None] * LOG2E
    return jnp.asarray(coef.reshape(N_HEADS // HEADS_PER_BLOCK, -1), dtype=F32)


def _attention(proj):
    n_hb = ATTN_WIDTH // LANES
    n_p = len(DILATED_PATTERNS)
    return pl.pallas_call(
        _attn_kernel,
        grid=(n_hb, SEQ // ATTN_SB),
        in_specs=[
            pl.BlockSpec((None, ATTN_SB, LANES), lambda h, s: (h, s, 0)),
            pl.BlockSpec((None, SEQ, LANES), lambda h, s: (n_hb + h, 0, 0)),
            pl.BlockSpec((None, SEQ, LANES), lambda h, s: (2 * n_hb + h, 0, 0)),
            pl.BlockSpec((None, ATTN_SB, LANES), lambda h, s: (3 * n_hb + h, s, 0)),
            pl.BlockSpec(memory_space=pltpu.SMEM),
        ],
        out_specs=pl.BlockSpec((ATTN_SB, LANES), lambda h, s: (s, h)),
        out_shape=jax.ShapeDtypeStruct((SEQ, ATTN_WIDTH), BF16),
        scratch_shapes=[
            pltpu.VMEM((n_p, ATTN_QB + SEQ, LANES), BF16),
            pltpu.VMEM((n_p, ATTN_QB + SEQ, LANES), BF16),
            pltpu.VMEM((SEQ, LANES), F32),
            pltpu.VMEM((n_p, 2, HEADS_PER_BLOCK * ATTN_QB, 2 * ATTN_QB), F32),
            pltpu.VMEM((2 * n_p, ATTN_SB + STATE_PAD, LANES), F32),
        ],
        compiler_params=pltpu.CompilerParams(
            dimension_semantics=("arbitrary", "arbitrary"),
            vmem_limit_bytes=ATTN_VMEM_LIMIT_BYTES),
        name="dilated_attn",
    )(proj, proj, proj, proj, _alibi_coefs())


def _lru_kernel(u_ref, g_ref, cw_ref, cb_ref, wg_ref, bg_ref, lam_ref, wout_ref,
                o_ref, wout_bf16_ref, ubuf, a_s, b_s, h_s, p_s):
    wout_bf16_ref[...] = wout_ref[...].astype(BF16)

    lam = lam_ref[...]
    log_sig = jnp.minimum(lam, 0.0) - jnp.log1p(jnp.exp(-jnp.abs(lam)))
    half_c_log_sig = (0.5 * LRU_C) * log_sig
    chunks_per_seg = LRU_SEG_LEN // LRU_CHUNK

    ubuf[pl.ds(0, SUBLANES), :] = jnp.zeros((SUBLANES, LANES), F32)

    def chunk(c, carry):
        rows = pl.ds(pl.multiple_of(c * LRU_CHUNK, LRU_CHUNK), LRU_CHUNK)
        ubuf[pl.ds(SUBLANES, LRU_CHUNK), :] = u_ref[rows, :]
        xc = cb_ref[...]
        for j in range(CONV_WIDTH):
            off = SUBLANES - (CONV_WIDTH - 1) + j
            xc = xc + ubuf[pl.ds(off, LRU_CHUNK), :] * cw_ref[pl.ds(j, 1), :]
        ubuf[pl.ds(0, SUBLANES), :] = ubuf[pl.ds(LRU_CHUNK, SUBLANES), :]

        z = jnp.dot(xc.astype(BF16), wg_ref[...], preferred_element_type=F32) + bg_ref[...]
        two_i = jnp.tanh(z[:, LANES:]) + 1.0
        log_a = half_c_log_sig * jnp.tanh(z[:, :LANES]) + half_c_log_sig
        a = jnp.exp(log_a)
        th = jnp.tanh(log_a)
        x4 = -0.5 * th / (1.0 - th)
        half_mult = jnp.where(x4 > 0.0, x4 * lax.rsqrt(x4), 0.0)
        b = (half_mult * xc) * two_i

        seg = lax.shift_right_logical(c, chunks_per_seg.bit_length() - 1)
        within = jnp.bitwise_and(c, chunks_per_seg - 1)
        dst = pl.ds(pl.multiple_of(seg * LRU_SEG_PITCH + within * LRU_CHUNK, SUBLANES), LRU_CHUNK)
        a_s[dst, :] = a
        b_s[dst, :] = b
        return carry

    lax.fori_loop(0, SEQ // LRU_CHUNK, chunk, 0, unroll=True)

    def step(t, carry):
        h, prod = carry
        rows = pl.ds(t, LRU_SEGMENTS, stride=LRU_SEG_PITCH)
        a = a_s[rows, :]
        h = a * h + b_s[rows, :]
        prod = prod * a
        h_s[rows, :] = h
        p_s[rows, :] = prod
        return h, prod

    h_end, p_end = lax.fori_loop(
        0, LRU_SEG_LEN, step,
        (jnp.zeros((LRU_SEGMENTS, LANES), F32), jnp.ones((LRU_SEGMENTS, LANES), F32)), unroll=128)

    seg_id = lax.broadcasted_iota(jnp.int32, (LRU_SEGMENTS, LANES), 0)
    c_in = jnp.zeros((LRU_SEGMENTS, LANES), F32)
    for _ in range(LRU_SEGMENTS - 1):
        c_in = jnp.where(seg_id == 0, 0.0, pltpu.roll(h_end + p_end * c_in, 1, 0))

    for s in range(LRU_SEGMENTS):
        c_s = c_in[s:s + 1, :]

        def gate(k, carry, s=s, c_s=c_s):
            off = pl.multiple_of(k * LRU_CHUNK, LRU_CHUNK)
            src = pl.ds(s * LRU_SEG_PITCH + off, LRU_CHUNK)
            rows = pl.ds(s * LRU_SEG_LEN + off, LRU_CHUNK)
            h = h_s[src, :] + p_s[src, :] * c_s
            o_ref[rows, :] = (h * _silu(g_ref[rows, :])).astype(BF16)
            return carry

        lax.fori_loop(0, chunks_per_seg, gate, 0, unroll=True)


def _lru(proj, conv_w, conv_b, w_gates, b_gates, lam, w_out):
    n_cb = LRU_WIDTH // LANES
    u_col0 = 4 * ATTN_WIDTH // LANES
    g_col0 = u_col0 + n_cb
    return pl.pallas_call(
        _lru_kernel,
        grid=(n_cb,),
        in_specs=[
            pl.BlockSpec((None, SEQ, LANES), lambda j: (u_col0 + j, 0, 0)),
            pl.BlockSpec((None, SEQ, LANES), lambda j: (g_col0 + j, 0, 0)),
            pl.BlockSpec((CONV_WIDTH, LANES), lambda j: (0, j)),
            pl.BlockSpec((1, LANES), lambda j: (0, j)),
            pl.BlockSpec((None, LANES, 2 * LANES), lambda j: (j, 0, 0)),
            pl.BlockSpec((None, 1, 2 * LANES), lambda j: (j, 0, 0)),
            pl.BlockSpec((1, LANES), lambda j: (0, j)),
            pl.BlockSpec((D_MODEL // n_cb, D_MODEL), lambda j: (j, 0)),
        ],
        out_specs=[
            pl.BlockSpec((SEQ, LANES), lambda j: (0, j)),
            pl.BlockSpec((D_MODEL // n_cb, D_MODEL), lambda j: (j, 0)),
        ],
        out_shape=[
            jax.ShapeDtypeStruct((SEQ, LRU_WIDTH), BF16),
            jax.ShapeDtypeStruct((D_MODEL, D_MODEL), BF16),
        ],
        scratch_shapes=[
            pltpu.VMEM((LRU_CHUNK + SUBLANES, LANES), F32),
        ] + [pltpu.VMEM((LRU_SEGMENTS * LRU_SEG_PITCH, LANES), F32)] * 4,
        compiler_params=pltpu.CompilerParams(
            dimension_semantics=("arbitrary",), vmem_limit_bytes=VMEM_LIMIT_BYTES),
        name="rglru",
    )(proj, proj, conv_w, conv_b, w_gates, b_gates, lam, w_out)


def _block_diag_gates(w_rgate, b_rgate, w_igate, b_igate):
    per = LANES // LRU_BLOCK_W
    n_cb = LRU_WIDTH // LANES

    def bd(w):
        w = w.reshape(n_cb, per, LRU_BLOCK_W, LRU_BLOCK_W)
        eye = jnp.eye(per, dtype=w.dtype)
        return jnp.einsum('cpkj,pq->cpkqj', w, eye).reshape(n_cb, LANES, LANES)

    w = (0.5 * jnp.concatenate([bd(w_rgate), bd(w_igate)], axis=-1)).astype(BF16)
    b = 0.5 * jnp.concatenate(
        [b_rgate.reshape(n_cb, 1, LANES), b_igate.reshape(n_cb, 1, LANES)], axis=-1)
    return w, b


def _out_kernel(ma_ref, ml_ref, w_ref, x_ref, gate_ref, fg_ref, o_ref):
    mixed = jnp.concatenate([ma_ref[...], ml_ref[...]], axis=1)
    mix = jnp.dot(mixed, w_ref[...], preferred_element_type=F32)
    y = x_ref[...] + gate_ref[...] * mix
    var = jnp.mean(y * y, axis=-1, keepdims=True)
    o_ref[...] = y * lax.rsqrt(var + EPS) * fg_ref[...]


def _out_proj(mixed_attn, mixed_lru, w_out_bf16, x2d, mod, final_gain):
    return pl.pallas_call(
        _out_kernel,
        grid=(SEQ // OUT_TM,),
        in_specs=[
            pl.BlockSpec((OUT_TM, ATTN_WIDTH), lambda i: (i, 0)),
            pl.BlockSpec((OUT_TM, LRU_WIDTH), lambda i: (i, 0)),
            pl.BlockSpec((D_MODEL, D_MODEL), lambda i: (0, 0)),
            pl.BlockSpec((OUT_TM, D_MODEL), lambda i: (i, 0)),
            pl.BlockSpec((1, D_MODEL), lambda i: (0, MOD_GATE)),
            pl.BlockSpec((1, D_MODEL), lambda i: (0, 0)),
        ],
        out_specs=pl.BlockSpec((OUT_TM, D_MODEL), lambda i: (i, 0)),
        out_shape=jax.ShapeDtypeStruct((SEQ, D_MODEL), F32),
        compiler_params=pltpu.CompilerParams(
            dimension_semantics=("arbitrary",), vmem_limit_bytes=VMEM_LIMIT_BYTES),
        name="out_proj",
    )(mixed_attn, mixed_lru, w_out_bf16, x2d, mod, final_gain)


@jax.jit
def kernel(x, c, norm_gain, w_ada, b_ada, w_in, conv_w, conv_b, w_rgate, b_rgate,
           w_igate, b_igate, lru_lambda, w_out, final_gain):
    assert x.shape == (1, SEQ, D_MODEL) and norm_gain.shape[0] == 1
    x2d = x.reshape(SEQ, D_MODEL)
    mod = _ada_mod(c.reshape(D_MODEL, 1), w_ada[0], b_ada)
    proj = _in_proj(x2d, norm_gain, mod, w_in[0])
    mixed_attn = _attention(proj)
    w_gates, b_gates = _block_diag_gates(w_rgate[0], b_rgate[0], w_igate[0], b_igate[0])
    mixed_lru, w_out_bf16 = _lru(proj, conv_w[0], conv_b, w_gates, b_gates, lru_lambda, w_out[0])
    y = _out_proj(mixed_attn, mixed_lru, w_out_bf16, x2d, mod,
                  final_gain.reshape(1, D_MODEL))
    return y.reshape(1, SEQ, D_MODEL)
```

```python
import functools
import math

import jax
import jax.numpy as jnp
import numpy as np
from jax import lax
from jax.experimental import pallas as pl
from jax.experimental.pallas import tpu as pltpu

D_MODEL = 2048
SEQ = 8192
ATTN_WIDTH = D_MODEL // 2
LRU_WIDTH = D_MODEL - ATTN_WIDTH
HEAD_DIM = 64
N_HEADS = ATTN_WIDTH // HEAD_DIM
LRU_BLOCKS = 16
LRU_BLOCK_W = LRU_WIDTH // LRU_BLOCKS
CONV_WIDTH = 4
LRU_C = 8.0
DILATED_PATTERNS = ((128, 1), (512, 4), (2048, 16))
ATTN_SCALE = 1.0 / math.sqrt(HEAD_DIM)
NEG_INF = -1e30
LOG2E = math.log2(math.e)
EPS = 1e-6
PROJ_WIDTH = 4 * ATTN_WIDTH + 2 * LRU_WIDTH
MOD_SHIFT, MOD_SCALE, MOD_GATE = 0, 1, 2

LANES = 128
SUBLANES = 8
VMEM_LIMIT_BYTES = 56 * 1024 * 1024

ADA_TK = 256
PROJ_TM = 256
PROJ_W_CHUNK = 512
PROJ_W_SLOTS = 2
PROJ_VMEM_LIMIT_BYTES = 60 * 1024 * 1024
NORM_ROWS = 64
ATTN_QB = 128
ATTN_SB = 2048
DEINT_ROWS = 256
DEINT_RATIO = 4
ATTN_VMEM_LIMIT_BYTES = 60 * 1024 * 1024
HEADS_PER_BLOCK = LANES // HEAD_DIM
STATE_PAD = SUBLANES
LRU_CHUNK = 1024
LRU_SEGMENTS = SUBLANES
LRU_SEG_LEN = SEQ // LRU_SEGMENTS
LRU_SEG_PITCH = LRU_SEG_LEN + SUBLANES
OUT_TM = 512

F32 = jnp.float32
BF16 = jnp.bfloat16


def _silu(x):
    h = 0.5 * x
    return h + h * jnp.tanh(h)


def _ada_kernel(c_ref, w_ref, b_ref, o_ref):
    @pl.when(pl.program_id(0) == 0)
    def _():
        o_ref[...] = b_ref[...]

    c_act = jnp.transpose(_silu(c_ref[...]))
    o_ref[...] += jnp.sum(w_ref[...] * c_act, axis=0, keepdims=True)


def _ada_mod(c_col, w_ada, b_ada):
    n = w_ada.shape[1]
    return pl.pallas_call(
        _ada_kernel,
        grid=(D_MODEL // ADA_TK,),
        in_specs=[
            pl.BlockSpec((1, ADA_TK), lambda k: (0, k)),
            pl.BlockSpec((ADA_TK, n), lambda k: (k, 0)),
            pl.BlockSpec((1, n), lambda k: (0, 0)),
        ],
        out_specs=pl.BlockSpec((1, n), lambda k: (0, 0)),
        out_shape=jax.ShapeDtypeStruct((1, n), F32),
        compiler_params=pltpu.CompilerParams(
            dimension_semantics=("arbitrary",), vmem_limit_bytes=VMEM_LIMIT_BYTES),
        name="ada_mod",
    )(c_col, w_ada, b_ada)


def _w_chunk_copy(w_hbm, stage, sem, c):
    slot = c % PROJ_W_SLOTS
    cols = pl.ds(c * PROJ_W_CHUNK, PROJ_W_CHUNK)
    return pltpu.make_async_copy(w_hbm.at[:, cols], stage.at[slot], sem.at[slot])


def _proj_kernel(x0_ref, xn_ref, gain_ref, shift_ref, scale_ref, w_hbm, o_ref,
                 w_res, stage, h_even, h_odd, sem):
    s = pl.program_id(0)
    n_chunks = PROJ_WIDTH // PROJ_W_CHUNK
    slabs_per_chunk = PROJ_W_CHUNK // LANES

    def normalise(x_ref, h_ref):
        mul = gain_ref[...] * (1.0 + scale_ref[...])
        shift = shift_ref[...]
        for r0 in range(0, PROJ_TM, NORM_ROWS):
            xv = x_ref[pl.ds(r0, NORM_ROWS), :]
            var = jnp.mean(xv * xv, axis=-1, keepdims=True)
            h = xv * lax.rsqrt(var + EPS) * mul + shift
            h_ref[pl.ds(r0, NORM_ROWS), :] = h.astype(BF16)

    def project(h_ref, first_slab, n_slabs):
        cols = pl.ds(first_slab * LANES, n_slabs * LANES)
        acc = jnp.dot(h_ref[...], w_res[:, cols], preferred_element_type=F32)
        for b in range(n_slabs):
            o_ref[first_slab + b] = acc[:, b * LANES:(b + 1) * LANES]

    @pl.when(s == 0)
    def _():
        for c in range(PROJ_W_SLOTS):
            _w_chunk_copy(w_hbm, stage, sem, c).start()
        normalise(x0_ref, h_even)
        normalise(xn_ref, h_odd)
        for c in range(n_chunks):
            _w_chunk_copy(w_hbm, stage, sem, c).wait()
            cols = pl.ds(c * PROJ_W_CHUNK, PROJ_W_CHUNK)
            w_res[:, cols] = stage[c % PROJ_W_SLOTS].astype(BF16)
            if c + PROJ_W_SLOTS < n_chunks:
                _w_chunk_copy(w_hbm, stage, sem, c + PROJ_W_SLOTS).start()
            project(h_even, c * slabs_per_chunk, slabs_per_chunk)

    is_even = jnp.bitwise_and(s, 1) == 0

    @pl.when(jnp.logical_and(s > 0, is_even))
    def _():
        normalise(xn_ref, h_odd)
        project(h_even, 0, PROJ_WIDTH // LANES)

    @pl.when(jnp.logical_not(is_even))
    def _():
        normalise(xn_ref, h_even)
        project(h_odd, 0, PROJ_WIDTH // LANES)


def _in_proj(x2d, gain, mod, w_in):
    n_tiles = SEQ // PROJ_TM
    return pl.pallas_call(
        _proj_kernel,
        grid=(n_tiles,),
        in_specs=[
            pl.BlockSpec((PROJ_TM, D_MODEL), lambda s: (0, 0), pipeline_mode=pl.Buffered(1)),
            pl.BlockSpec((PROJ_TM, D_MODEL), lambda s: (jnp.minimum(s + 1, n_tiles - 1), 0)),
            pl.BlockSpec((1, D_MODEL), lambda s: (0, 0)),
            pl.BlockSpec((1, D_MODEL), lambda s: (0, MOD_SHIFT)),
            pl.BlockSpec((1, D_MODEL), lambda s: (0, MOD_SCALE)),
            pl.BlockSpec(memory_space=pl.ANY),
        ],
        out_specs=pl.BlockSpec((PROJ_WIDTH // LANES, PROJ_TM, LANES), lambda s: (0, s, 0)),
        out_shape=jax.ShapeDtypeStruct((PROJ_WIDTH // LANES, SEQ, LANES), F32),
        scratch_shapes=[
            pltpu.VMEM((D_MODEL, PROJ_WIDTH), BF16),
            pltpu.VMEM((PROJ_W_SLOTS, D_MODEL, PROJ_W_CHUNK), F32),
            pltpu.VMEM((PROJ_TM, D_MODEL), BF16),
            pltpu.VMEM((PROJ_TM, D_MODEL), BF16),
            pltpu.SemaphoreType.DMA((PROJ_W_SLOTS,)),
        ],
        compiler_params=pltpu.CompilerParams(
            dimension_semantics=("arbitrary",), vmem_limit_bytes=PROJ_VMEM_LIMIT_BYTES),
        name="in_proj",
    )(x2d, x2d, gain, mod, mod, w_in)


def _rows(start, size, stride):
    return pl.ds(start, size) if stride == 1 else pl.ds(start, size, stride=stride)


def _attn_kernel(q_ref, k_ref, v_ref, g_ref, coef_ref, o_ref,
                 kd, vd, tmp, bias_s, state):
    sb = pl.program_id(1)
    n_p = len(DILATED_PATTERNS)
    assert DILATED_PATTERNS[0][1] == 1
    lse_s = [None] + [state.at[p] for p in range(1, n_p)]
    out_s = [None] + [state.at[n_p + p] for p in range(1, n_p)]
    lane = lax.broadcasted_iota(jnp.int32, (ATTN_QB, LANES), 1)
    is_h0 = lane < HEAD_DIM

    @pl.when(sb == 0)
    def _():
        for src_ref, dst in ((k_ref, kd), (v_ref, vd)):
            for p, (_, d) in enumerate(DILATED_PATTERNS):
                dst[p, pl.ds(0, ATTN_QB), :] = jnp.zeros((ATTN_QB, LANES), BF16)
                d_prev = DILATED_PATTERNS[p - 1][1] if p else 1
                ratio = d // d_prev
                assert p == 0 and d == 1 or ratio == DEINT_RATIO
                per_residue = SEQ // d // DEINT_ROWS
                keep_f32 = 0 < p < len(DILATED_PATTERNS) - 1
                from_ref = src_ref if p <= 1 else tmp

                def deint(c, carry, p=p, d_prev=d_prev, ratio=ratio, per_residue=per_residue,
                          keep_f32=keep_f32, from_ref=from_ref, dst=dst):
                    r = lax.shift_right_logical(c, per_residue.bit_length() - 1)
                    chunk = jnp.bitwise_and(c, per_residue - 1)
                    r_prev = jnp.bitwise_and(r, d_prev - 1)
                    j = lax.shift_right_logical(r, d_prev.bit_length() - 1)
                    start = r_prev * (SEQ // d_prev) + chunk * (DEINT_ROWS * ratio) + j
                    x = from_ref[_rows(start, DEINT_ROWS, ratio), :]
                    if keep_f32:
                        tmp[pl.ds(pl.multiple_of(c * DEINT_ROWS, DEINT_ROWS), DEINT_ROWS), :] = x
                    rows = pl.ds(pl.multiple_of(ATTN_QB + c * DEINT_ROWS, ATTN_QB), DEINT_ROWS)
                    dst[p, rows, :] = x.astype(BF16)
                    return carry

                lax.fori_loop(0, SEQ // DEINT_ROWS, deint, 0, unroll=4)

        qi = lax.broadcasted_iota(jnp.int32, (ATTN_QB, 2 * ATTN_QB), 0)
        ki = lax.broadcasted_iota(jnp.int32, (ATTN_QB, 2 * ATTN_QB), 1)
        dist = qi + ATTN_QB - ki
        in_band = jnp.logical_and(dist >= 0, dist <= ATTN_QB)
        in_band_cur = jnp.logical_and(in_band, ki >= ATTN_QB)
        dist_f = dist.astype(F32)
        for p in range(len(DILATED_PATTERNS)):
            for h in range(HEADS_PER_BLOCK):
                pen = -(coef_ref[pl.program_id(0), p * HEADS_PER_BLOCK + h] * dist_f)
                rows = pl.ds(h * ATTN_QB, ATTN_QB)
                bias_s[p, 0, rows, :] = jnp.where(in_band, pen, NEG_INF)
                bias_s[p, 1, rows, :] = jnp.where(in_band_cur, pen, NEG_INF)

    ones = jnp.ones((2 * ATTN_QB, LANES), BF16)

    def place(d, ti):
        r, n = ti % d, ti // d
        blk = sb * (ATTN_SB // (ATTN_QB * d)) + n
        q_rows = _rows(r + n * (ATTN_QB * d), ATTN_QB, d)
        window = pl.ds(pl.multiple_of(r * (SEQ // d) + blk * ATTN_QB, ATTN_QB), 2 * ATTN_QB)
        return q_rows, window, blk == 0

    def tile(p, d, ti):
        q_rows, window, first = place(d, ti)
        q = (q_ref[q_rows, :] * (ATTN_SCALE * LOG2E)).astype(BF16)
        zero = jnp.zeros_like(q)
        qs = jnp.concatenate([jnp.where(is_h0, q, zero), jnp.where(is_h0, zero, q)], axis=0)
        s = lax.dot_general(qs, kd[p, window, :], (((1,), (1,)), ((), ())),
                            preferred_element_type=F32)
        s = s + bias_s[p, first.astype(jnp.int32)]
        mx = jnp.broadcast_to(jnp.max(s, axis=1, keepdims=True), (2 * ATTN_QB, LANES))
        e = jnp.concatenate(
            [jnp.exp2(s[:, :LANES] - mx), jnp.exp2(s[:, LANES:] - mx)], axis=1).astype(BF16)
        vc = jnp.concatenate([vd[p, window, :], ones], axis=1)
        pv = jnp.dot(e, vc, preferred_element_type=F32)
        acc = jnp.where(is_h0, pv[:ATTN_QB, :LANES], pv[ATTN_QB:, :LANES])
        l = jnp.where(is_h0, pv[:ATTN_QB, LANES:], pv[ATTN_QB:, LANES:])
        m = jnp.where(is_h0, mx[:ATTN_QB], mx[ATTN_QB:])
        out_t = acc / l
        lse_t = m + jnp.log2(l)
        if p > 0:
            out_s[p][q_rows, :] = out_t
            lse_s[p][q_rows, :] = lse_t
        else:
            lse = [lse_t] + [lse_s[j][q_rows, :] for j in range(1, n_p)]
            outs = [out_t] + [out_s[j][q_rows, :] for j in range(1, n_p)]
            top = functools.reduce(jnp.maximum, lse)
            w = [jnp.exp2(x - top) for x in lse]
            num = functools.reduce(jnp.add, [w[j] * outs[j] for j in range(n_p)])
            den = functools.reduce(jnp.add, w)
            o_ref[q_rows, :] = (num / den * _silu(g_ref[q_rows, :])).astype(BF16)

    for p in reversed(range(n_p)):
        for ti in range(ATTN_SB // ATTN_QB):
            tile(p, DILATED_PATTERNS[p][1], ti)


def _alibi_coefs():
    slopes = 2.0 ** (-8.0 * np.arange(1, N_HEADS + 1, dtype=np.float64) / N_HEADS)
    dil = np.array([d for _, d in DILATED_PATTERNS], dtype=np.float64)
    coef = slopes.reshape(-1, HEADS_PER_BLOCK)[:, None, :] * dil[None, :, None] * LOG2E
    return jnp.asarray(coef.reshape(N_HEADS // HEADS_PER_BLOCK, -1), dtype=F32)


def _attention(proj):
    n_hb = ATTN_WIDTH // LANES
    n_p = len(DILATED_PATTERNS)
    return pl.pallas_call(
        _attn_kernel,
        grid=(n_hb, SEQ // ATTN_SB),
        in_specs=[
            pl.BlockSpec((None, ATTN_SB, LANES), lambda h, s: (h, s, 0)),
            pl.BlockSpec((None, SEQ, LANES), lambda h, s: (n_hb + h, 0, 0)),
            pl.BlockSpec((None, SEQ, LANES), lambda h, s: (2 * n_hb + h, 0, 0)),
            pl.BlockSpec((None, ATTN_SB, LANES), lambda h, s: (3 * n_hb + h, s, 0)),
            pl.BlockSpec(memory_space=pltpu.SMEM),
        ],
        out_specs=pl.BlockSpec((ATTN_SB, LANES), lambda h, s: (s, h)),
        out_shape=jax.ShapeDtypeStruct((SEQ, ATTN_WIDTH), BF16),
        scratch_shapes=[
            pltpu.VMEM((n_p, ATTN_QB + SEQ, LANES), BF16),
            pltpu.VMEM((n_p, ATTN_QB + SEQ, LANES), BF16),
            pltpu.VMEM((SEQ, LANES), F32),
            pltpu.VMEM((n_p, 2, HEADS_PER_BLOCK * ATTN_QB, 2 * ATTN_QB), F32),
            pltpu.VMEM((2 * n_p, ATTN_SB + STATE_PAD, LANES), F32),
        ],
        compiler_params=pltpu.CompilerParams(
            dimension_semantics=("arbitrary", "arbitrary"),
            vmem_limit_bytes=ATTN_VMEM_LIMIT_BYTES),
        name="dilated_attn",
    )(proj, proj, proj, proj, _alibi_coefs())


def _lru_kernel(u_ref, g_ref, cw_ref, cb_ref, wg_ref, bg_ref, lam_ref, wout_ref,
                o_ref, wout_bf16_ref, ubuf, a_s, b_s, h_s, p_s):
    wout_bf16_ref[...] = wout_ref[...].astype(BF16)

    lam = lam_ref[...]
    log_sig = jnp.minimum(lam, 0.0) - jnp.log1p(jnp.exp(-jnp.abs(lam)))
    half_c_log_sig = (0.5 * LRU_C) * log_sig
    chunks_per_seg = LRU_SEG_LEN // LRU_CHUNK

    ubuf[pl.ds(0, SUBLANES), :] = jnp.zeros((SUBLANES, LANES), F32)

    def chunk(c, carry):
        rows = pl.ds(pl.multiple_of(c * LRU_CHUNK, LRU_CHUNK), LRU_CHUNK)
        ubuf[pl.ds(SUBLANES, LRU_CHUNK), :] = u_ref[rows, :]
        xc = cb_ref[...]
        for j in range(CONV_WIDTH):
            off = SUBLANES - (CONV_WIDTH - 1) + j
            xc = xc + ubuf[pl.ds(off, LRU_CHUNK), :] * cw_ref[pl.ds(j, 1), :]
        ubuf[pl.ds(0, SUBLANES), :] = ubuf[pl.ds(LRU_CHUNK, SUBLANES), :]

        z = jnp.dot(xc.astype(BF16), wg_ref[...], preferred_element_type=F32) + bg_ref[...]
        two_i = jnp.tanh(z[:, LANES:]) + 1.0
        log_a = half_c_log_sig * jnp.tanh(z[:, :LANES]) + half_c_log_sig
        a = jnp.exp(log_a)
        th = jnp.tanh(log_a)
        x4 = -0.5 * th / (1.0 - th)
        half_mult = jnp.where(x4 > 0.0, x4 * lax.rsqrt(x4), 0.0)
        b = (half_mult * xc) * two_i

        seg = lax.shift_right_logical(c, chunks_per_seg.bit_length() - 1)
        within = jnp.bitwise_and(c, chunks_per_seg - 1)
        dst = pl.ds(pl.multiple_of(seg * LRU_SEG_PITCH + within * LRU_CHUNK, SUBLANES), LRU_CHUNK)
        a_s[dst, :] = a
        b_s[dst, :] = b
        return carry

    lax.fori_loop(0, SEQ // LRU_CHUNK, chunk, 0, unroll=True)

    def step(t, carry):
        h, prod = carry
        rows = pl.ds(t, LRU_SEGMENTS, stride=LRU_SEG_PITCH)
        a = a_s[rows, :]
        h = a * h + b_s[rows, :]
        prod = prod * a
        h_s[rows, :] = h
        p_s[rows, :] = prod
        return h, prod

    h_end, p_end = lax.fori_loop(
        0, LRU_SEG_LEN, step,
        (jnp.zeros((LRU_SEGMENTS, LANES), F32), jnp.ones((LRU_SEGMENTS, LANES), F32)), unroll=128)

    seg_id = lax.broadcasted_iota(jnp.int32, (LRU_SEGMENTS, LANES), 0)
    c_in = jnp.zeros((LRU_SEGMENTS, LANES), F32)
    for _ in range(LRU_SEGMENTS - 1):
        c_in = jnp.where(seg_id == 0, 0.0, pltpu.roll(h_end + p_end * c_in, 1, 0))

    for s in range(LRU_SEGMENTS):
        c_s = c_in[s:s + 1, :]

        def gate(k, carry, s=s, c_s=c_s):
            off = pl.multiple_of(k * LRU_CHUNK, LRU_CHUNK)
            src = pl.ds(s * LRU_SEG_PITCH + off, LRU_CHUNK)
            rows = pl.ds(s * LRU_SEG_LEN + off, LRU_CHUNK)
            h = h_s[src, :] + p_s[src, :] * c_s
            o_ref[rows, :] = (h * _silu(g_ref[rows, :])).astype(BF16)
            return carry

        lax.fori_loop(0, chunks_per_seg, gate, 0, unroll=True)


def _lru(proj, conv_w, conv_b, w_gates, b_gates, lam, w_out):
    n_cb = LRU_WIDTH // LANES
    u_col0 = 4 * ATTN_WIDTH // LANES
    g_col0 = u_col0 + n_cb
    return pl.pallas_call(
        _lru_kernel,
        grid=(n_cb,),
        in_specs=[
            pl.BlockSpec((None, SEQ, LANES), lambda j: (u_col0 + j, 0, 0)),
            pl.BlockSpec((None, SEQ, LANES), lambda j: (g_col0 + j, 0, 0)),
            pl.BlockSpec((CONV_WIDTH, LANES), lambda j: (0, j)),
            pl.BlockSpec((1, LANES), lambda j: (0, j)),
            pl.BlockSpec((None, LANES, 2 * LANES), lambda j: (j, 0, 0)),
            pl.BlockSpec((None, 1, 2 * LANES), lambda j: (j, 0, 0)),
            pl.BlockSpec((1, LANES), lambda j: (0, j)),
            pl.BlockSpec((D_MODEL // n_cb, D_MODEL), lambda j: (j, 0)),
        ],
        out_specs=[
            pl.BlockSpec((SEQ, LANES), lambda j: (0, j)),
            pl.BlockSpec((D_MODEL // n_cb, D_MODEL), lambda j: (j, 0)),
        ],
        out_shape=[
            jax.ShapeDtypeStruct((SEQ, LRU_WIDTH), BF16),
            jax.ShapeDtypeStruct((D_MODEL, D_MODEL), BF16),
        ],
        scratch_shapes=[
            pltpu.VMEM((LRU_CHUNK + SUBLANES, LANES), F32),
        ] + [pltpu.VMEM((LRU_SEGMENTS * LRU_SEG_PITCH, LANES), F32)] * 4,
        compiler_params=pltpu.CompilerParams(
            dimension_semantics=("arbitrary",), vmem_limit_bytes=VMEM_LIMIT_BYTES),
        name="rglru",
    )(proj, proj, conv_w, conv_b, w_gates, b_gates, lam, w_out)


def _block_diag_gates(w_rgate, b_rgate, w_igate, b_igate):
    per = LANES // LRU_BLOCK_W
    n_cb = LRU_WIDTH // LANES

    def bd(w):
        w = w.reshape(n_cb, per, LRU_BLOCK_W, LRU_BLOCK_W)
        eye = jnp.eye(per, dtype=w.dtype)
        return jnp.einsum('cpkj,pq->cpkqj', w, eye).reshape(n_cb, LANES, LANES)

    w = (0.5 * jnp.concatenate([bd(w_rgate), bd(w_igate)], axis=-1)).astype(BF16)
    b = 0.5 * jnp.concatenate(
        [b_rgate.reshape(n_cb, 1, LANES), b_igate.reshape(n_cb, 1, LANES)], axis=-1)
    return w, b


def _out_kernel(ma_ref, ml_ref, w_ref, x_ref, gate_ref, fg_ref, o_ref):
    mixed = jnp.concatenate([ma_ref[...], ml_ref[...]], axis=1)
    mix = jnp.dot(mixed, w_ref[...], preferred_element_type=F32)
    y = x_ref[...] + gate_ref[...] * mix
    var = jnp.mean(y * y, axis=-1, keepdims=True)
    o_ref[...] = y * lax.rsqrt(var + EPS) * fg_ref[...]


def _out_proj(mixed_attn, mixed_lru, w_out_bf16, x2d, mod, final_gain):
    return pl.pallas_call(
        _out_kernel,
        grid=(SEQ // OUT_TM,),
        in_specs=[
            pl.BlockSpec((OUT_TM, ATTN_WIDTH), lambda i: (i, 0)),
            pl.BlockSpec((OUT_TM, LRU_WIDTH), lambda i: (i, 0)),
            pl.BlockSpec((D_MODEL, D_MODEL), lambda i: (0, 0)),
            pl.BlockSpec((OUT_TM, D_MODEL), lambda i: (i, 0)),
            pl.BlockSpec((1, D_MODEL), lambda i: (0, MOD_GATE)),
            pl.BlockSpec((1, D_MODEL), lambda i: (0, 0)),
        ],
        out_specs=pl.BlockSpec((OUT_TM, D_MODEL), lambda i: (i, 0)),
        out_shape=jax.ShapeDtypeStruct((SEQ, D_MODEL), F32),
        compiler_params=pltpu.CompilerParams(
            dimension_semantics=("arbitrary",), vmem_limit_bytes=VMEM_LIMIT_BYTES),
        name="out_proj",
    )(mixed_attn, mixed_lru, w_out_bf16, x2d, mod, final_gain)


@jax.jit
def kernel(x, c, norm_gain, w_ada, b_ada, w_in, conv_w, conv_b, w_rgate, b_rgate,
           w_igate, b_igate, lru_lambda, w_out, final_gain):
    assert x.shape == (1, SEQ, D_MODEL) and norm_gain.shape[0] == 1
    x2d = x.reshape(SEQ, D_MODEL)
    mod = _ada_mod(c, w_ada[0], b_ada)
    proj = _in_proj(x2d, norm_gain, mod, w_in[0])
    mixed_attn = _attention(proj)
    w_gates, b_gates = _block_diag_gates(w_rgate[0], b_rgate[0], w_igate[0], b_igate[0])
    mixed_lru, w_out_bf16 = _lru(proj, conv_w[0], conv_b, w_gates, b_gates, lru_lambda, w_out[0])
    y = _out_proj(mixed_attn, mixed_lru, w_out_bf16, x2d, mod,
                  final_gain.reshape(1, D_MODEL))
    return y.reshape(1, SEQ, D_MODEL)
```

```python
import functools
import math

import jax
import jax.numpy as jnp
import numpy as np
from jax import lax
from jax.experimental import pallas as pl
from jax.experimental.pallas import tpu as pltpu

D_MODEL = 2048
SEQ = 8192
ATTN_WIDTH = D_MODEL // 2
LRU_WIDTH = D_MODEL - ATTN_WIDTH
HEAD_DIM = 64
N_HEADS = ATTN_WIDTH // HEAD_DIM
LRU_BLOCKS = 16
LRU_BLOCK_W = LRU_WIDTH // LRU_BLOCKS
CONV_WIDTH = 4
LRU_C = 8.0
DILATED_PATTERNS = ((128, 1), (512, 4), (2048, 16))
ATTN_SCALE = 1.0 / math.sqrt(HEAD_DIM)
NEG_INF = -1e30
LOG2E = math.log2(math.e)
EPS = 1e-6
PROJ_WIDTH = 4 * ATTN_WIDTH + 2 * LRU_WIDTH
MOD_SHIFT, MOD_SCALE, MOD_GATE = 0, 1, 2

LANES = 128
SUBLANES = 8
VMEM_LIMIT_BYTES = 56 * 1024 * 1024

ADA_TK = 256
PROJ_TM = 256
PROJ_W_CHUNK = 512
PROJ_W_SLOTS = 2
PROJ_VMEM_LIMIT_BYTES = 60 * 1024 * 1024
NORM_ROWS = 64
ATTN_QB = 128
ATTN_SB = 2048
DEINT_ROWS = 256
DEINT_RATIO = 4
ATTN_VMEM_LIMIT_BYTES = 60 * 1024 * 1024
HEADS_PER_BLOCK = LANES // HEAD_DIM
STATE_PAD = SUBLANES
LRU_CHUNK = 1024
LRU_SEGMENTS = SUBLANES
LRU_SEG_LEN = SEQ // LRU_SEGMENTS
LRU_SEG_PITCH = LRU_SEG_LEN + SUBLANES
OUT_TM = 512

F32 = jnp.float32
BF16 = jnp.bfloat16


def _silu(x):
    h = 0.5 * x
    return h + h * jnp.tanh(h)


def _ada_kernel(c_ref, w_ref, b_ref, o_ref):
    @pl.when(pl.program_id(0) == 0)
    def _():
        o_ref[...] = b_ref[...]

    c_act = jnp.transpose(_silu(c_ref[...]))
    o_ref[...] += jnp.sum(w_ref[...] * c_act, axis=0, keepdims=True)


def _ada_mod(c_col, w_ada, b_ada):
    n = w_ada.shape[1]
    return pl.pallas_call(
        _ada_kernel,
        grid=(D_MODEL // ADA_TK,),
        in_specs=[
            pl.BlockSpec((1, ADA_TK), lambda k: (0, k)),
            pl.BlockSpec((ADA_TK, n), lambda k: (k, 0)),
            pl.BlockSpec((1, n), lambda k: (0, 0)),
        ],
        out_specs=pl.BlockSpec((1, n), lambda k: (0, 0)),
        out_shape=jax.ShapeDtypeStruct((1, n), F32),
        compiler_params=pltpu.CompilerParams(
            dimension_semantics=("arbitrary",), vmem_limit_bytes=VMEM_LIMIT_BYTES),
        name="ada_mod",
    )(c_col, w_ada, b_ada)


def _w_chunk_copy(w_hbm, stage, sem, c):
    slot = c % PROJ_W_SLOTS
    cols = pl.ds(c * PROJ_W_CHUNK, PROJ_W_CHUNK)
    return pltpu.make_async_copy(w_hbm.at[:, cols], stage.at[slot], sem.at[slot])


def _proj_kernel(x0_ref, xn_ref, gain_ref, shift_ref, scale_ref, w_hbm, o_ref,
                 w_res, stage, h_even, h_odd, sem):
    s = pl.program_id(0)
    n_chunks = PROJ_WIDTH // PROJ_W_CHUNK
    slabs_per_chunk = PROJ_W_CHUNK // LANES

    def normalise(x_ref, h_ref):
        mul = gain_ref[...] * (1.0 + scale_ref[...])
        shift = shift_ref[...]
        for r0 in range(0, PROJ_TM, NORM_ROWS):
            xv = x_ref[pl.ds(r0, NORM_ROWS), :]
            var = jnp.mean(xv * xv, axis=-1, keepdims=True)
            h = xv * lax.rsqrt(var + EPS) * mul + shift
            h_ref[pl.ds(r0, NORM_ROWS), :] = h.astype(BF16)

    def project(h_ref, first_slab, n_slabs):
        cols = pl.ds(first_slab * LANES, n_slabs * LANES)
        acc = jnp.dot(h_ref[...], w_res[:, cols], preferred_element_type=F32)
        for b in range(n_slabs):
            o_ref[first_slab + b] = acc[:, b * LANES:(b + 1) * LANES]

    @pl.when(s == 0)
    def _():
        for c in range(PROJ_W_SLOTS):
            _w_chunk_copy(w_hbm, stage, sem, c).start()
        normalise(x0_ref, h_even)
        normalise(xn_ref, h_odd)
        for c in range(n_chunks):
            _w_chunk_copy(w_hbm, stage, sem, c).wait()
            cols = pl.ds(c * PROJ_W_CHUNK, PROJ_W_CHUNK)
            w_res[:, cols] = stage[c % PROJ_W_SLOTS].astype(BF16)
            if c + PROJ_W_SLOTS < n_chunks:
                _w_chunk_copy(w_hbm, stage, sem, c + PROJ_W_SLOTS).start()
            project(h_even, c * slabs_per_chunk, slabs_per_chunk)

    is_even = jnp.bitwise_and(s, 1) == 0

    @pl.when(jnp.logical_and(s > 0, is_even))
    def _():
        normalise(xn_ref, h_odd)
        project(h_even, 0, PROJ_WIDTH // LANES)

    @pl.when(jnp.logical_not(is_even))
    def _():
        normalise(xn_ref, h_even)
        project(h_odd, 0, PROJ_WIDTH // LANES)


def _in_proj(x2d, gain, mod, w_in):
    n_tiles = SEQ // PROJ_TM
    return pl.pallas_call(
        _proj_kernel,
        grid=(n_tiles,),
        in_specs=[
            pl.BlockSpec((PROJ_TM, D_MODEL), lambda s: (0, 0), pipeline_mode=pl.Buffered(1)),
            pl.BlockSpec((PROJ_TM, D_MODEL), lambda s: (jnp.minimum(s + 1, n_tiles - 1), 0)),
            pl.BlockSpec((1, D_MODEL), lambda s: (0, 0)),
            pl.BlockSpec((1, D_MODEL), lambda s: (0, MOD_SHIFT)),
            pl.BlockSpec((1, D_MODEL), lambda s: (0, MOD_SCALE)),
            pl.BlockSpec(memory_space=pl.ANY),
        ],
        out_specs=pl.BlockSpec((PROJ_WIDTH // LANES, PROJ_TM, LANES), lambda s: (0, s, 0)),
        out_shape=jax.ShapeDtypeStruct((PROJ_WIDTH // LANES, SEQ, LANES), F32),
        scratch_shapes=[
            pltpu.VMEM((D_MODEL, PROJ_WIDTH), BF16),
            pltpu.VMEM((PROJ_W_SLOTS, D_MODEL, PROJ_W_CHUNK), F32),
            pltpu.VMEM((PROJ_TM, D_MODEL), BF16),
            pltpu.VMEM((PROJ_TM, D_MODEL), BF16),
            pltpu.SemaphoreType.DMA((PROJ_W_SLOTS,)),
        ],
        compiler_params=pltpu.CompilerParams(
            dimension_semantics=("arbitrary",), vmem_limit_bytes=PROJ_VMEM_LIMIT_BYTES),
        name="in_proj",
    )(x2d, x2d, gain, mod, mod, w_in)


def _rows(start, size, stride):
    return pl.ds(start, size) if stride == 1 else pl.ds(start, size, stride=stride)


def _attn_kernel(q_ref, k_ref, v_ref, g_ref, coef_ref, o_ref,
                 kd, vd, tmp, bias_s, state):
    sb = pl.program_id(1)
    n_p = len(DILATED_PATTERNS)
    assert DILATED_PATTERNS[0][1] == 1
    lse_s = [None] + [state.at[p] for p in range(1, n_p)]
    out_s = [None] + [state.at[n_p + p] for p in range(1, n_p)]
    lane = lax.broadcasted_iota(jnp.int32, (ATTN_QB, LANES), 1)
    is_h0 = lane < HEAD_DIM

    @pl.when(sb == 0)
    def _():
        for src_ref, dst in ((k_ref, kd), (v_ref, vd)):
            for p, (_, d) in enumerate(DILATED_PATTERNS):
                dst[p, pl.ds(0, ATTN_QB), :] = jnp.zeros((ATTN_QB, LANES), BF16)
                d_prev = DILATED_PATTERNS[p - 1][1] if p else 1
                ratio = d // d_prev
                assert p == 0 and d == 1 or ratio == DEINT_RATIO
                per_residue = SEQ // d // DEINT_ROWS
                keep_f32 = 0 < p < len(DILATED_PATTERNS) - 1
                from_ref = src_ref if p <= 1 else tmp

                def deint(c, carry, p=p, d_prev=d_prev, ratio=ratio, per_residue=per_residue,
                          keep_f32=keep_f32, from_ref=from_ref, dst=dst):
                    r = lax.shift_right_logical(c, per_residue.bit_length() - 1)
                    chunk = jnp.bitwise_and(c, per_residue - 1)
                    r_prev = jnp.bitwise_and(r, d_prev - 1)
                    j = lax.shift_right_logical(r, d_prev.bit_length() - 1)
                    start = r_prev * (SEQ // d_prev) + chunk * (DEINT_ROWS * ratio) + j
                    x = from_ref[_rows(start, DEINT_ROWS, ratio), :]
                    if keep_f32:
                        tmp[pl.ds(pl.multiple_of(c * DEINT_ROWS, DEINT_ROWS), DEINT_ROWS), :] = x
                    rows = pl.ds(pl.multiple_of(ATTN_QB + c * DEINT_ROWS, ATTN_QB), DEINT_ROWS)
                    dst[p, rows, :] = x.astype(BF16)
                    return carry

                lax.fori_loop(0, SEQ // DEINT_ROWS, deint, 0, unroll=4)

        qi = lax.broadcasted_iota(jnp.int32, (ATTN_QB, 2 * ATTN_QB), 0)
        ki = lax.broadcasted_iota(jnp.int32, (ATTN_QB, 2 * ATTN_QB), 1)
        dist = qi + ATTN_QB - ki
        in_band = jnp.logical_and(dist >= 0, dist <= ATTN_QB)
        in_band_cur = jnp.logical_and(in_band, ki >= ATTN_QB)
        dist_f = dist.astype(F32)
        for p in range(len(DILATED_PATTERNS)):
            for h in range(HEADS_PER_BLOCK):
                pen = -(coef_ref[pl.program_id(0), p * HEADS_PER_BLOCK + h] * dist_f)
                rows = pl.ds(h * ATTN_QB, ATTN_QB)
                bias_s[p, 0, rows, :] = jnp.where(in_band, pen, NEG_INF)
                bias_s[p, 1, rows, :] = jnp.where(in_band_cur, pen, NEG_INF)

    ones = jnp.ones((2 * ATTN_QB, LANES), BF16)

    def place(d, ti):
        r, n = ti % d, ti // d
        blk = sb * (ATTN_SB // (ATTN_QB * d)) + n
        q_rows = _rows(r + n * (ATTN_QB * d), ATTN_QB, d)
        window = pl.ds(pl.multiple_of(r * (SEQ // d) + blk * ATTN_QB, ATTN_QB), 2 * ATTN_QB)
        return q_rows, window, blk == 0

    def tile(p, d, ti):
        q_rows, window, first = place(d, ti)
        q = (q_ref[q_rows, :] * (ATTN_SCALE * LOG2E)).astype(BF16)
        zero = jnp.zeros_like(q)
        qs = jnp.concatenate([jnp.where(is_h0, q, zero), jnp.where(is_h0, zero, q)], axis=0)
        s = lax.dot_general(qs, kd[p, window, :], (((1,), (1,)), ((), ())),
                            preferred_element_type=F32)
        s = s + bias_s[p, first.astype(jnp.int32)]
        mx = jnp.broadcast_to(jnp.max(s, axis=1, keepdims=True), (2 * ATTN_QB, LANES))
        e = jnp.concatenate(
            [jnp.exp2(s[:, :LANES] - mx), jnp.exp2(s[:, LANES:] - mx)], axis=1).astype(BF16)
        vc = jnp.concatenate([vd[p, window, :], ones], axis=1)
        pv = jnp.dot(e, vc, preferred_element_type=F32)
        acc = jnp.where(is_h0, pv[:ATTN_QB, :LANES], pv[ATTN_QB:, :LANES])
        l = jnp.where(is_h0, pv[:ATTN_QB, LANES:], pv[ATTN_QB:, LANES:])
        m = jnp.where(is_h0, mx[:ATTN_QB], mx[ATTN_QB:])
        out_t = acc / l
        lse_t = m + jnp.log2(l)
        if p > 0:
            out_s[p][q_rows, :] = out_t
            lse_s[p][q_rows, :] = lse_t
        else:
            lse = [lse_t] + [lse_s[j][q_rows, :] for j in range(1, n_p)]
            outs = [out_t] + [out_s[j][q_rows, :] for j in range(1, n_p)]
            top = functools.reduce(jnp.maximum, lse)
            w = [jnp.exp2(x - top) for x in lse]
            num = functools.reduce(jnp.add, [w[j] * outs[j] for j in range(n_p)])
            den = functools.reduce(jnp.add, w)
            o_ref[q_rows, :] = (num / den * _silu(g_ref[q_rows, :])).astype(BF16)

    for p in reversed(range(n_p)):
        for ti in range(ATTN_SB // ATTN_QB):
            tile(p, DILATED_PATTERNS[p][1], ti)


def _alibi_coefs():
    slopes = 2.0 ** (-8.0 * np.arange(1, N_HEADS + 1, dtype=np.float64) / N_HEADS)
    dil = np.array([d for _, d in DILATED_PATTERNS], dtype=np.float64)
    coef = slopes.reshape(-1, HEADS_PER_BLOCK)[:, None, :] * dil[None, :, None] * LOG2E
    return jnp.asarray(coef.reshape(N_HEADS // HEADS_PER_BLOCK, -1), dtype=F32)


def _attention(proj):
    n_hb = ATTN_WIDTH // LANES
    n_p = len(DILATED_PATTERNS)
    return pl.pallas_call(
        _attn_kernel,
        grid=(n_hb, SEQ // ATTN_SB),
        in_specs=[
            pl.BlockSpec((None, ATTN_SB, LANES), lambda h, s: (h, s, 0)),
            pl.BlockSpec((None, SEQ, LANES), lambda h, s: (n_hb + h, 0, 0)),
            pl.BlockSpec((None, SEQ, LANES), lambda h, s: (2 * n_hb + h, 0, 0)),
            pl.BlockSpec((None, ATTN_SB, LANES), lambda h, s: (3 * n_hb + h, s, 0)),
            pl.BlockSpec(memory_space=pltpu.SMEM),
        ],
        out_specs=pl.BlockSpec((ATTN_SB, LANES), lambda h, s: (s, h)),
        out_shape=jax.ShapeDtypeStruct((SEQ, ATTN_WIDTH), BF16),
        scratch_shapes=[
            pltpu.VMEM((n_p, ATTN_QB + SEQ, LANES), BF16),
            pltpu.VMEM((n_p, ATTN_QB + SEQ, LANES), BF16),
            pltpu.VMEM((SEQ, LANES), F32),
            pltpu.VMEM((n_p, 2, HEADS_PER_BLOCK * ATTN_QB, 2 * ATTN_QB), F32),
            pltpu.VMEM((2 * n_p, ATTN_SB + STATE_PAD, LANES), F32),
        ],
        compiler_params=pltpu.CompilerParams(
            dimension_semantics=("arbitrary", "arbitrary"),
            vmem_limit_bytes=ATTN_VMEM_LIMIT_BYTES),
        name="dilated_attn",
    )(proj, proj, proj, proj, _alibi_coefs())


def _lru_kernel(u_ref, g_ref, cw_ref, cb_ref, wr_ref, wi_ref, br_ref, bi_ref, lam_ref, wout_ref,
                o_ref, wout_bf16_ref, ubuf, a_s, b_s, h_s, p_s):
    wout_bf16_ref[...] = wout_ref[...].astype(BF16)

    gi = lax.broadcasted_iota(jnp.int32, (LANES, LANES), 0)
    gj = lax.broadcasted_iota(jnp.int32, (LANES, LANES), 1)
    on_diag = (gi < LRU_BLOCK_W) == (gj < LRU_BLOCK_W)

    def block_diag(w_ref):
        w = w_ref[...]
        return jnp.where(on_diag, jnp.concatenate([w, w], axis=1), 0.0)

    w_gates = (0.5 * jnp.concatenate([block_diag(wr_ref), block_diag(wi_ref)], axis=1)).astype(BF16)
    b_gates = 0.5 * jnp.concatenate([br_ref[...], bi_ref[...]], axis=1)

    lam = lam_ref[...]
    log_sig = jnp.minimum(lam, 0.0) - jnp.log1p(jnp.exp(-jnp.abs(lam)))
    half_c_log_sig = (0.5 * LRU_C) * log_sig
    chunks_per_seg = LRU_SEG_LEN // LRU_CHUNK

    ubuf[pl.ds(0, SUBLANES), :] = jnp.zeros((SUBLANES, LANES), F32)

    def chunk(c, carry):
        rows = pl.ds(pl.multiple_of(c * LRU_CHUNK, LRU_CHUNK), LRU_CHUNK)
        ubuf[pl.ds(SUBLANES, LRU_CHUNK), :] = u_ref[rows, :]
        xc = cb_ref[...]
        for j in range(CONV_WIDTH):
            off = SUBLANES - (CONV_WIDTH - 1) + j
            xc = xc + ubuf[pl.ds(off, LRU_CHUNK), :] * cw_ref[pl.ds(j, 1), :]
        ubuf[pl.ds(0, SUBLANES), :] = ubuf[pl.ds(LRU_CHUNK, SUBLANES), :]

        z = jnp.dot(xc.astype(BF16), w_gates, preferred_element_type=F32) + b_gates
        two_i = jnp.tanh(z[:, LANES:]) + 1.0
        log_a = half_c_log_sig * jnp.tanh(z[:, :LANES]) + half_c_log_sig
        a = jnp.exp(log_a)
        th = jnp.tanh(log_a)
        x4 = -0.5 * th / (1.0 - th)
        half_mult = jnp.where(x4 > 0.0, x4 * lax.rsqrt(x4), 0.0)
        b = (half_mult * xc) * two_i

        seg = lax.shift_right_logical(c, chunks_per_seg.bit_length() - 1)
        within = jnp.bitwise_and(c, chunks_per_seg - 1)
        dst = pl.ds(pl.multiple_of(seg * LRU_SEG_PITCH + within * LRU_CHUNK, SUBLANES), LRU_CHUNK)
        a_s[dst, :] = a
        b_s[dst, :] = b
        return carry

    lax.fori_loop(0, SEQ // LRU_CHUNK, chunk, 0, unroll=True)

    def step(t, carry):
        h, prod = carry
        rows = pl.ds(t, LRU_SEGMENTS, stride=LRU_SEG_PITCH)
        a = a_s[rows, :]
        h = a * h + b_s[rows, :]
        prod = prod * a
        h_s[rows, :] = h
        p_s[rows, :] = prod
        return h, prod

    h_end, p_end = lax.fori_loop(
        0, LRU_SEG_LEN, step,
        (jnp.zeros((LRU_SEGMENTS, LANES), F32), jnp.ones((LRU_SEGMENTS, LANES), F32)), unroll=128)

    seg_id = lax.broadcasted_iota(jnp.int32, (LRU_SEGMENTS, LANES), 0)
    c_in = jnp.zeros((LRU_SEGMENTS, LANES), F32)
    for _ in range(LRU_SEGMENTS - 1):
        c_in = jnp.where(seg_id == 0, 0.0, pltpu.roll(h_end + p_end * c_in, 1, 0))

    for s in range(LRU_SEGMENTS):
        c_s = c_in[s:s + 1, :]

        def gate(k, carry, s=s, c_s=c_s):
            off = pl.multiple_of(k * LRU_CHUNK, LRU_CHUNK)
            src = pl.ds(s * LRU_SEG_PITCH + off, LRU_CHUNK)
            rows = pl.ds(s * LRU_SEG_LEN + off, LRU_CHUNK)
            h = h_s[src, :] + p_s[src, :] * c_s
            o_ref[rows, :] = (h * _silu(g_ref[rows, :])).astype(BF16)
            return carry

        lax.fori_loop(0, chunks_per_seg, gate, 0, unroll=True)


def _lru(proj, conv_w, conv_b, w_rgate, b_rgate, w_igate, b_igate, lam, w_out):
    n_cb = LRU_WIDTH // LANES
    u_col0 = 4 * ATTN_WIDTH // LANES
    g_col0 = u_col0 + n_cb
    return pl.pallas_call(
        _lru_kernel,
        grid=(n_cb,),
        in_specs=[
            pl.BlockSpec((None, SEQ, LANES), lambda j: (u_col0 + j, 0, 0)),
            pl.BlockSpec((None, SEQ, LANES), lambda j: (g_col0 + j, 0, 0)),
            pl.BlockSpec((CONV_WIDTH, LANES), lambda j: (0, j)),
            pl.BlockSpec((1, LANES), lambda j: (0, j)),
            pl.BlockSpec((None, LANES, LRU_BLOCK_W), lambda j: (j, 0, 0)),
            pl.BlockSpec((None, LANES, LRU_BLOCK_W), lambda j: (j, 0, 0)),
            pl.BlockSpec((None, 1, LANES), lambda j: (j, 0, 0)),
            pl.BlockSpec((None, 1, LANES), lambda j: (j, 0, 0)),
            pl.BlockSpec((1, LANES), lambda j: (0, j)),
            pl.BlockSpec((D_MODEL // n_cb, D_MODEL), lambda j: (j, 0)),
        ],
        out_specs=[
            pl.BlockSpec((SEQ, LANES), lambda j: (0, j)),
            pl.BlockSpec((D_MODEL // n_cb, D_MODEL), lambda j: (j, 0)),
        ],
        out_shape=[
            jax.ShapeDtypeStruct((SEQ, LRU_WIDTH), BF16),
            jax.ShapeDtypeStruct((D_MODEL, D_MODEL), BF16),
        ],
        scratch_shapes=[
            pltpu.VMEM((LRU_CHUNK + SUBLANES, LANES), F32),
        ] + [pltpu.VMEM((LRU_SEGMENTS * LRU_SEG_PITCH, LANES), F32)] * 4,
        compiler_params=pltpu.CompilerParams(
            dimension_semantics=("arbitrary",), vmem_limit_bytes=VMEM_LIMIT_BYTES),
        name="rglru",
    )(proj, proj, conv_w, conv_b,
      w_rgate.reshape(n_cb, LANES, LRU_BLOCK_W), w_igate.reshape(n_cb, LANES, LRU_BLOCK_W),
      b_rgate.reshape(n_cb, 1, LANES), b_igate.reshape(n_cb, 1, LANES), lam, w_out)


def _out_kernel(ma_ref, ml_ref, w_ref, x_ref, gate_ref, fg_ref, o_ref):
    mixed = jnp.concatenate([ma_ref[...], ml_ref[...]], axis=1)
    mix = jnp.dot(mixed, w_ref[...], preferred_element_type=F32)
    y = x_ref[...] + gate_ref[...] * mix
    var = jnp.mean(y * y, axis=-1, keepdims=True)
    o_ref[...] = y * lax.rsqrt(var + EPS) * fg_ref[...]


def _out_proj(mixed_attn, mixed_lru, w_out_bf16, x2d, mod, final_gain):
    return pl.pallas_call(
        _out_kernel,
        grid=(SEQ // OUT_TM,),
        in_specs=[
            pl.BlockSpec((OUT_TM, ATTN_WIDTH), lambda i: (i, 0)),
            pl.BlockSpec((OUT_TM, LRU_WIDTH), lambda i: (i, 0)),
            pl.BlockSpec((D_MODEL, D_MODEL), lambda i: (0, 0)),
            pl.BlockSpec((OUT_TM, D_MODEL), lambda i: (i, 0)),
            pl.BlockSpec((1, D_MODEL), lambda i: (0, MOD_GATE)),
            pl.BlockSpec((1, D_MODEL), lambda i: (0, 0)),
        ],
        out_specs=pl.BlockSpec((OUT_TM, D_MODEL), lambda i: (i, 0)),
        out_shape=jax.ShapeDtypeStruct((SEQ, D_MODEL), F32),
        compiler_params=pltpu.CompilerParams(
            dimension_semantics=("arbitrary",), vmem_limit_bytes=VMEM_LIMIT_BYTES),
        name="out_proj",
    )(mixed_attn, mixed_lru, w_out_bf16, x2d, mod, final_gain)


@jax.jit
def kernel(x, c, norm_gain, w_ada, b_ada, w_in, conv_w, conv_b, w_rgate, b_rgate,
           w_igate, b_igate, lru_lambda, w_out, final_gain):
    assert x.shape == (1, SEQ, D_MODEL) and norm_gain.shape[0] == 1
    x2d = x.reshape(SEQ, D_MODEL)
    mod = _ada_mod(c, w_ada[0], b_ada)
    proj = _in_proj(x2d, norm_gain, mod, w_in[0])
    mixed_attn = _attention(proj)
    mixed_lru, w_out_bf16 = _lru(proj, conv_w[0], conv_b, w_rgate[0], b_rgate[0], w_igate[0],
                                 b_igate[0], lru_lambda, w_out[0])
    y = _out_proj(mixed_attn, mixed_lru, w_out_bf16, x2d, mod,
                  final_gain.reshape(1, D_MODEL))
    return y.reshape(1, SEQ, D_MODEL)
```

```python
import functools
import math

import jax
import jax.numpy as jnp
import numpy as np
from jax import lax
from jax.experimental import pallas as pl
from jax.experimental.pallas import tpu as pltpu

D_MODEL = 2048
SEQ = 8192
ATTN_WIDTH = D_MODEL // 2
LRU_WIDTH = D_MODEL - ATTN_WIDTH
HEAD_DIM = 64
N_HEADS = ATTN_WIDTH // HEAD_DIM
LRU_BLOCKS = 16
LRU_BLOCK_W = LRU_WIDTH // LRU_BLOCKS
CONV_WIDTH = 4
LRU_C = 8.0
DILATED_PATTERNS = ((128, 1), (512, 4), (2048, 16))
ATTN_SCALE = 1.0 / math.sqrt(HEAD_DIM)
NEG_INF = -1e30
LOG2E = math.log2(math.e)
EPS = 1e-6
PROJ_WIDTH = 4 * ATTN_WIDTH + 2 * LRU_WIDTH
MOD_SHIFT, MOD_SCALE, MOD_GATE = 0, 1, 2

LANES = 128
SUBLANES = 8
VMEM_LIMIT_BYTES = 56 * 1024 * 1024

ADA_TK = 256
PROJ_TM = 256
PROJ_W_CHUNK = 512
PROJ_W_SLOTS = 2
PROJ_VMEM_LIMIT_BYTES = 60 * 1024 * 1024
NORM_ROWS = 64
ATTN_QB = 128
ATTN_SB = 2048
DEINT_ROWS = 256
DEINT_RATIO = 4
ATTN_VMEM_LIMIT_BYTES = 60 * 1024 * 1024
HEADS_PER_BLOCK = LANES // HEAD_DIM
STATE_PAD = SUBLANES
LRU_CHUNK = 1024
LRU_SEGMENTS = SUBLANES
LRU_SEG_LEN = SEQ // LRU_SEGMENTS
LRU_SEG_PITCH = LRU_SEG_LEN + SUBLANES
OUT_TM = 512

F32 = jnp.float32
BF16 = jnp.bfloat16


def _silu(x):
    h = 0.5 * x
    return h + h * jnp.tanh(h)


def _ada_kernel(c_ref, w_ref, b_ref, o_ref):
    @pl.when(pl.program_id(0) == 0)
    def _():
        o_ref[...] = b_ref[...]

    c_act = jnp.transpose(_silu(c_ref[...]))
    o_ref[...] += jnp.sum(w_ref[...] * c_act, axis=0, keepdims=True)


def _ada_mod(c_col, w_ada, b_ada):
    n = w_ada.shape[1]
    return pl.pallas_call(
        _ada_kernel,
        grid=(D_MODEL // ADA_TK,),
        in_specs=[
            pl.BlockSpec((1, ADA_TK), lambda k: (0, k)),
            pl.BlockSpec((ADA_TK, n), lambda k: (k, 0)),
            pl.BlockSpec((1, n), lambda k: (0, 0)),
        ],
        out_specs=pl.BlockSpec((1, n), lambda k: (0, 0)),
        out_shape=jax.ShapeDtypeStruct((1, n), F32),
        compiler_params=pltpu.CompilerParams(
            dimension_semantics=("arbitrary",), vmem_limit_bytes=VMEM_LIMIT_BYTES),
        name="ada_mod",
    )(c_col, w_ada, b_ada)


def _w_chunk_copy(w_hbm, stage, sem, c):
    slot = c % PROJ_W_SLOTS
    cols = pl.ds(c * PROJ_W_CHUNK, PROJ_W_CHUNK)
    return pltpu.make_async_copy(w_hbm.at[:, cols], stage.at[slot], sem.at[slot])


def _proj_kernel(x0_ref, xn_ref, gain_ref, shift_ref, scale_ref, w_hbm, o_ref,
                 w_res, stage, h_even, h_odd, sem):
    s = pl.program_id(0)
    n_chunks = PROJ_WIDTH // PROJ_W_CHUNK
    slabs_per_chunk = PROJ_W_CHUNK // LANES

    def normalise(x_ref, h_ref):
        mul = gain_ref[...] * (1.0 + scale_ref[...])
        shift = shift_ref[...]
        for r0 in range(0, PROJ_TM, NORM_ROWS):
            xv = x_ref[pl.ds(r0, NORM_ROWS), :]
            var = jnp.mean(xv * xv, axis=-1, keepdims=True)
            h = xv * lax.rsqrt(var + EPS) * mul + shift
            h_ref[pl.ds(r0, NORM_ROWS), :] = h.astype(BF16)

    def project(h_ref, first_slab, n_slabs):
        cols = pl.ds(first_slab * LANES, n_slabs * LANES)
        acc = jnp.dot(h_ref[...], w_res[:, cols], preferred_element_type=F32)
        for b in range(n_slabs):
            o_ref[first_slab + b] = acc[:, b * LANES:(b + 1) * LANES]

    @pl.when(s == 0)
    def _():
        for c in range(PROJ_W_SLOTS):
            _w_chunk_copy(w_hbm, stage, sem, c).start()
        normalise(x0_ref, h_even)
        normalise(xn_ref, h_odd)
        for c in range(n_chunks):
            _w_chunk_copy(w_hbm, stage, sem, c).wait()
            cols = pl.ds(c * PROJ_W_CHUNK, PROJ_W_CHUNK)
            w_res[:, cols] = stage[c % PROJ_W_SLOTS].astype(BF16)
            if c + PROJ_W_SLOTS < n_chunks:
                _w_chunk_copy(w_hbm, stage, sem, c + PROJ_W_SLOTS).start()
            project(h_even, c * slabs_per_chunk, slabs_per_chunk)

    is_even = jnp.bitwise_and(s, 1) == 0

    @pl.when(jnp.logical_and(s > 0, is_even))
    def _():
        normalise(xn_ref, h_odd)
        project(h_even, 0, PROJ_WIDTH // LANES)

    @pl.when(jnp.logical_not(is_even))
    def _():
        normalise(xn_ref, h_even)
        project(h_odd, 0, PROJ_WIDTH // LANES)


def _in_proj(x2d, gain, mod, w_in):
    n_tiles = SEQ // PROJ_TM
    return pl.pallas_call(
        _proj_kernel,
        grid=(n_tiles,),
        in_specs=[
            pl.BlockSpec((PROJ_TM, D_MODEL), lambda s: (0, 0), pipeline_mode=pl.Buffered(1)),
            pl.BlockSpec((PROJ_TM, D_MODEL), lambda s: (jnp.minimum(s + 1, n_tiles - 1), 0)),
            pl.BlockSpec((1, D_MODEL), lambda s: (0, 0)),
            pl.BlockSpec((1, D_MODEL), lambda s: (0, MOD_SHIFT)),
            pl.BlockSpec((1, D_MODEL), lambda s: (0, MOD_SCALE)),
            pl.BlockSpec(memory_space=pl.ANY),
        ],
        out_specs=pl.BlockSpec((PROJ_WIDTH // LANES, PROJ_TM, LANES), lambda s: (0, s, 0)),
        out_shape=jax.ShapeDtypeStruct((PROJ_WIDTH // LANES, SEQ, LANES), F32),
        scratch_shapes=[
            pltpu.VMEM((D_MODEL, PROJ_WIDTH), BF16),
            pltpu.VMEM((PROJ_W_SLOTS, D_MODEL, PROJ_W_CHUNK), F32),
            pltpu.VMEM((PROJ_TM, D_MODEL), BF16),
            pltpu.VMEM((PROJ_TM, D_MODEL), BF16),
            pltpu.SemaphoreType.DMA((PROJ_W_SLOTS,)),
        ],
        compiler_params=pltpu.CompilerParams(
            dimension_semantics=("arbitrary",), vmem_limit_bytes=PROJ_VMEM_LIMIT_BYTES),
        name="in_proj",
    )(x2d, x2d, gain, mod, mod, w_in)


def _rows(start, size, stride):
    return pl.ds(start, size) if stride == 1 else pl.ds(start, size, stride=stride)


def _attn_kernel(q_ref, k_ref, v_ref, g_ref, coef_ref, o_ref,
                 kd, vd, tmp, bias_s, state):
    sb = pl.program_id(1)
    n_p = len(DILATED_PATTERNS)
    assert DILATED_PATTERNS[0][1] == 1
    lse_s = [None] + [state.at[p] for p in range(1, n_p)]
    out_s = [None] + [state.at[n_p + p] for p in range(1, n_p)]
    lane = lax.broadcasted_iota(jnp.int32, (ATTN_QB, LANES), 1)
    is_h0 = lane < HEAD_DIM

    @pl.when(sb == 0)
    def _():
        for src_ref, dst in ((k_ref, kd), (v_ref, vd)):
            for p, (_, d) in enumerate(DILATED_PATTERNS):
                dst[p, pl.ds(0, ATTN_QB), :] = jnp.zeros((ATTN_QB, LANES), BF16)
                d_prev = DILATED_PATTERNS[p - 1][1] if p else 1
                ratio = d // d_prev
                assert p == 0 and d == 1 or ratio == DEINT_RATIO
                per_residue = SEQ // d // DEINT_ROWS
                keep_f32 = 0 < p < len(DILATED_PATTERNS) - 1
                from_ref = src_ref if p <= 1 else tmp

                def deint(c, carry, p=p, d_prev=d_prev, ratio=ratio, per_residue=per_residue,
                          keep_f32=keep_f32, from_ref=from_ref, dst=dst):
                    r = lax.shift_right_logical(c, per_residue.bit_length() - 1)
                    chunk = jnp.bitwise_and(c, per_residue - 1)
                    r_prev = jnp.bitwise_and(r, d_prev - 1)
                    j = lax.shift_right_logical(r, d_prev.bit_length() - 1)
                    start = r_prev * (SEQ // d_prev) + chunk * (DEINT_ROWS * ratio) + j
                    x = from_ref[_rows(start, DEINT_ROWS, ratio), :]
                    if keep_f32:
                        tmp[pl.ds(pl.multiple_of(c * DEINT_ROWS, DEINT_ROWS), DEINT_ROWS), :] = x
                    rows = pl.ds(pl.multiple_of(ATTN_QB + c * DEINT_ROWS, ATTN_QB), DEINT_ROWS)
                    dst[p, rows, :] = x.astype(BF16)
                    return carry

                lax.fori_loop(0, SEQ // DEINT_ROWS, deint, 0, unroll=4)

        qi = lax.broadcasted_iota(jnp.int32, (ATTN_QB, 2 * ATTN_QB), 0)
        ki = lax.broadcasted_iota(jnp.int32, (ATTN_QB, 2 * ATTN_QB), 1)
        dist = qi + ATTN_QB - ki
        in_band = jnp.logical_and(dist >= 0, dist <= ATTN_QB)
        in_band_cur = jnp.logical_and(in_band, ki >= ATTN_QB)
        dist_f = dist.astype(F32)
        for p in range(len(DILATED_PATTERNS)):
            for h in range(HEADS_PER_BLOCK):
                pen = -(coef_ref[pl.program_id(0), p * HEADS_PER_BLOCK + h] * dist_f)
                rows = pl.ds(h * ATTN_QB, ATTN_QB)
                bias_s[p, 0, rows, :] = jnp.where(in_band, pen, NEG_INF)
                bias_s[p, 1, rows, :] = jnp.where(in_band_cur, pen, NEG_INF)

    ones = jnp.ones((2 * ATTN_QB, LANES), BF16)

    def place(d, ti):
        r, n = ti % d, ti // d
        blk = sb * (ATTN_SB // (ATTN_QB * d)) + n
        q_rows = _rows(r + n * (ATTN_QB * d), ATTN_QB, d)
        window = pl.ds(pl.multiple_of(r * (SEQ // d) + blk * ATTN_QB, ATTN_QB), 2 * ATTN_QB)
        return q_rows, window, blk == 0

    def tile(p, d, ti):
        q_rows, window, first = place(d, ti)
        q = (q_ref[q_rows, :] * (ATTN_SCALE * LOG2E)).astype(BF16)
        zero = jnp.zeros_like(q)
        qs = jnp.concatenate([jnp.where(is_h0, q, zero), jnp.where(is_h0, zero, q)], axis=0)
        s = lax.dot_general(qs, kd[p, window, :], (((1,), (1,)), ((), ())),
                            preferred_element_type=F32)
        s = s + bias_s[p, first.astype(jnp.int32)]
        mx = jnp.broadcast_to(jnp.max(s, axis=1, keepdims=True), (2 * ATTN_QB, LANES))
        e = jnp.concatenate(
            [jnp.exp2(s[:, :LANES] - mx), jnp.exp2(s[:, LANES:] - mx)], axis=1).astype(BF16)
        vc = jnp.concatenate([vd[p, window, :], ones], axis=1)
        pv = jnp.dot(e, vc, preferred_element_type=F32)
        acc = jnp.where(is_h0, pv[:ATTN_QB, :LANES], pv[ATTN_QB:, :LANES])
        l = jnp.where(is_h0, pv[:ATTN_QB, LANES:], pv[ATTN_QB:, LANES:])
        m = jnp.where(is_h0, mx[:ATTN_QB], mx[ATTN_QB:])
        out_t = acc / l
        lse_t = m + jnp.log2(l)
        if p > 0:
            out_s[p][q_rows, :] = out_t
            lse_s[p][q_rows, :] = lse_t
        else:
            lse = [lse_t] + [lse_s[j][q_rows, :] for j in range(1, n_p)]
            outs = [out_t] + [out_s[j][q_rows, :] for j in range(1, n_p)]
            top = functools.reduce(jnp.maximum, lse)
            w = [jnp.exp2(x - top) for x in lse]
            num = functools.reduce(jnp.add, [w[j] * outs[j] for j in range(n_p)])
            den = functools.reduce(jnp.add, w)
            o_ref[q_rows, :] = (num / den * _silu(g_ref[q_rows, :])).astype(BF16)

    for p in reversed(range(n_p)):
        for ti in range(ATTN_SB // ATTN_QB):
            tile(p, DILATED_PATTERNS[p][1], ti)


def _alibi_coefs():
    slopes = 2.0 ** (-8.0 * np.arange(1, N_HEADS + 1, dtype=np.float64) / N_HEADS)
    dil = np.array([d for _, d in DILATED_PATTERNS], dtype=np.float64)
    coef = slopes.reshape(-1, HEADS_PER_BLOCK)[:, None, :] * dil[None, :, None] * LOG2E
    return jnp.asarray(coef.reshape(N_HEADS // HEADS_PER_BLOCK, -1), dtype=F32)


def _attention(proj):
    n_hb = ATTN_WIDTH // LANES
    n_p = len(DILATED_PATTERNS)
    return pl.pallas_call(
        _attn_kernel,
        grid=(n_hb, SEQ // ATTN_SB),
        in_specs=[
            pl.BlockSpec((None, ATTN_SB, LANES), lambda h, s: (h, s, 0)),
            pl.BlockSpec((None, SEQ, LANES), lambda h, s: (n_hb + h, 0, 0)),
            pl.BlockSpec((None, SEQ, LANES), lambda h, s: (2 * n_hb + h, 0, 0)),
            pl.BlockSpec((None, ATTN_SB, LANES), lambda h, s: (3 * n_hb + h, s, 0)),
            pl.BlockSpec(memory_space=pltpu.SMEM),
        ],
        out_specs=pl.BlockSpec((ATTN_SB, LANES), lambda h, s: (s, h)),
        out_shape=jax.ShapeDtypeStruct((SEQ, ATTN_WIDTH), BF16),
        scratch_shapes=[
            pltpu.VMEM((n_p, ATTN_QB + SEQ, LANES), BF16),
            pltpu.VMEM((n_p, ATTN_QB + SEQ, LANES), BF16),
            pltpu.VMEM((SEQ, LANES), F32),
            pltpu.VMEM((n_p, 2, HEADS_PER_BLOCK * ATTN_QB, 2 * ATTN_QB), F32),
            pltpu.VMEM((2 * n_p, ATTN_SB + STATE_PAD, LANES), F32),
        ],
        compiler_params=pltpu.CompilerParams(
            dimension_semantics=("arbitrary", "arbitrary"),
            vmem_limit_bytes=ATTN_VMEM_LIMIT_BYTES),
        name="dilated_attn",
    )(proj, proj, proj, proj, _alibi_coefs())


def _lru_kernel(u_ref, g_ref, cw_ref, cb_ref, wr_ref, wi_ref, br_ref, bi_ref, lam_ref, wout_ref,
                o_ref, wout_bf16_ref, ubuf, a_s, b_s, h_s, p_s):
    wout_bf16_ref[...] = wout_ref[...].astype(BF16)

    gi = lax.broadcasted_iota(jnp.int32, (LANES, LANES), 0)
    gj = lax.broadcasted_iota(jnp.int32, (LANES, LANES), 1)
    on_diag = (gi < LRU_BLOCK_W) == (gj < LRU_BLOCK_W)

    def block_diag(w_ref):
        w = w_ref[...]
        return jnp.where(on_diag, jnp.concatenate([w, w], axis=1), 0.0)

    w_gates = (0.5 * jnp.concatenate([block_diag(wr_ref), block_diag(wi_ref)], axis=1)).astype(BF16)
    b_gates = 0.5 * jnp.concatenate([br_ref[...], bi_ref[...]], axis=1)

    lam = lam_ref[...]
    log_sig = jnp.minimum(lam, 0.0) - jnp.log1p(jnp.exp(-jnp.abs(lam)))
    half_c_log_sig = (0.5 * LRU_C) * log_sig
    chunks_per_seg = LRU_SEG_LEN // LRU_CHUNK

    ubuf[pl.ds(0, SUBLANES), :] = jnp.zeros((SUBLANES, LANES), F32)

    def chunk(c, carry):
        rows = pl.ds(pl.multiple_of(c * LRU_CHUNK, LRU_CHUNK), LRU_CHUNK)
        ubuf[pl.ds(SUBLANES, LRU_CHUNK), :] = u_ref[rows, :]
        xc = cb_ref[...]
        for j in range(CONV_WIDTH):
            off = SUBLANES - (CONV_WIDTH - 1) + j
            xc = xc + ubuf[pl.ds(off, LRU_CHUNK), :] * cw_ref[pl.ds(j, 1), :]
        ubuf[pl.ds(0, SUBLANES), :] = ubuf[pl.ds(LRU_CHUNK, SUBLANES), :]

        z = jnp.dot(xc.astype(BF16), w_gates, preferred_element_type=F32) + b_gates
        two_i = jnp.tanh(z[:, LANES:]) + 1.0
        log_a = half_c_log_sig * jnp.tanh(z[:, :LANES]) + half_c_log_sig
        a = jnp.exp(log_a)
        th = jnp.tanh(log_a)
        x4 = -0.5 * th / (1.0 - th)
        half_mult = jnp.where(x4 > 0.0, x4 * lax.rsqrt(x4), 0.0)
        b = (half_mult * xc) * two_i

        seg = lax.shift_right_logical(c, chunks_per_seg.bit_length() - 1)
        within = jnp.bitwise_and(c, chunks_per_seg - 1)
        dst = pl.ds(pl.multiple_of(seg * LRU_SEG_PITCH + within * LRU_CHUNK, SUBLANES), LRU_CHUNK)
        a_s[dst, :] = a
        b_s[dst, :] = b
        return carry

    lax.fori_loop(0, SEQ // LRU_CHUNK, chunk, 0, unroll=True)

    def step(t, carry):
        h, prod = carry
        rows = pl.ds(t, LRU_SEGMENTS, stride=LRU_SEG_PITCH)
        a = a_s[rows, :]
        h = a * h + b_s[rows, :]
        prod = prod * a
        h_s[rows, :] = h
        p_s[rows, :] = prod
        return h, prod

    h_end, p_end = lax.fori_loop(
        0, LRU_SEG_LEN, step,
        (jnp.zeros((LRU_SEGMENTS, LANES), F32), jnp.ones((LRU_SEGMENTS, LANES), F32)), unroll=128)

    seg_id = lax.broadcasted_iota(jnp.int32, (LRU_SEGMENTS, LANES), 0)
    c_in = jnp.zeros((LRU_SEGMENTS, LANES), F32)
    for _ in range(LRU_SEGMENTS - 1):
        c_in = jnp.where(seg_id == 0, 0.0, pltpu.roll(h_end + p_end * c_in, 1, 0))

    for s in range(LRU_SEGMENTS):
        c_s = c_in[s:s + 1, :]

        def gate(k, carry, s=s, c_s=c_s):
            off = pl.multiple_of(k * LRU_CHUNK, LRU_CHUNK)
            src = pl.ds(s * LRU_SEG_PITCH + off, LRU_CHUNK)
            rows = pl.ds(s * LRU_SEG_LEN + off, LRU_CHUNK)
            h = h_s[src, :] + p_s[src, :] * c_s
            o_ref[rows, :] = (h * _silu(g_ref[rows, :])).astype(BF16)
            return carry

        lax.fori_loop(0, chunks_per_seg, gate, 0, unroll=True)


def _lru(proj, conv_w, conv_b, w_rgate, b_rgate, w_igate, b_igate, lam, w_out):
    n_cb = LRU_WIDTH // LANES
    u_col0 = 4 * ATTN_WIDTH // LANES
    g_col0 = u_col0 + n_cb
    return pl.pallas_call(
        _lru_kernel,
        grid=(n_cb,),
        in_specs=[
            pl.BlockSpec((None, SEQ, LANES), lambda j: (u_col0 + j, 0, 0)),
            pl.BlockSpec((None, SEQ, LANES), lambda j: (g_col0 + j, 0, 0)),
            pl.BlockSpec((CONV_WIDTH, LANES), lambda j: (0, j)),
            pl.BlockSpec((1, LANES), lambda j: (0, j)),
            pl.BlockSpec((None, LANES, LRU_BLOCK_W), lambda j: (j, 0, 0)),
            pl.BlockSpec((None, LANES, LRU_BLOCK_W), lambda j: (j, 0, 0)),
            pl.BlockSpec((None, 1, LANES), lambda j: (j, 0, 0)),
            pl.BlockSpec((None, 1, LANES), lambda j: (j, 0, 0)),
            pl.BlockSpec((1, LANES), lambda j: (0, j)),
            pl.BlockSpec((D_MODEL // n_cb, D_MODEL), lambda j: (j, 0)),
        ],
        out_specs=[
            pl.BlockSpec((SEQ, LANES), lambda j: (0, j)),
            pl.BlockSpec((D_MODEL // n_cb, D_MODEL), lambda j: (j, 0)),
        ],
        out_shape=[
            jax.ShapeDtypeStruct((SEQ, LRU_WIDTH), BF16),
            jax.ShapeDtypeStruct((D_MODEL, D_MODEL), BF16),
        ],
        scratch_shapes=[
            pltpu.VMEM((LRU_CHUNK + SUBLANES, LANES), F32),
        ] + [pltpu.VMEM((LRU_SEGMENTS * LRU_SEG_PITCH, LANES), F32)] * 4,
        compiler_params=pltpu.CompilerParams(
            dimension_semantics=("arbitrary",), vmem_limit_bytes=VMEM_LIMIT_BYTES),
        name="rglru",
    )(proj, proj, conv_w, conv_b,
      w_rgate.reshape(n_cb, LANES, LRU_BLOCK_W), w_igate.reshape(n_cb, LANES, LRU_BLOCK_W),
      b_rgate.reshape(n_cb, 1, LANES), b_igate.reshape(n_cb, 1, LANES), lam, w_out)


def _out_kernel(ma_ref, ml_ref, w_ref, x_ref, gate_ref, fg_ref, o_ref, mix_even, mix_odd):
    s = pl.program_id(0)
    n_tiles = pl.num_programs(0) - 1

    def multiply(mix_ref):
        mixed = jnp.concatenate([ma_ref[...], ml_ref[...]], axis=1)
        mix_ref[...] = jnp.dot(mixed, w_ref[...], preferred_element_type=F32)

    def finish(mix_ref):
        y = x_ref[...] + gate_ref[...] * mix_ref[...]
        var = jnp.mean(y * y, axis=-1, keepdims=True)
        o_ref[...] = y * lax.rsqrt(var + EPS) * fg_ref[...]

    is_even = jnp.bitwise_and(s, 1) == 0

    @pl.when(s == 0)
    def _():
        multiply(mix_even)

    @pl.when(jnp.logical_and(is_even, jnp.logical_and(s > 0, s < n_tiles)))
    def _():
        multiply(mix_even)
        finish(mix_odd)

    @pl.when(jnp.logical_not(is_even))
    def _():
        multiply(mix_odd)
        finish(mix_even)

    @pl.when(s == n_tiles)
    def _():
        finish(mix_odd)


def _out_proj(mixed_attn, mixed_lru, w_out_bf16, x2d, mod, final_gain):
    n_tiles = SEQ // OUT_TM
    assert n_tiles % 2 == 0
    cur = lambda s: (jnp.minimum(s, n_tiles - 1), 0)
    prev = lambda s: (jnp.maximum(s - 1, 0), 0)
    return pl.pallas_call(
        _out_kernel,
        grid=(n_tiles + 1,),
        in_specs=[
            pl.BlockSpec((OUT_TM, ATTN_WIDTH), cur),
            pl.BlockSpec((OUT_TM, LRU_WIDTH), cur),
            pl.BlockSpec((D_MODEL, D_MODEL), lambda s: (0, 0)),
            pl.BlockSpec((OUT_TM, D_MODEL), prev),
            pl.BlockSpec((1, D_MODEL), lambda s: (0, MOD_GATE)),
            pl.BlockSpec((1, D_MODEL), lambda s: (0, 0)),
        ],
        out_specs=pl.BlockSpec((OUT_TM, D_MODEL), prev),
        out_shape=jax.ShapeDtypeStruct((SEQ, D_MODEL), F32),
        scratch_shapes=[pltpu.VMEM((OUT_TM, D_MODEL), F32)] * 2,
        compiler_params=pltpu.CompilerParams(
            dimension_semantics=("arbitrary",), vmem_limit_bytes=VMEM_LIMIT_BYTES),
        name="out_proj",
    )(mixed_attn, mixed_lru, w_out_bf16, x2d, mod, final_gain)


@jax.jit
def kernel(x, c, norm_gain, w_ada, b_ada, w_in, conv_w, conv_b, w_rgate, b_rgate,
           w_igate, b_igate, lru_lambda, w_out, final_gain):
    assert x.shape == (1, SEQ, D_MODEL) and norm_gain.shape[0] == 1
    x2d = x.reshape(SEQ, D_MODEL)
    mod = _ada_mod(c, w_ada[0], b_ada)
    proj = _in_proj(x2d, norm_gain, mod, w_in[0])
    mixed_attn = _attention(proj)
    mixed_lru, w_out_bf16 = _lru(proj, conv_w[0], conv_b, w_rgate[0], b_rgate[0], w_igate[0],
                                 b_igate[0], lru_lambda, w_out[0])
    y = _out_proj(mixed_attn, mixed_lru, w_out_bf16, x2d, mod,
                  final_gain.reshape(1, D_MODEL))
    return y.reshape(1, SEQ, D_MODEL)
```

```python
import functools
import math

import jax
import jax.numpy as jnp
import numpy as np
from jax import lax
from jax.experimental import pallas as pl
from jax.experimental.pallas import tpu as pltpu

D_MODEL = 2048
SEQ = 8192
ATTN_WIDTH = D_MODEL // 2
LRU_WIDTH = D_MODEL - ATTN_WIDTH
HEAD_DIM = 64
N_HEADS = ATTN_WIDTH // HEAD_DIM
LRU_BLOCKS = 16
LRU_BLOCK_W = LRU_WIDTH // LRU_BLOCKS
CONV_WIDTH = 4
LRU_C = 8.0
DILATED_PATTERNS = ((128, 1), (512, 4), (2048, 16))
ATTN_SCALE = 1.0 / math.sqrt(HEAD_DIM)
NEG_INF = -1e30
LOG2E = math.log2(math.e)
EPS = 1e-6
PROJ_WIDTH = 4 * ATTN_WIDTH + 2 * LRU_WIDTH
MOD_SHIFT, MOD_SCALE, MOD_GATE = 0, 1, 2

LANES = 128
SUBLANES = 8
VMEM_LIMIT_BYTES = 56 * 1024 * 1024

ADA_TK = 256
PROJ_TM = 256
PROJ_W_CHUNK = 512
PROJ_W_SLOTS = 3
PROJ_VMEM_LIMIT_BYTES = 60 * 1024 * 1024
NORM_ROWS = 64
ATTN_QB = 128
ATTN_SB = 2048
DEINT_ROWS = 256
DEINT_RATIO = 4
ATTN_VMEM_LIMIT_BYTES = 60 * 1024 * 1024
HEADS_PER_BLOCK = LANES // HEAD_DIM
STATE_PAD = SUBLANES
LRU_CHUNK = 1024
LRU_SEGMENTS = SUBLANES
LRU_SEG_LEN = SEQ // LRU_SEGMENTS
LRU_SEG_PITCH = LRU_SEG_LEN + SUBLANES
OUT_TM = 512

F32 = jnp.float32
BF16 = jnp.bfloat16


def _silu(x):
    h = 0.5 * x
    return h + h * jnp.tanh(h)


def _ada_kernel(c_ref, w_ref, b_ref, o_ref):
    @pl.when(pl.program_id(0) == 0)
    def _():
        o_ref[...] = b_ref[...]

    c_act = jnp.transpose(_silu(c_ref[...]))
    o_ref[...] += jnp.sum(w_ref[...] * c_act, axis=0, keepdims=True)


def _ada_mod(c_col, w_ada, b_ada):
    n = w_ada.shape[1]
    return pl.pallas_call(
        _ada_kernel,
        grid=(D_MODEL // ADA_TK,),
        in_specs=[
            pl.BlockSpec((1, ADA_TK), lambda k: (0, k)),
            pl.BlockSpec((ADA_TK, n), lambda k: (k, 0)),
            pl.BlockSpec((1, n), lambda k: (0, 0)),
        ],
        out_specs=pl.BlockSpec((1, n), lambda k: (0, 0)),
        out_shape=jax.ShapeDtypeStruct((1, n), F32),
        compiler_params=pltpu.CompilerParams(
            dimension_semantics=("arbitrary",), vmem_limit_bytes=VMEM_LIMIT_BYTES),
        name="ada_mod",
    )(c_col, w_ada, b_ada)


def _w_chunk_copy(w_hbm, stage, sem, c):
    slot = c % PROJ_W_SLOTS
    cols = pl.ds(c * PROJ_W_CHUNK, PROJ_W_CHUNK)
    return pltpu.make_async_copy(w_hbm.at[:, cols], stage.at[slot], sem.at[slot])


def _proj_kernel(x0_ref, xn_ref, gain_ref, shift_ref, scale_ref, w_hbm, o_ref,
                 w_res, stage, h_even, h_odd, sem):
    s = pl.program_id(0)
    n_chunks = PROJ_WIDTH // PROJ_W_CHUNK
    slabs_per_chunk = PROJ_W_CHUNK // LANES

    def normalise(x_ref, h_ref):
        mul = gain_ref[...] * (1.0 + scale_ref[...])
        shift = shift_ref[...]
        for r0 in range(0, PROJ_TM, NORM_ROWS):
            xv = x_ref[pl.ds(r0, NORM_ROWS), :]
            var = jnp.mean(xv * xv, axis=-1, keepdims=True)
            h = xv * lax.rsqrt(var + EPS) * mul + shift
            h_ref[pl.ds(r0, NORM_ROWS), :] = h.astype(BF16)

    def project(h_ref, first_slab, n_slabs):
        cols = pl.ds(first_slab * LANES, n_slabs * LANES)
        acc = jnp.dot(h_ref[...], w_res[:, cols], preferred_element_type=F32)
        for b in range(n_slabs):
            o_ref[first_slab + b] = acc[:, b * LANES:(b + 1) * LANES]

    @pl.when(s == 0)
    def _():
        for c in range(PROJ_W_SLOTS):
            _w_chunk_copy(w_hbm, stage, sem, c).start()
        normalise(x0_ref, h_even)
        normalise(xn_ref, h_odd)
        for c in range(n_chunks):
            _w_chunk_copy(w_hbm, stage, sem, c).wait()
            cols = pl.ds(c * PROJ_W_CHUNK, PROJ_W_CHUNK)
            w_res[:, cols] = stage[c % PROJ_W_SLOTS].astype(BF16)
            if c + PROJ_W_SLOTS < n_chunks:
                _w_chunk_copy(w_hbm, stage, sem, c + PROJ_W_SLOTS).start()
            project(h_even, c * slabs_per_chunk, slabs_per_chunk)

    is_even = jnp.bitwise_and(s, 1) == 0

    @pl.when(jnp.logical_and(s > 0, is_even))
    def _():
        normalise(xn_ref, h_odd)
        project(h_even, 0, PROJ_WIDTH // LANES)

    @pl.when(jnp.logical_not(is_even))
    def _():
        normalise(xn_ref, h_even)
        project(h_odd, 0, PROJ_WIDTH // LANES)


def _in_proj(x2d, gain, mod, w_in):
    n_tiles = SEQ // PROJ_TM
    return pl.pallas_call(
        _proj_kernel,
        grid=(n_tiles,),
        in_specs=[
            pl.BlockSpec((PROJ_TM, D_MODEL), lambda s: (0, 0), pipeline_mode=pl.Buffered(1)),
            pl.BlockSpec((PROJ_TM, D_MODEL), lambda s: (jnp.minimum(s + 1, n_tiles - 1), 0)),
            pl.BlockSpec((1, D_MODEL), lambda s: (0, 0)),
            pl.BlockSpec((1, D_MODEL), lambda s: (0, MOD_SHIFT)),
            pl.BlockSpec((1, D_MODEL), lambda s: (0, MOD_SCALE)),
            pl.BlockSpec(memory_space=pl.ANY),
        ],
        out_specs=pl.BlockSpec((PROJ_WIDTH // LANES, PROJ_TM, LANES), lambda s: (0, s, 0)),
        out_shape=jax.ShapeDtypeStruct((PROJ_WIDTH // LANES, SEQ, LANES), F32),
        scratch_shapes=[
            pltpu.VMEM((D_MODEL, PROJ_WIDTH), BF16),
            pltpu.VMEM((PROJ_W_SLOTS, D_MODEL, PROJ_W_CHUNK), F32),
            pltpu.VMEM((PROJ_TM, D_MODEL), BF16),
            pltpu.VMEM((PROJ_TM, D_MODEL), BF16),
            pltpu.SemaphoreType.DMA((PROJ_W_SLOTS,)),
        ],
        compiler_params=pltpu.CompilerParams(
            dimension_semantics=("arbitrary",), vmem_limit_bytes=PROJ_VMEM_LIMIT_BYTES),
        name="in_proj",
    )(x2d, x2d, gain, mod, mod, w_in)


def _rows(start, size, stride):
    return pl.ds(start, size) if stride == 1 else pl.ds(start, size, stride=stride)


def _attn_kernel(q_ref, k_ref, v_ref, g_ref, coef_ref, o_ref,
                 kd, vd, tmp, bias_s, state):
    sb = pl.program_id(1)
    n_p = len(DILATED_PATTERNS)
    assert DILATED_PATTERNS[0][1] == 1
    lse_s = [None] + [state.at[p] for p in range(1, n_p)]
    out_s = [None] + [state.at[n_p + p] for p in range(1, n_p)]
    lane = lax.broadcasted_iota(jnp.int32, (ATTN_QB, LANES), 1)
    is_h0 = lane < HEAD_DIM

    @pl.when(sb == 0)
    def _():
        for src_ref, dst in ((k_ref, kd), (v_ref, vd)):
            for p, (_, d) in enumerate(DILATED_PATTERNS):
                dst[p, pl.ds(0, ATTN_QB), :] = jnp.zeros((ATTN_QB, LANES), BF16)
                d_prev = DILATED_PATTERNS[p - 1][1] if p else 1
                ratio = d // d_prev
                assert p == 0 and d == 1 or ratio == DEINT_RATIO
                per_residue = SEQ // d // DEINT_ROWS
                keep_f32 = 0 < p < len(DILATED_PATTERNS) - 1
                from_ref = src_ref if p <= 1 else tmp

                def deint(c, carry, p=p, d_prev=d_prev, ratio=ratio, per_residue=per_residue,
                          keep_f32=keep_f32, from_ref=from_ref, dst=dst):
                    r = lax.shift_right_logical(c, per_residue.bit_length() - 1)
                    chunk = jnp.bitwise_and(c, per_residue - 1)
                    r_prev = jnp.bitwise_and(r, d_prev - 1)
                    j = lax.shift_right_logical(r, d_prev.bit_length() - 1)
                    start = r_prev * (SEQ // d_prev) + chunk * (DEINT_ROWS * ratio) + j
                    x = from_ref[_rows(start, DEINT_ROWS, ratio), :]
                    if keep_f32:
                        tmp[pl.ds(pl.multiple_of(c * DEINT_ROWS, DEINT_ROWS), DEINT_ROWS), :] = x
                    rows = pl.ds(pl.multiple_of(ATTN_QB + c * DEINT_ROWS, ATTN_QB), DEINT_ROWS)
                    dst[p, rows, :] = x.astype(BF16)
                    return carry

                lax.fori_loop(0, SEQ // DEINT_ROWS, deint, 0, unroll=4)

        qi = lax.broadcasted_iota(jnp.int32, (ATTN_QB, 2 * ATTN_QB), 0)
        ki = lax.broadcasted_iota(jnp.int32, (ATTN_QB, 2 * ATTN_QB), 1)
        dist = qi + ATTN_QB - ki
        in_band = jnp.logical_and(dist >= 0, dist <= ATTN_QB)
        in_band_cur = jnp.logical_and(in_band, ki >= ATTN_QB)
        dist_f = dist.astype(F32)
        for p in range(len(DILATED_PATTERNS)):
            for h in range(HEADS_PER_BLOCK):
                pen = -(coef_ref[pl.program_id(0), p * HEADS_PER_BLOCK + h] * dist_f)
                rows = pl.ds(h * ATTN_QB, ATTN_QB)
                bias_s[p, 0, rows, :] = jnp.where(in_band, pen, NEG_INF)
                bias_s[p, 1, rows, :] = jnp.where(in_band_cur, pen, NEG_INF)

    ones = jnp.ones((2 * ATTN_QB, LANES), BF16)

    def place(d, ti):
        r, n = ti % d, ti // d
        blk = sb * (ATTN_SB // (ATTN_QB * d)) + n
        q_rows = _rows(r + n * (ATTN_QB * d), ATTN_QB, d)
        window = pl.ds(pl.multiple_of(r * (SEQ // d) + blk * ATTN_QB, ATTN_QB), 2 * ATTN_QB)
        return q_rows, window, blk == 0

    def tile(p, d, ti):
        q_rows, window, first = place(d, ti)
        q = (q_ref[q_rows, :] * (ATTN_SCALE * LOG2E)).astype(BF16)
        zero = jnp.zeros_like(q)
        qs = jnp.concatenate([jnp.where(is_h0, q, zero), jnp.where(is_h0, zero, q)], axis=0)
        s = lax.dot_general(qs, kd[p, window, :], (((1,), (1,)), ((), ())),
                            preferred_element_type=F32)
        s = s + bias_s[p, first.astype(jnp.int32)]
        mx = jnp.broadcast_to(jnp.max(s, axis=1, keepdims=True), (2 * ATTN_QB, LANES))
        e = jnp.concatenate(
            [jnp.exp2(s[:, :LANES] - mx), jnp.exp2(s[:, LANES:] - mx)], axis=1).astype(BF16)
        vc = jnp.concatenate([vd[p, window, :], ones], axis=1)
        pv = jnp.dot(e, vc, preferred_element_type=F32)
        acc = jnp.where(is_h0, pv[:ATTN_QB, :LANES], pv[ATTN_QB:, :LANES])
        l = jnp.where(is_h0, pv[:ATTN_QB, LANES:], pv[ATTN_QB:, LANES:])
        m = jnp.where(is_h0, mx[:ATTN_QB], mx[ATTN_QB:])
        out_t = acc / l
        lse_t = m + jnp.log2(l)
        if p > 0:
            out_s[p][q_rows, :] = out_t
            lse_s[p][q_rows, :] = lse_t
        else:
            lse = [lse_t] + [lse_s[j][q_rows, :] for j in range(1, n_p)]
            outs = [out_t] + [out_s[j][q_rows, :] for j in range(1, n_p)]
            top = functools.reduce(jnp.maximum, lse)
            w = [jnp.exp2(x - top) for x in lse]
            num = functools.reduce(jnp.add, [w[j] * outs[j] for j in range(n_p)])
            den = functools.reduce(jnp.add, w)
            o_ref[q_rows, :] = (num / den * _silu(g_ref[q_rows, :])).astype(BF16)

    for p in reversed(range(n_p)):
        for ti in range(ATTN_SB // ATTN_QB):
            tile(p, DILATED_PATTERNS[p][1], ti)


def _alibi_coefs():
    slopes = 2.0 ** (-8.0 * np.arange(1, N_HEADS + 1, dtype=np.float64) / N_HEADS)
    dil = np.array([d for _, d in DILATED_PATTERNS], dtype=np.float64)
    coef = slopes.reshape(-1, HEADS_PER_BLOCK)[:, None, :] * dil[None, :, None] * LOG2E
    return jnp.asarray(coef.reshape(N_HEADS // HEADS_PER_BLOCK, -1), dtype=F32)


def _attention(proj):
    n_hb = ATTN_WIDTH // LANES
    n_p = len(DILATED_PATTERNS)
    return pl.pallas_call(
        _attn_kernel,
        grid=(n_hb, SEQ // ATTN_SB),
        in_specs=[
            pl.BlockSpec((None, ATTN_SB, LANES), lambda h, s: (h, s, 0)),
            pl.BlockSpec((None, SEQ, LANES), lambda h, s: (n_hb + h, 0, 0)),
            pl.BlockSpec((None, SEQ, LANES), lambda h, s: (2 * n_hb + h, 0, 0)),
            pl.BlockSpec((None, ATTN_SB, LANES), lambda h, s: (3 * n_hb + h, s, 0)),
            pl.BlockSpec(memory_space=pltpu.SMEM),
        ],
        out_specs=pl.BlockSpec((ATTN_SB, LANES), lambda h, s: (s, h)),
        out_shape=jax.ShapeDtypeStruct((SEQ, ATTN_WIDTH), BF16),
        scratch_shapes=[
            pltpu.VMEM((n_p, ATTN_QB + SEQ, LANES), BF16),
            pltpu.VMEM((n_p, ATTN_QB + SEQ, LANES), BF16),
            pltpu.VMEM((SEQ, LANES), F32),
            pltpu.VMEM((n_p, 2, HEADS_PER_BLOCK * ATTN_QB, 2 * ATTN_QB), F32),
            pltpu.VMEM((2 * n_p, ATTN_SB + STATE_PAD, LANES), F32),
        ],
        compiler_params=pltpu.CompilerParams(
            dimension_semantics=("arbitrary", "arbitrary"),
            vmem_limit_bytes=ATTN_VMEM_LIMIT_BYTES),
        name="dilated_attn",
    )(proj, proj, proj, proj, _alibi_coefs())


def _lru_kernel(u_ref, g_ref, cw_ref, cb_ref, wr_ref, wi_ref, br_ref, bi_ref, lam_ref, wout_ref,
                o_ref, wout_bf16_ref, ubuf, a_s, b_s, h_s, p_s):
    wout_bf16_ref[...] = wout_ref[...].astype(BF16)

    gi = lax.broadcasted_iota(jnp.int32, (LANES, LANES), 0)
    gj = lax.broadcasted_iota(jnp.int32, (LANES, LANES), 1)
    on_diag = (gi < LRU_BLOCK_W) == (gj < LRU_BLOCK_W)

    def block_diag(w_ref):
        w = w_ref[...]
        return jnp.where(on_diag, jnp.concatenate([w, w], axis=1), 0.0)

    w_gates = (0.5 * jnp.concatenate([block_diag(wr_ref), block_diag(wi_ref)], axis=1)).astype(BF16)
    b_gates = 0.5 * jnp.concatenate([br_ref[...], bi_ref[...]], axis=1)

    lam = lam_ref[...]
    log_sig = jnp.minimum(lam, 0.0) - jnp.log1p(jnp.exp(-jnp.abs(lam)))
    half_c_log_sig = (0.5 * LRU_C) * log_sig
    chunks_per_seg = LRU_SEG_LEN // LRU_CHUNK

    ubuf[pl.ds(0, SUBLANES), :] = jnp.zeros((SUBLANES, LANES), F32)

    def chunk(c, carry):
        rows = pl.ds(pl.multiple_of(c * LRU_CHUNK, LRU_CHUNK), LRU_CHUNK)
        ubuf[pl.ds(SUBLANES, LRU_CHUNK), :] = u_ref[rows, :]
        xc = cb_ref[...]
        for j in range(CONV_WIDTH):
            off = SUBLANES - (CONV_WIDTH - 1) + j
            xc = xc + ubuf[pl.ds(off, LRU_CHUNK), :] * cw_ref[pl.ds(j, 1), :]
        ubuf[pl.ds(0, SUBLANES), :] = ubuf[pl.ds(LRU_CHUNK, SUBLANES), :]

        z = jnp.dot(xc.astype(BF16), w_gates, preferred_element_type=F32) + b_gates
        two_i = jnp.tanh(z[:, LANES:]) + 1.0
        log_a = half_c_log_sig * jnp.tanh(z[:, :LANES]) + half_c_log_sig
        a = jnp.exp(log_a)
        th = jnp.tanh(log_a)
        x4 = -0.5 * th / (1.0 - th)
        half_mult = jnp.where(x4 > 0.0, x4 * lax.rsqrt(x4), 0.0)
        b = (half_mult * xc) * two_i

        seg = lax.shift_right_logical(c, chunks_per_seg.bit_length() - 1)
        within = jnp.bitwise_and(c, chunks_per_seg - 1)
        dst = pl.ds(pl.multiple_of(seg * LRU_SEG_PITCH + within * LRU_CHUNK, SUBLANES), LRU_CHUNK)
        a_s[dst, :] = a
        b_s[dst, :] = b
        return carry

    lax.fori_loop(0, SEQ // LRU_CHUNK, chunk, 0, unroll=True)

    def step(t, carry):
        h, prod = carry
        rows = pl.ds(t, LRU_SEGMENTS, stride=LRU_SEG_PITCH)
        a = a_s[rows, :]
        h = a * h + b_s[rows, :]
        prod = prod * a
        h_s[rows, :] = h
        p_s[rows, :] = prod
        return h, prod

    h_end, p_end = lax.fori_loop(
        0, LRU_SEG_LEN, step,
        (jnp.zeros((LRU_SEGMENTS, LANES), F32), jnp.ones((LRU_SEGMENTS, LANES), F32)), unroll=128)

    seg_id = lax.broadcasted_iota(jnp.int32, (LRU_SEGMENTS, LANES), 0)
    c_in = jnp.zeros((LRU_SEGMENTS, LANES), F32)
    for _ in range(LRU_SEGMENTS - 1):
        c_in = jnp.where(seg_id == 0, 0.0, pltpu.roll(h_end + p_end * c_in, 1, 0))

    for s in range(LRU_SEGMENTS):
        c_s = c_in[s:s + 1, :]

        def gate(k, carry, s=s, c_s=c_s):
            off = pl.multiple_of(k * LRU_CHUNK, LRU_CHUNK)
            src = pl.ds(s * LRU_SEG_PITCH + off, LRU_CHUNK)
            rows = pl.ds(s * LRU_SEG_LEN + off, LRU_CHUNK)
            h = h_s[src, :] + p_s[src, :] * c_s
            o_ref[rows, :] = (h * _silu(g_ref[rows, :])).astype(BF16)
            return carry

        lax.fori_loop(0, chunks_per_seg, gate, 0, unroll=True)


def _lru(proj, conv_w, conv_b, w_rgate, b_rgate, w_igate, b_igate, lam, w_out):
    n_cb = LRU_WIDTH // LANES
    u_col0 = 4 * ATTN_WIDTH // LANES
    g_col0 = u_col0 + n_cb
    return pl.pallas_call(
        _lru_kernel,
        grid=(n_cb,),
        in_specs=[
            pl.BlockSpec((None, SEQ, LANES), lambda j: (u_col0 + j, 0, 0)),
            pl.BlockSpec((None, SEQ, LANES), lambda j: (g_col0 + j, 0, 0)),
            pl.BlockSpec((CONV_WIDTH, LANES), lambda j: (0, j)),
            pl.BlockSpec((1, LANES), lambda j: (0, j)),
            pl.BlockSpec((None, LANES, LRU_BLOCK_W), lambda j: (j, 0, 0)),
            pl.BlockSpec((None, LANES, LRU_BLOCK_W), lambda j: (j, 0, 0)),
            pl.BlockSpec((None, 1, LANES), lambda j: (j, 0, 0)),
            pl.BlockSpec((None, 1, LANES), lambda j: (j, 0, 0)),
            pl.BlockSpec((1, LANES), lambda j: (0, j)),
            pl.BlockSpec((D_MODEL // n_cb, D_MODEL), lambda j: (j, 0)),
        ],
        out_specs=[
            pl.BlockSpec((SEQ, LANES), lambda j: (0, j)),
            pl.BlockSpec((D_MODEL // n_cb, D_MODEL), lambda j: (j, 0)),
        ],
        out_shape=[
            jax.ShapeDtypeStruct((SEQ, LRU_WIDTH), BF16),
            jax.ShapeDtypeStruct((D_MODEL, D_MODEL), BF16),
        ],
        scratch_shapes=[
            pltpu.VMEM((LRU_CHUNK + SUBLANES, LANES), F32),
        ] + [pltpu.VMEM((LRU_SEGMENTS * LRU_SEG_PITCH, LANES), F32)] * 4,
        compiler_params=pltpu.CompilerParams(
            dimension_semantics=("arbitrary",), vmem_limit_bytes=VMEM_LIMIT_BYTES),
        name="rglru",
    )(proj, proj, conv_w, conv_b,
      w_rgate.reshape(n_cb, LANES, LRU_BLOCK_W), w_igate.reshape(n_cb, LANES, LRU_BLOCK_W),
      b_rgate.reshape(n_cb, 1, LANES), b_igate.reshape(n_cb, 1, LANES), lam, w_out)


def _out_kernel(ma_ref, ml_ref, w_ref, x_ref, gate_ref, fg_ref, o_ref):
    mixed = jnp.concatenate([ma_ref[...], ml_ref[...]], axis=1)
    mix = jnp.dot(mixed, w_ref[...], preferred_element_type=F32)
    y = x_ref[...] + gate_ref[...] * mix
    var = jnp.mean(y * y, axis=-1, keepdims=True)
    o_ref[...] = y * lax.rsqrt(var + EPS) * fg_ref[...]


def _out_proj(mixed_attn, mixed_lru, w_out_bf16, x2d, mod, final_gain):
    return pl.pallas_call(
        _out_kernel,
        grid=(SEQ // OUT_TM,),
        in_specs=[
            pl.BlockSpec((OUT_TM, ATTN_WIDTH), lambda i: (i, 0)),
            pl.BlockSpec((OUT_TM, LRU_WIDTH), lambda i: (i, 0)),
            pl.BlockSpec((D_MODEL, D_MODEL), lambda i: (0, 0)),
            pl.BlockSpec((OUT_TM, D_MODEL), lambda i: (i, 0)),
            pl.BlockSpec((1, D_MODEL), lambda i: (0, MOD_GATE)),
            pl.BlockSpec((1, D_MODEL), lambda i: (0, 0)),
        ],
        out_specs=pl.BlockSpec((OUT_TM, D_MODEL), lambda i: (i, 0)),
        out_shape=jax.ShapeDtypeStruct((SEQ, D_MODEL), F32),
        compiler_params=pltpu.CompilerParams(
            dimension_semantics=("arbitrary",), vmem_limit_bytes=VMEM_LIMIT_BYTES),
        name="out_proj",
    )(mixed_attn, mixed_lru, w_out_bf16, x2d, mod, final_gain)


@jax.jit
def kernel(x, c, norm_gain, w_ada, b_ada, w_in, conv_w, conv_b, w_rgate, b_rgate,
           w_igate, b_igate, lru_lambda, w_out, final_gain):
    assert x.shape == (1, SEQ, D_MODEL) and norm_gain.shape[0] == 1
    x2d = x.reshape(SEQ, D_MODEL)
    mod = _ada_mod(c, w_ada[0], b_ada)
    proj = _in_proj(x2d, norm_gain, mod, w_in[0])
    mixed_attn = _attention(proj)
    mixed_lru, w_out_bf16 = _lru(proj, conv_w[0], conv_b, w_rgate[0], b_rgate[0], w_igate[0],
                                 b_igate[0], lru_lambda, w_out[0])
    y = _out_proj(mixed_attn, mixed_lru, w_out_bf16, x2d, mod,
                  final_gain.reshape(1, D_MODEL))
    return y.reshape(1, SEQ, D_MODEL)
```
